```python
import jax, jax.numpy as jnp
from jax import lax
import numpy as np

D_MODEL = 2048
BATCH = 4
SEQ = 4096
DEPTH = 4

N_META = 16
N_MIXERS = 2
N_RWKV = (DEPTH + 1) // 2
N_GLA = DEPTH // 2
D_FF = 4 * D_MODEL
NORM_EPS = 1e-6

RW_HEAD = 64
RW_HEADS = D_MODEL // RW_HEAD
RW_DECAY_LORA = 96
RW_AAA_LORA = 96
RW_MV_LORA = 64
RW_GATE_LORA = 256
RW_GN_EPS = 64e-5

GLA_HEADS = 4
GLA_DK = D_MODEL // 2
GLA_DV = D_MODEL
GLA_HK = GLA_DK // GLA_HEADS
GLA_HV = GLA_DV // GLA_HEADS
GLA_GATE_LORA = 16
GLA_TAU = 16.0
GLA_CHUNK = 64
GLA_HEAD_EPS = 1e-5
GLA_IN = 2 * GLA_DK + 2 * GLA_DV + GLA_GATE_LORA

kernel_name = 'meta_rwkv7_gla_sqrelu_hybrid'


def rms_norm(x, g, eps=NORM_EPS):
    xf = x.astype(jnp.float32)
    y = xf * lax.rsqrt(jnp.mean(xf * xf, axis=-1, keepdims=True) + eps)
    return (y * g.astype(jnp.float32)).astype(x.dtype)


def sq_relu_mlp(x, w1, w2):
    h = jax.nn.relu(x @ w1)
    return (h * h) @ w2


def rwkv7_scan(r, w, k, v, a, b):
    B, T, H, N = r.shape

    def step(S, inp):
        r_t, w_t, k_t, v_t, a_t, b_t = inp
        sa = jnp.einsum('bhvk,bhk->bhv', S, a_t)
        S = S * w_t[:, :, None, :] + sa[..., None] * b_t[:, :, None, :] + v_t[..., None] * k_t[:, :, None, :]
        y = jnp.einsum('bhvk,bhk->bhv', S, r_t)
        return S, y

    xs = tuple(jnp.moveaxis(t, 1, 0) for t in (r, w, k, v, a, b))
    S0 = jnp.zeros((B, H, N, N), jnp.float32)
    _, y = lax.scan(step, S0, xs)
    return jnp.moveaxis(y, 0, 1)


def rwkv7_time_mix(x, v_first, mix, w_rkv, w0, w1, w2, a0, a1, a2, g1, g2, k_k, k_a, r_k,
                   ln_w, ln_b, w_o, vres):
    B, T, D = x.shape
    xx = jnp.pad(x, ((0, 0), (1, 0), (0, 0)))[:, :-1] - x
    xr, xw, xk, xv, xa, xg = (x + xx * mix[i] for i in range(6))
    r = xr @ w_rkv[0]
    k = xk @ w_rkv[1]
    v = xv @ w_rkv[2]
    if vres is None:
        v_first = v
    else:
        v0, v1, v2 = vres
        v = v + (v_first - v) * jax.nn.sigmoid(v0 + (xv @ v1) @ v2)
    w_log = -jax.nn.softplus(-(w0 + jnp.tanh(xw @ w1) @ w2)) - 0.5
    a = jax.nn.sigmoid(a0 + (xa @ a1) @ a2)
    g = jax.nn.sigmoid(xg @ g1) @ g2
    hs = lambda t: t.reshape(B, T, RW_HEADS, RW_HEAD).astype(jnp.float32)
    kk = hs(k * k_k)
    kk = kk / jnp.maximum(jnp.sqrt(jnp.sum(kk * kk, axis=-1, keepdims=True)), 1e-12)
    k = k * (1.0 + (a - 1.0) * k_a)
    rh, kh, vh, ah = hs(r), hs(k), hs(v), hs(a)
    decay = jnp.exp(-jnp.exp(hs(w_log)))
    y = rwkv7_scan(rh, decay, kh, vh, -kk, kk * ah)
    mu = jnp.mean(y, axis=-1, keepdims=True)
    var = jnp.mean(jnp.square(y - mu), axis=-1, keepdims=True)
    y = ((y - mu) * lax.rsqrt(var + RW_GN_EPS)).reshape(B, T, D) * ln_w + ln_b
    bonus = (jnp.sum(rh * kh * r_k, axis=-1, keepdims=True) * vh).reshape(B, T, D)
    out = ((y + bonus).astype(x.dtype) * g) @ w_o
    return out, v_first


def gla_chunk(S, q, k, v, g):
    L = q.shape[2]
    b = jnp.cumsum(g, axis=2)
    b_last = b[:, :, -1:, :]
    q_t = q * jnp.exp(b)
    k_t = k * jnp.exp(-b)
    causal = jnp.tril(jnp.ones((L, L), dtype=bool))
    A = jnp.where(causal, jnp.einsum('bhid,bhjd->bhij', q_t, k_t), 0.0)
    o = jnp.einsum('bhid,bhdv->bhiv', q_t, S) + jnp.einsum('bhij,bhjv->bhiv', A, v)
    S_new = jnp.exp(b_last)[:, :, 0, :, None] * S + jnp.einsum('bhjd,bhjv->bhdv', k * jnp.exp(b_last - b), v)
    return S_new, o


def gla_time_mix(x, w_in, w_a2, b_a, gn_w, w_o):
    B, T, D = x.shape
    p = x @ w_in
    q = p[..., :GLA_DK]
    k = p[..., GLA_DK:2 * GLA_DK]
    v = p[..., 2 * GLA_DK:2 * GLA_DK + GLA_DV]
    gate = p[..., 2 * GLA_DK + GLA_DV:2 * GLA_DK + 2 * GLA_DV]
    za = p[..., 2 * GLA_DK + 2 * GLA_DV:]
    glog = jax.nn.log_sigmoid((za @ w_a2 + b_a).astype(jnp.float32)) / GLA_TAU
    heads = lambda t, d: t.reshape(B, T, GLA_HEADS, d).transpose(0, 2, 1, 3).astype(jnp.float32)
    qh = heads(q, GLA_HK) * (GLA_HK ** -0.5)
    kh, gh, vh = heads(k, GLA_HK), heads(glog, GLA_HK), heads(v, GLA_HV)
    S0 = jnp.zeros((B, GLA_HEADS, GLA_HK, GLA_HV), jnp.float32)
    S1, o_meta = gla_chunk(S0, qh[:, :, :N_META], kh[:, :, :N_META], vh[:, :, :N_META], gh[:, :, :N_META])
    n_c = (T - N_META) // GLA_CHUNK
    to_chunks = lambda t: jnp.moveaxis(t[:, :, N_META:].reshape(B, GLA_HEADS, n_c, GLA_CHUNK, t.shape[-1]), 2, 0)
    _, o_real = lax.scan(lambda S, c: gla_chunk(S, *c), S1,
                         (to_chunks(qh), to_chunks(kh), to_chunks(vh), to_chunks(gh)))
    o_real = jnp.moveaxis(o_real, 0, 2).reshape(B, GLA_HEADS, T - N_META, GLA_HV)
    o = jnp.concatenate([o_meta, o_real], axis=2)
    o = o * lax.rsqrt(jnp.mean(o * o, axis=-1, keepdims=True) + GLA_HEAD_EPS)
    o = o * gn_w.reshape(GLA_HEADS, 1, GLA_HV)
    o = o.transpose(0, 2, 1, 3).reshape(B, T, GLA_DV).astype(x.dtype)
    return (o * jax.nn.silu(gate)) @ w_o


def setup_inputs(seed: int = 0) -> dict:
    key = jax.random.key(seed)
    ks = iter(jax.random.split(key, 48))
    f32 = jnp.float32
    D = D_MODEL

    def nrm(shape, scale):
        return jax.random.normal(next(ks), shape, f32) * scale

    def unif(shape, lo, hi):
        return jax.random.uniform(next(ks), shape, f32, lo, hi)

    return {
        'x': nrm((BATCH, SEQ, D), 1.0),
        'meta': nrm((N_META, D), 1.0),
        'norm_mix': 1.0 + nrm((DEPTH, D), 0.05),
        'norm_mlp': 1.0 + nrm((DEPTH, D), 0.05),
        'norm_f': 1.0 + nrm((D,), 0.05),
        'mlp_w1': nrm((DEPTH, D, D_FF), D ** -0.5),
        'mlp_w2': nrm((DEPTH, D_FF, D), D_FF ** -0.5),
        'rw_mix': unif((N_RWKV, 6, D), 0.0, 1.0),
        'rw_w_rkv': nrm((N_RWKV, 3, D, D), D ** -0.5),
        'rw_w0': unif((N_RWKV, D), -6.0, -0.5),
        'rw_w1': nrm((N_RWKV, D, RW_DECAY_LORA), D ** -0.5),
        'rw_w2': nrm((N_RWKV, RW_DECAY_LORA, D), 0.5 * RW_DECAY_LORA ** -0.5),
        'rw_a0': nrm((N_RWKV, D), 0.1),
        'rw_a1': nrm((N_RWKV, D, RW_AAA_LORA), D ** -0.5),
        'rw_a2': nrm((N_RWKV, RW_AAA_LORA, D), RW_AAA_LORA ** -0.5),
        'rw_v0': nrm((N_RWKV - 1, D), 0.1),
        'rw_v1': nrm((N_RWKV - 1, D, RW_MV_LORA), D ** -0.5),
        'rw_v2': nrm((N_RWKV - 1, RW_MV_LORA, D), RW_MV_LORA ** -0.5),
        'rw_g1': nrm((N_RWKV, D, RW_GATE_LORA), D ** -0.5),
        'rw_g2': nrm((N_RWKV, RW_GATE_LORA, D), RW_GATE_LORA ** -0.5),
        'rw_k_k': 0.85 + nrm((N_RWKV, D), 0.05),
        'rw_k_a': 1.0 + nrm((N_RWKV, D), 0.05),
        'rw_r_k': nrm((N_RWKV, RW_HEADS, RW_HEAD), 0.1),
        'rw_ln_w': 1.0 + nrm((N_RWKV, D), 0.05),
        'rw_ln_b': nrm((N_RWKV, D), 0.01),
        'rw_w_o': nrm((N_RWKV, D, D), D ** -0.5),
        'gla_w_in': nrm((N_GLA, D, GLA_IN), D ** -0.5),
        'gla_w_a2': nrm((N_GLA, GLA_GATE_LORA, GLA_DK), GLA_GATE_LORA ** -0.5),
        'gla_b_a': nrm((N_GLA, GLA_DK), 0.1),
        'gla_gn_w': 1.0 + nrm((N_GLA, GLA_DV), 0.05),
        'gla_w_o': nrm((N_GLA, GLA_DV, D), GLA_DV ** -0.5),
    }


def reference(x, meta, norm_mix, norm_mlp, norm_f, mlp_w1, mlp_w2, rw_mix, rw_w_rkv, rw_w0, rw_w1,
              rw_w2, rw_a0, rw_a1, rw_a2, rw_v0, rw_v1, rw_v2, rw_g1, rw_g2, rw_k_k, rw_k_a, rw_r_k,
              rw_ln_w, rw_ln_b, rw_w_o, gla_w_in, gla_w_a2, gla_b_a, gla_gn_w, gla_w_o):
    B = x.shape[0]
    h = jnp.concatenate([jnp.broadcast_to(meta.astype(x.dtype)[None], (B, N_META, D_MODEL)), x], axis=1)
    v_first = None
    for i in range(DEPTH):
        j = i // N_MIXERS
        hn = rms_norm(h, norm_mix[i])
        if i % N_MIXERS == 0:
            vres = None if j == 0 else (rw_v0[j - 1], rw_v1[j - 1], rw_v2[j - 1])
            mix_out, vf = rwkv7_time_mix(hn, v_first, rw_mix[j], rw_w_rkv[j], rw_w0[j], rw_w1[j], rw_w2[j],
                                         rw_a0[j], rw_a1[j], rw_a2[j], rw_g1[j], rw_g2[j], rw_k_k[j],
                                         rw_k_a[j], rw_r_k[j], rw_ln_w[j], rw_ln_b[j], rw_w_o[j], vres)
            if j == 0:
                v_first = vf
        else:
            mix_out = gla_time_mix(hn, gla_w_in[j], gla_w_a2[j], gla_b_a[j], gla_gn_w[j], gla_w_o[j])
        h = h + mix_out
        h = h + sq_relu_mlp(rms_norm(h, norm_mlp[i]), mlp_w1[i], mlp_w2[i])
    return rms_norm(h, norm_f)[:, N_META:]
```

```python
import functools

import jax
import jax.numpy as jnp
from jax import lax
from jax.experimental import pallas as pl
from jax.experimental.pallas import tpu as pltpu

F32 = jnp.float32
BF16 = jnp.bfloat16

N_META = 16
CHUNK = 64
LEAD = CHUNK
N_DUMMY = LEAD - N_META
NORM_EPS = 1e-6

RW_HEAD = 64
LANES = 128
RW_GN_EPS = 64e-5
RW_DECAY_SCALE = 0.6065306597126334

GLA_HEADS = 4
GLA_TAU = 16.0
GLA_HEAD_EPS = 1e-5

VMEM_LIMIT = 56 * 1024 * 1024


def _cparams(*sem):
    return pltpu.CompilerParams(dimension_semantics=sem, vmem_limit_bytes=VMEM_LIMIT)


def _dot(a, b):
    return jnp.dot(a.astype(BF16), b.astype(BF16), preferred_element_type=F32)


def _dot_nt(a, b):
    return lax.dot_general(a.astype(BF16), b.astype(BF16), (((1,), (1,)), ((), ())),
                           preferred_element_type=F32)


def _dot_tn(a, b):
    return lax.dot_general(a.astype(BF16), b.astype(BF16), (((0,), (0,)), ((), ())),
                           preferred_element_type=F32)


def _split3(x):
    x1 = x.astype(BF16)
    r1 = x - x1.astype(F32)
    x2 = r1.astype(BF16)
    x3 = (r1 - x2.astype(F32)).astype(BF16)
    return x1, x2, x3


def _dot_sel_left(sel, x):
    x1, x2, x3 = _split3(x)
    return ((jnp.dot(sel, x1, preferred_element_type=F32)
             + jnp.dot(sel, x2, preferred_element_type=F32))
            + jnp.dot(sel, x3, preferred_element_type=F32))


def _dot_sel_right(x, sel):
    x1, x2, x3 = _split3(x)
    return ((jnp.dot(x1, sel, preferred_element_type=F32)
             + jnp.dot(x2, sel, preferred_element_type=F32))
            + jnp.dot(x3, sel, preferred_element_type=F32))


def _rms(x, g, eps):
    return x * lax.rsqrt(jnp.mean(x * x, axis=-1, keepdims=True) + eps) * g


def _sigmoid(x):
    return 1.0 / (1.0 + jnp.exp(-x))


def _pick_tile(n, target):
    best = None
    for t in range(16, min(n, target) + 1, 16):
        if n % t == 0:
            best = t
    assert best is not None, (n, target)
    return best


def _mlp_kernel(h_ref, g_ref, w1_ref, w2_ref, gf_ref, o_ref, xn_ref, *, final):
    j = pl.program_id(1)

    @pl.when(j == 0)
    def _():
        x = h_ref[...]
        xn_ref[...] = _rms(x, g_ref[...], NORM_EPS).astype(BF16)
        o_ref[...] = x

    hid = jnp.dot(xn_ref[...], w1_ref[...], preferred_element_type=F32)
    hid = jnp.maximum(hid, 0.0)
    hid = hid * hid
    o_ref[...] += jnp.dot(hid.astype(BF16), w2_ref[...], preferred_element_type=F32)

    if final:
        @pl.when(j == pl.num_programs(1) - 1)
        def _():
            o_ref[...] = _rms(o_ref[...], gf_ref[...], NORM_EPS)


def _mlp(h, g, w1, w2, gf, final):
    m, d = h.shape
    ff = w1.shape[1]
    tm = _pick_tile(m, 640)
    tf = _pick_tile(ff, 512)
    return pl.pallas_call(
        functools.partial(_mlp_kernel, final=final),
        grid=(m // tm, ff // tf),
        in_specs=[
            pl.BlockSpec((tm, d), lambda i, j: (i, 0)),
            pl.BlockSpec((1, d), lambda i, j: (0, 0)),
            pl.BlockSpec((d, tf), lambda i, j: (0, j)),
            pl.BlockSpec((tf, d), lambda i, j: (j, 0)),
            pl.BlockSpec((1, d), lambda i, j: (0, 0)),
        ],
        out_specs=pl.BlockSpec((tm, d), lambda i, j: (i, 0)),
        out_shape=jax.ShapeDtypeStruct((m, d), F32),
        scratch_shapes=[pltpu.VMEM((tm, d), BF16)],
        compiler_params=_cparams("parallel", "arbitrary"),
        name="mlp",
    )(h, g, w1, w2, gf)


def _rw_prep_kernel(h_ref, hp_ref, g_ref, mix_ref, o_ref):
    i = pl.program_id(1)
    g = g_ref[...]
    hn = _rms(h_ref[0], g, NORM_EPS)
    pn = _rms(hp_ref[0], g, NORM_EPS)[7:8]
    pn = jnp.where(i == 0, 0.0, pn)
    row = lax.broadcasted_iota(jnp.int32, hn.shape, 0)
    prev = jnp.where(row == 0, pn, pltpu.roll(hn, 1, axis=0))
    xx = prev - hn
    for s in range(6):
        o_ref[s, 0] = (hn + xx * mix_ref[s:s + 1, :]).astype(BF16)


def _rw_prep(h3, g, mix):
    b, tp, d = h3.shape
    tm = _pick_tile(tp, 416)
    nb = tm // 8
    return pl.pallas_call(
        _rw_prep_kernel,
        grid=(b, tp // tm),
        in_specs=[
            pl.BlockSpec((1, tm, d), lambda bi, i: (bi, i, 0)),
            pl.BlockSpec((1, 8, d), lambda bi, i: (bi, jnp.maximum(i * nb - 1, 0), 0)),
            pl.BlockSpec((1, d), lambda bi, i: (0, 0)),
            pl.BlockSpec((6, d), lambda bi, i: (0, 0)),
        ],
        out_specs=pl.BlockSpec((6, 1, tm, d), lambda bi, i: (0, bi, i, 0)),
        out_shape=jax.ShapeDtypeStruct((6, b, tp, d), BF16),
        compiler_params=_cparams("parallel", "parallel"),
        name="rw_prep",
    )(h3, h3, g, mix)


def _to_pairs(o_ref, lead, val):
    for p in range(val.shape[1] // LANES):
        o_ref[lead + (p,)] = val[:, p * LANES:(p + 1) * LANES].astype(o_ref.dtype)


def _rkv_kernel(x_ref, w_ref, o_ref):
    _to_pairs(o_ref, (0,), jnp.dot(x_ref[0], w_ref[0], preferred_element_type=F32))


def _rkv(xs, w):
    _, m, d = xs.shape
    tm = _pick_tile(m, 640)
    npair = d // LANES
    return pl.pallas_call(
        _rkv_kernel,
        grid=(3, m // tm),
        in_specs=[
            pl.BlockSpec((1, tm, d), lambda gi, i: (gi, i, 0)),
            pl.BlockSpec((1, d, d), lambda gi, i: (gi, 0, 0)),
        ],
        out_specs=pl.BlockSpec((1, npair, tm, LANES), lambda gi, i: (gi, 0, i, 0)),
        out_shape=jax.ShapeDtypeStruct((3, npair, m, LANES), F32),
        compiler_params=_cparams("parallel", "parallel"),
        name="rw_rkv",
    )(xs, w)


def _lora_kernel(x_ref, w1_ref, w2_ref, b_ref, o_ref, *, mid, out):
    z = jnp.dot(x_ref[0], w1_ref[...], preferred_element_type=F32)
    if mid == "tanh":
        z = jnp.tanh(z)
    elif mid == "sigmoid":
        z = _sigmoid(z)
    y = jnp.dot(z.astype(BF16), w2_ref[...], preferred_element_type=F32)
    if out == "decay":
        y = -RW_DECAY_SCALE * _sigmoid(y + b_ref[...])
    elif out == "sigmoid":
        y = _sigmoid(y + b_ref[...])
    _to_pairs(o_ref, (), y)


def _lora(xs, slot, w1, w2, bias, mid, out):
    _, m, d = xs.shape
    r = w1.shape[1]
    tm = _pick_tile(m, 640)
    npair = d // LANES
    return pl.pallas_call(
        functools.partial(_lora_kernel, mid=mid, out=out),
        grid=(m // tm,),
        in_specs=[
            pl.BlockSpec((1, tm, d), lambda i: (slot, i, 0)),
            pl.BlockSpec((d, r), lambda i: (0, 0)),
            pl.BlockSpec((r, d), lambda i: (0, 0)),
            pl.BlockSpec((1, d), lambda i: (0, 0)),
        ],
        out_specs=pl.BlockSpec((npair, tm, LANES), lambda i: (0, i, 0)),
        out_shape=jax.ShapeDtypeStruct((npair, m, LANES), F32),
        compiler_params=_cparams("parallel"),
        name="rw_lora_" + out,
    )(xs, w1, w2, bias)


def _pad_rank(w1, w2):
    r = w1.shape[1]
    rp = -(-r // LANES) * LANES
    return (jnp.pad(w1, ((0, 0), (0, rp - r))).astype(BF16),
            jnp.pad(w2, ((0, rp - r), (0, 0))).astype(BF16))


def _stack_heads(x, lane_lo):
    return jnp.concatenate([jnp.where(lane_lo, x, 0.0), jnp.where(lane_lo, 0.0, x)], axis=0)


def _rw_scan_kernel(*refs, has_vres):
    if has_vres:
        (rkv_ref, lw_ref, ag_ref, gg_ref, vf_ref, vg_ref,
         kk_ref, ka_ref, rk_ref, lnw_ref, lnb_ref, z_ref, st_ref) = refs
    else:
        (rkv_ref, lw_ref, ag_ref, gg_ref,
         kk_ref, ka_ref, rk_ref, lnw_ref, lnb_ref, z_ref, st_ref) = refs
    c = pl.program_id(1)
    npair = z_ref.shape[0]
    two_l = 2 * CHUNK

    @pl.when(c == 0)
    def _():
        st_ref[...] = jnp.zeros_like(st_ref)

    row = lax.broadcasted_iota(jnp.int32, (two_l, two_l), 0)
    col = lax.broadcasted_iota(jnp.int32, (two_l, two_l), 1)
    t_row = row % CHUNK
    t_col = col % CHUNK
    strict = t_row > t_col
    incl = t_row >= t_col
    same_head = (row // CHUNK) == (col // CHUNK)
    ones_bd = jnp.where(same_head, 1.0, 0.0).astype(BF16)
    tril = jnp.where(lax.broadcasted_iota(jnp.int32, (CHUNK, CHUNK), 0)
                     >= lax.broadcasted_iota(jnp.int32, (CHUNK, CHUNK), 1), 1.0, 0.0).astype(BF16)
    diag = row == col
    lane_lo = lax.broadcasted_iota(jnp.int32, (CHUNK, LANES), 1) < RW_HEAD
    live = jnp.logical_or(c > 0, lax.broadcasted_iota(jnp.int32, (CHUNK, LANES), 0) >= N_DUMMY)

    def pair_body(p, carry):
        r = rkv_ref[0, p]
        k = rkv_ref[1, p]
        v = rkv_ref[2, p]
        lw = lw_ref[p]
        ag = ag_ref[p]
        if has_vres:
            v = v + (vf_ref[0, p] - v) * vg_ref[p]
        kk = k * kk_ref[p]
        k = k * (1.0 + (ag - 1.0) * ka_ref[p])
        sums = _dot_sel_right(jnp.concatenate([kk * kk, r * k * rk_ref[p]], axis=0), ones_bd)
        kk = kk / jnp.maximum(jnp.sqrt(sums[:CHUNK]), 1e-12)
        bonus = sums[CHUNK:] * v

        cs = _dot_sel_left(tril, lw)
        e_pos = jnp.exp(cs)
        e_neg = jnp.exp(-cs)
        w_all = e_pos[CHUNK - 1:CHUNK, :]
        a_t = -kk * jnp.exp(cs - lw)
        r_t = r * e_pos
        b_t = kk * ag * e_neg
        k_t = k * e_neg

        a_s = _stack_heads(a_t, lane_lo)
        v_s = _stack_heads(v, lane_lo).astype(BF16)
        gram = _dot_nt(jnp.concatenate([a_s, _stack_heads(r_t, lane_lo)], axis=0),
                       jnp.concatenate([_stack_heads(b_t, lane_lo), _stack_heads(k_t, lane_lo)], axis=0))
        a_ab = jnp.where(strict, gram[:two_l, :two_l], 0.0)
        a_ak = jnp.where(strict, gram[:two_l, two_l:], 0.0).astype(BF16)
        a_rb = jnp.where(incl, gram[two_l:, :two_l], 0.0).astype(BF16)
        a_rk = jnp.where(incl, gram[two_l:, two_l:], 0.0).astype(BF16)

        z = jnp.concatenate([a_s, jnp.dot(a_ak, v_s, preferred_element_type=F32)], axis=1)
        pw = a_ab.astype(BF16)
        n_sq = CHUNK.bit_length() - 1
        for s in range(n_sq):
            z = z + jnp.dot(pw, z.astype(BF16), preferred_element_type=F32)
            if s + 1 < n_sq:
                pw = jnp.dot(pw, pw, preferred_element_type=F32).astype(BF16)
        zb = z.astype(BF16)

        rz = jnp.dot(a_rb, zb, preferred_element_type=F32)
        r_hat = _stack_heads(r_t, lane_lo) + rz[:, :LANES]
        y_hat = rz[:, LANES:] + jnp.dot(a_rk, v_s, preferred_element_type=F32)
        bz = _dot_tn(_stack_heads(b_t * w_all, lane_lo), zb)
        m_mat = jnp.where(diag, w_all, 0.0) + bz[:, :LANES]
        n_mat = bz[:, LANES:] + _dot_tn(_stack_heads(k_t * w_all, lane_lo), v_s)

        hb = st_ref[p].astype(BF16)
        y2 = jnp.dot(r_hat.astype(BF16), hb, preferred_element_type=F32) + y_hat
        st_ref[p] = jnp.dot(m_mat.astype(BF16), hb, preferred_element_type=F32) + n_mat
        y = y2[:CHUNK] + y2[CHUNK:]

        mu = _dot_sel_right(y, ones_bd) * (1.0 / RW_HEAD)
        dy = y - mu
        var = _dot_sel_right(dy * dy, ones_bd) * (1.0 / RW_HEAD)
        yn = dy * lax.rsqrt(var + RW_GN_EPS) * lnw_ref[p] + lnb_ref[p]
        out = (yn + bonus) * gg_ref[p]
        z_ref[p] = jnp.where(live, out, 0.0).astype(z_ref.dtype)
        return carry

    lax.fori_loop(0, npair, pair_body, 0)


def _rw_scan(rkv, lw, ag, gg, vres, kk, ka, rk, lnw, lnb, batch):
    _, npair, m, _ = rkv.shape
    nc = m // batch // CHUNK
    row_map3 = lambda bi, c: (0, bi * nc + c, 0)
    row_map4 = lambda bi, c: (0, 0, bi * nc + c, 0)
    par = pl.BlockSpec((npair, 1, LANES), lambda bi, c: (0, 0, 0))
    tile = pl.BlockSpec((npair, CHUNK, LANES), row_map3)
    in_specs = [pl.BlockSpec((3, npair, CHUNK, LANES), row_map4), tile, tile, tile]
    args = [rkv, lw, ag, gg]
    if vres is not None:
        vfirst, vgate = vres
        in_specs += [pl.BlockSpec((1, npair, CHUNK, LANES), lambda bi, c: (2, 0, bi * nc + c, 0)), tile]
        args += [vfirst, vgate]
    in_specs += [par] * 5
    args += [kk, ka, rk, lnw, lnb]
    return pl.pallas_call(
        functools.partial(_rw_scan_kernel, has_vres=vres is not None),
        grid=(batch, nc),
        in_specs=in_specs,
        out_specs=tile,
        out_shape=jax.ShapeDtypeStruct((npair, m, LANES), BF16),
        scratch_shapes=[pltpu.VMEM((npair, 2 * CHUNK, LANES), F32)],
        compiler_params=_cparams("parallel", "arbitrary"),
        name="rw_scan",
    )(*args)


def _proj_kernel(x_ref, w_ref, h_ref, o_ref, *, pair_major):
    if pair_major:
        x = jnp.concatenate([x_ref[p] for p in range(x_ref.shape[0])], axis=1)
    else:
        x = x_ref[...]
    o_ref[...] = h_ref[...] + jnp.dot(x, w_ref[...], preferred_element_type=F32)


def _proj_residual(x, w, h, pair_major):
    m, d = h.shape
    kdim = w.shape[0]
    tm = _pick_tile(m, 640)
    if pair_major:
        x_spec = pl.BlockSpec((x.shape[0], tm, LANES), lambda i: (0, i, 0))
    else:
        x_spec = pl.BlockSpec((tm, kdim), lambda i: (i, 0))
    return pl.pallas_call(
        functools.partial(_proj_kernel, pair_major=pair_major),
        grid=(m // tm,),
        in_specs=[x_spec,
                  pl.BlockSpec((kdim, d), lambda i: (0, 0)),
                  pl.BlockSpec((tm, d), lambda i: (i, 0))],
        out_specs=pl.BlockSpec((tm, d), lambda i: (i, 0)),
        out_shape=jax.ShapeDtypeStruct((m, d), F32),
        compiler_params=_cparams("parallel"),
        name="proj_residual",
    )(x, w, h)


def _norm_matmul_kernel(h_ref, g_ref, w_ref, o_ref, xn_ref):
    @pl.when(pl.program_id(1) == 0)
    def _():
        xn_ref[...] = _rms(h_ref[...], g_ref[...], NORM_EPS).astype(BF16)

    o_ref[...] = jnp.dot(xn_ref[...], w_ref[...], preferred_element_type=F32).astype(o_ref.dtype)


def _norm_matmul(h, g, w, out_dtype):
    m, d = h.shape
    n = w.shape[1]
    tm = _pick_tile(m, 640)
    tn = _pick_tile(n, 1024)
    return pl.pallas_call(
        _norm_matmul_kernel,
        grid=(m // tm, n // tn),
        in_specs=[
            pl.BlockSpec((tm, d), lambda i, j: (i, 0)),
            pl.BlockSpec((1, d), lambda i, j: (0, 0)),
            pl.BlockSpec((d, tn), lambda i, j: (0, j)),
        ],
        out_specs=pl.BlockSpec((tm, tn), lambda i, j: (i, j)),
        out_shape=jax.ShapeDtypeStruct((m, n), out_dtype),
        scratch_shapes=[pltpu.VMEM((tm, d), BF16)],
        compiler_params=_cparams("parallel", "arbitrary"),
        name="gla_in",
    )(h, g, w)


def _gla_gate_kernel(h_ref, g_ref, wz_ref, wa_ref, ba_ref, o_ref):
    xn = _rms(h_ref[...], g_ref[...], NORM_EPS).astype(BF16)
    za = jnp.dot(xn, wz_ref[...], preferred_element_type=F32)
    u = _dot(za, wa_ref[...]) + ba_ref[...]
    o_ref[...] = (jnp.minimum(u, 0.0) - jnp.log1p(jnp.exp(-jnp.abs(u)))) * (1.0 / GLA_TAU)


def _gla_gate(h, g, wz, wa, ba):
    m, d = h.shape
    dk = wa.shape[1]
    tm = _pick_tile(m, 640)
    return pl.pallas_call(
        _gla_gate_kernel,
        grid=(m // tm,),
        in_specs=[
            pl.BlockSpec((tm, d), lambda i: (i, 0)),
            pl.BlockSpec((1, d), lambda i: (0, 0)),
            pl.BlockSpec((d, LANES), lambda i: (0, 0)),
            pl.BlockSpec((LANES, dk), lambda i: (0, 0)),
            pl.BlockSpec((1, dk), lambda i: (0, 0)),
        ],
        out_specs=pl.BlockSpec((tm, dk), lambda i: (i, 0)),
        out_shape=jax.ShapeDtypeStruct((m, dk), F32),
        compiler_params=_cparams("parallel"),
        name="gla_gate",
    )(h, g, wz, wa, ba)


def _gla_chunk_kernel(q_ref, k_ref, v_ref, gate_ref, gl_ref, gn_ref, z_ref, st_ref, *, scale):
    c = pl.program_id(2)

    @pl.when(c == 0)
    def _():
        st_ref[...] = jnp.zeros_like(st_ref)

    row = lax.broadcasted_iota(jnp.int32, (CHUNK, CHUNK), 0)
    col = lax.broadcasted_iota(jnp.int32, (CHUNK, CHUNK), 1)
    causal = row >= col
    tril = jnp.where(causal, 1.0, 0.0).astype(BF16)

    gl = gl_ref[...]
    live = jnp.logical_or(c > 0, lax.broadcasted_iota(jnp.int32, gl.shape, 0) >= N_DUMMY)
    gl = jnp.where(live, gl, 0.0)
    bc = _dot_sel_left(tril, gl)
    b_last = bc[CHUNK - 1:CHUNK, :]
    v = v_ref[...].astype(BF16)
    q_t = q_ref[...] * scale * jnp.exp(bc)
    k_t = k_ref[...] * jnp.exp(-bc)
    k_h = k_ref[...] * jnp.exp(b_last - bc)

    att = jnp.where(causal, _dot_nt(q_t, k_t), 0.0)
    st = st_ref[...]
    o = _dot_nt(q_t, st) + jnp.dot(att.astype(BF16), v, preferred_element_type=F32)
    st_ref[...] = st * jnp.exp(b_last) + _dot_tn(v, k_h)

    o = o * lax.rsqrt(jnp.mean(o * o, axis=-1, keepdims=True) + GLA_HEAD_EPS) * gn_ref[...]
    gate = gate_ref[...]
    z_ref[...] = (o * (gate * _sigmoid(gate))).astype(z_ref.dtype)


def _gla_chunk(p, glog, gn_w, batch, d):
    m = p.shape[0]
    nc = m // batch // CHUNK
    hk = d // 2 // GLA_HEADS
    hv = d // GLA_HEADS
    nh = GLA_HEADS
    rows = lambda bi, hi, c: bi * nc + c
    return pl.pallas_call(
        functools.partial(_gla_chunk_kernel, scale=hk ** -0.5),
        grid=(batch, nh, nc),
        in_specs=[
            pl.BlockSpec((CHUNK, hk), lambda bi, hi, c: (rows(bi, hi, c), hi)),
            pl.BlockSpec((CHUNK, hk), lambda bi, hi, c: (rows(bi, hi, c), nh + hi)),
            pl.BlockSpec((CHUNK, hv), lambda bi, hi, c: (rows(bi, hi, c), nh + hi)),
            pl.BlockSpec((CHUNK, hv), lambda bi, hi, c: (rows(bi, hi, c), 2 * nh + hi)),
            pl.BlockSpec((CHUNK, hk), lambda bi, hi, c: (rows(bi, hi, c), hi)),
            pl.BlockSpec((1, hv), lambda bi, hi, c: (0, hi)),
        ],
        out_specs=pl.BlockSpec((CHUNK, hv), lambda bi, hi, c: (rows(bi, hi, c), hi)),
        out_shape=jax.ShapeDtypeStruct((m, d), BF16),
        scratch_shapes=[pltpu.VMEM((hv, hk), F32)],
        compiler_params=_cparams("parallel", "parallel", "arbitrary"),
        name="gla_chunk",
    )(p, p, p, p, glog, gn_w)


def _rwkv_layer(h, batch, j, v_first, norm_g, rw_mix, rw_w_rkv, rw_w0, rw_w1, rw_w2, rw_a0, rw_a1, rw_a2,
                rw_v0, rw_v1, rw_v2, rw_g1, rw_g2, rw_k_k, rw_k_a, rw_r_k, rw_ln_w, rw_ln_b, rw_w_o):
    m, d = h.shape
    npair = d // LANES
    row = lambda t: t.reshape(1, d)
    pairs = lambda t: t.reshape(npair, 1, LANES)
    mix = rw_mix[j][jnp.array([0, 2, 3, 1, 4, 5])]
    xs = _rw_prep(h.reshape(batch, m // batch, d), row(norm_g), mix).reshape(6, m, d)
    rkv = _rkv(xs, rw_w_rkv[j].astype(BF16))
    zeros = jnp.zeros((1, d), F32)
    lw = _lora(xs, 3, *_pad_rank(rw_w1[j], rw_w2[j]), row(rw_w0[j]), "tanh", "decay")
    ag = _lora(xs, 4, *_pad_rank(rw_a1[j], rw_a2[j]), row(rw_a0[j]), "none", "sigmoid")
    gg = _lora(xs, 5, *_pad_rank(rw_g1[j], rw_g2[j]), zeros, "sigmoid", "none")
    vres = None
    if j > 0:
        vgate = _lora(xs, 2, *_pad_rank(rw_v1[j - 1], rw_v2[j - 1]), row(rw_v0[j - 1]), "none", "sigmoid")
        vres = (v_first, vgate)
    z = _rw_scan(rkv, lw, ag, gg, vres, pairs(rw_k_k[j]), pairs(rw_k_a[j]), pairs(rw_r_k[j]),
                 pairs(rw_ln_w[j]), pairs(rw_ln_b[j]), batch)
    h = _proj_residual(z, rw_w_o[j].astype(BF16), h, pair_major=True)
    return h, rkv


def _gla_layer(h, batch, j, norm_g, gla_w_in, gla_w_a2, gla_b_a, gla_gn_w, gla_w_o):
    m, d = h.shape
    dk = d // 2
    n_main = 2 * dk + 2 * d
    w_in = gla_w_in[j]
    rank = w_in.shape[1] - n_main
    g = norm_g.reshape(1, d)
    p = _norm_matmul(h, g, w_in[:, :n_main].astype(BF16), F32)
    wz = jnp.pad(w_in[:, n_main:], ((0, 0), (0, LANES - rank))).astype(BF16)
    wa = jnp.pad(gla_w_a2[j], ((0, LANES - rank), (0, 0)))
    glog = _gla_gate(h, g, wz, wa, gla_b_a[j].reshape(1, dk))
    z = _gla_chunk(p, glog, gla_gn_w[j].reshape(1, d), batch, d)
    return _proj_residual(z, gla_w_o[j].astype(BF16), h, pair_major=False)


def kernel(x, meta, norm_mix, norm_mlp, norm_f, mlp_w1, mlp_w2, rw_mix, rw_w_rkv, rw_w0, rw_w1, rw_w2, rw_a0, rw_a1, rw_a2, rw_v0, rw_v1, rw_v2, rw_g1, rw_g2, rw_k_k, rw_k_a, rw_r_k, rw_ln_w, rw_ln_b, rw_w_o, gla_w_in, gla_w_a2, gla_b_a, gla_gn_w, gla_w_o):
    batch, seq, d = x.shape
    depth = norm_mix.shape[0]
    assert seq % CHUNK == 0 and d % (2 * LANES) == 0 and meta.shape[0] == N_META
    tp = LEAD + seq
    lead = jnp.concatenate([jnp.zeros((N_DUMMY, d), x.dtype), meta.astype(x.dtype)], axis=0)
    h = jnp.concatenate([jnp.broadcast_to(lead[None], (batch, LEAD, d)), x], axis=1).reshape(batch * tp, d)
    gf = norm_f.reshape(1, d)
    v_first = None
    for i in range(depth):
        j = i // 2
        if i % 2 == 0:
            h, rkv = _rwkv_layer(h, batch, j, v_first, norm_mix[i], rw_mix, rw_w_rkv, rw_w0, rw_w1, rw_w2,
                                 rw_a0, rw_a1, rw_a2, rw_v0, rw_v1, rw_v2, rw_g1, rw_g2, rw_k_k, rw_k_a,
                                 rw_r_k.reshape(rw_r_k.shape[0], d), rw_ln_w, rw_ln_b, rw_w_o)
            if j == 0:
                v_first = rkv
        else:
            h = _gla_layer(h, batch, j, norm_mix[i], gla_w_in, gla_w_a2, gla_b_a, gla_gn_w, gla_w_o)
        h = _mlp(h, norm_mlp[i].reshape(1, d), mlp_w1[i].astype(BF16), mlp_w2[i].astype(BF16), gf,
                 final=(i == depth - 1))
    return h.reshape(batch, tp, d)[:, LEAD:]
```

```python
import functools

import jax
import jax.numpy as jnp
from jax import lax
from jax.experimental import pallas as pl
from jax.experimental.pallas import tpu as pltpu

F32 = jnp.float32
BF16 = jnp.bfloat16

N_META = 16
CHUNK = 64
LEAD = CHUNK
N_DUMMY = LEAD - N_META
NORM_EPS = 1e-6

RW_HEAD = 64
LANES = 128
RW_GN_EPS = 64e-5
RW_DECAY_SCALE = 0.6065306597126334
PAIR_GROUP = 16

GLA_HEADS = 4
GLA_TAU = 16.0
GLA_HEAD_EPS = 1e-5

VMEM_LIMIT = 56 * 1024 * 1024


def _cparams(*sem):
    return pltpu.CompilerParams(dimension_semantics=sem, vmem_limit_bytes=VMEM_LIMIT)


def _dot(a, b):
    return jnp.dot(a.astype(BF16), b.astype(BF16), preferred_element_type=F32)


def _dot_nt(a, b):
    return lax.dot_general(a.astype(BF16), b.astype(BF16), (((1,), (1,)), ((), ())),
                           preferred_element_type=F32)


def _dot_tn(a, b):
    return lax.dot_general(a.astype(BF16), b.astype(BF16), (((0,), (0,)), ((), ())),
                           preferred_element_type=F32)


def _split2(x):
    hi = x.astype(BF16)
    return hi, (x - hi.astype(F32)).astype(BF16)


def _dot_sel_left(sel, x):
    hi, lo = _split2(x)
    return jnp.dot(sel, hi, preferred_element_type=F32) + jnp.dot(sel, lo, preferred_element_type=F32)


def _dot_sel_right(x, sel):
    hi, lo = _split2(x)
    return jnp.dot(hi, sel, preferred_element_type=F32) + jnp.dot(lo, sel, preferred_element_type=F32)


def _rms(x, g, eps):
    return x * lax.rsqrt(jnp.mean(x * x, axis=-1, keepdims=True) + eps) * g


def _sigmoid(x):
    return 1.0 / (1.0 + jnp.exp(-x))


def _pick_tile(n, target):
    best = None
    for t in range(16, min(n, target) + 1, 16):
        if n % t == 0:
            best = t
    assert best is not None, (n, target)
    return best


def _mlp_kernel(h_ref, g_ref, w1_ref, w2_ref, gf_ref, o_ref, xn_ref, *, final):
    j = pl.program_id(1)

    @pl.when(j == 0)
    def _():
        x = h_ref[...]
        xn_ref[...] = _rms(x, g_ref[...], NORM_EPS).astype(BF16)
        o_ref[...] = x

    hid = jnp.dot(xn_ref[...], w1_ref[...], preferred_element_type=F32)
    hid = jnp.maximum(hid, 0.0)
    hid = hid * hid
    o_ref[...] += jnp.dot(hid.astype(BF16), w2_ref[...], preferred_element_type=F32)

    if final:
        @pl.when(j == pl.num_programs(1) - 1)
        def _():
            o_ref[...] = _rms(o_ref[...], gf_ref[...], NORM_EPS)


def _mlp(h, g, w1, w2, layer, gf, final):
    m, d = h.shape
    ff = w1.shape[2]
    tm = _pick_tile(m, 640)
    tf = _pick_tile(ff, 512)
    return pl.pallas_call(
        functools.partial(_mlp_kernel, final=final),
        grid=(m // tm, ff // tf),
        in_specs=[
            pl.BlockSpec((tm, d), lambda i, j: (i, 0)),
            pl.BlockSpec((1, d), lambda i, j: (0, 0)),
            pl.BlockSpec((None, d, tf), lambda i, j: (layer, 0, j)),
            pl.BlockSpec((None, tf, d), lambda i, j: (layer, j, 0)),
            pl.BlockSpec((1, d), lambda i, j: (0, 0)),
        ],
        out_specs=pl.BlockSpec((tm, d), lambda i, j: (i, 0)),
        out_shape=jax.ShapeDtypeStruct((m, d), F32),
        scratch_shapes=[pltpu.VMEM((tm, d), BF16)],
        compiler_params=_cparams("parallel", "arbitrary"),
        name="mlp",
    )(h, g, w1, w2, gf)


def _rw_prep_kernel(h_ref, hp_ref, g_ref, mix_ref, o_ref):
    i = pl.program_id(1)
    g = g_ref[...]
    hn = _rms(h_ref[0], g, NORM_EPS)
    pn = _rms(hp_ref[0], g, NORM_EPS)[7:8]
    pn = jnp.where(i == 0, 0.0, pn)
    row = lax.broadcasted_iota(jnp.int32, hn.shape, 0)
    prev = jnp.where(row == 0, pn, pltpu.roll(hn, 1, axis=0))
    xx = prev - hn
    for s in range(6):
        o_ref[s, 0] = (hn + xx * mix_ref[s:s + 1, :]).astype(BF16)


def _rw_prep(h3, g, mix):
    b, tp, d = h3.shape
    tm = _pick_tile(tp, 416)
    nb = tm // 8
    return pl.pallas_call(
        _rw_prep_kernel,
        grid=(b, tp // tm),
        in_specs=[
            pl.BlockSpec((1, tm, d), lambda bi, i: (bi, i, 0)),
            pl.BlockSpec((1, 8, d), lambda bi, i: (bi, jnp.maximum(i * nb - 1, 0), 0)),
            pl.BlockSpec((1, d), lambda bi, i: (0, 0)),
            pl.BlockSpec((6, d), lambda bi, i: (0, 0)),
        ],
        out_specs=pl.BlockSpec((6, 1, tm, d), lambda bi, i: (0, bi, i, 0)),
        out_shape=jax.ShapeDtypeStruct((6, b, tp, d), BF16),
        compiler_params=_cparams("parallel", "parallel"),
        name="rw_prep",
    )(h3, h3, g, mix)


def _to_pairs(o_ref, lead, val):
    for p in range(val.shape[1] // LANES):
        o_ref[lead + (p,)] = val[:, p * LANES:(p + 1) * LANES].astype(o_ref.dtype)


def _rkv_kernel(x_ref, w_ref, o_ref):
    _to_pairs(o_ref, (0,), jnp.dot(x_ref[0], w_ref[0], preferred_element_type=F32))


def _rkv(xs, w, layer):
    _, m, d = xs.shape
    tm = _pick_tile(m, 640)
    npair = d // LANES
    return pl.pallas_call(
        _rkv_kernel,
        grid=(3, m // tm),
        in_specs=[
            pl.BlockSpec((1, tm, d), lambda gi, i: (gi, i, 0)),
            pl.BlockSpec((None, 1, d, d), lambda gi, i: (layer, gi, 0, 0)),
        ],
        out_specs=pl.BlockSpec((1, npair, tm, LANES), lambda gi, i: (gi, 0, i, 0)),
        out_shape=jax.ShapeDtypeStruct((3, npair, m, LANES), F32),
        compiler_params=_cparams("parallel", "parallel"),
        name="rw_rkv",
    )(xs, w)


def _lora_kernel(x_ref, w1_ref, w2_ref, b_ref, o_ref, *, mid, out):
    z = jnp.dot(x_ref[0], w1_ref[...], preferred_element_type=F32)
    if mid == "tanh":
        z = jnp.tanh(z)
    elif mid == "sigmoid":
        z = _sigmoid(z)
    y = jnp.dot(z.astype(BF16), w2_ref[...], preferred_element_type=F32)
    if out == "decay":
        y = -RW_DECAY_SCALE * _sigmoid(y + b_ref[...])
    elif out == "sigmoid":
        y = _sigmoid(y + b_ref[...])
    _to_pairs(o_ref, (), y)


def _lora(xs, slot, w1, w2, bias, mid, out):
    _, m, d = xs.shape
    r = w1.shape[1]
    tm = _pick_tile(m, 640)
    npair = d // LANES
    return pl.pallas_call(
        functools.partial(_lora_kernel, mid=mid, out=out),
        grid=(m // tm,),
        in_specs=[
            pl.BlockSpec((1, tm, d), lambda i: (slot, i, 0)),
            pl.BlockSpec((d, r), lambda i: (0, 0)),
            pl.BlockSpec((r, d), lambda i: (0, 0)),
            pl.BlockSpec((1, d), lambda i: (0, 0)),
        ],
        out_specs=pl.BlockSpec((npair, tm, LANES), lambda i: (0, i, 0)),
        out_shape=jax.ShapeDtypeStruct((npair, m, LANES), F32),
        compiler_params=_cparams("parallel"),
        name="rw_lora_" + out,
    )(xs, w1, w2, bias)


def _pad_rank(w1, w2):
    r = w1.shape[1]
    rp = -(-r // LANES) * LANES
    return (jnp.pad(w1, ((0, 0), (0, rp - r))).astype(BF16),
            jnp.pad(w2, ((0, rp - r), (0, 0))).astype(BF16))


def _stack_heads(x, lane_lo):
    return jnp.concatenate([jnp.where(lane_lo, x, 0.0), jnp.where(lane_lo, 0.0, x)], axis=0)


def _rw_scan_kernel(*refs, has_vres):
    if has_vres:
        (rkv_ref, lw_ref, ag_ref, gg_ref, vf_ref, vg_ref,
         kk_ref, ka_ref, rk_ref, lnw_ref, lnb_ref, z_ref, st_ref) = refs
    else:
        (rkv_ref, lw_ref, ag_ref, gg_ref,
         kk_ref, ka_ref, rk_ref, lnw_ref, lnb_ref, z_ref, st_ref) = refs
    c = pl.program_id(1)
    npair = z_ref.shape[0]
    group = min(PAIR_GROUP, npair)
    assert npair % group == 0
    two_l = 2 * CHUNK

    @pl.when(c == 0)
    def _():
        st_ref[...] = jnp.zeros_like(st_ref)

    row = lax.broadcasted_iota(jnp.int32, (two_l, two_l), 0)
    col = lax.broadcasted_iota(jnp.int32, (two_l, two_l), 1)
    t_row = row % CHUNK
    t_col = col % CHUNK
    strict = t_row > t_col
    incl = t_row >= t_col
    same_head = (row // CHUNK) == (col // CHUNK)
    ones_bd = jnp.where(same_head, 1.0, 0.0).astype(BF16)
    tril = jnp.where(lax.broadcasted_iota(jnp.int32, (CHUNK, CHUNK), 0)
                     >= lax.broadcasted_iota(jnp.int32, (CHUNK, CHUNK), 1), 1.0, 0.0).astype(BF16)
    diag = row == col
    lane_lo = lax.broadcasted_iota(jnp.int32, (CHUNK, LANES), 1) < RW_HEAD
    live = jnp.logical_or(c > 0, lax.broadcasted_iota(jnp.int32, (CHUNK, LANES), 0) >= N_DUMMY)

    def each(fn, *lists):
        return [fn(*xs) for xs in zip(*lists)]

    def group_body(gi, carry):
        ps = [gi * group + i for i in range(group)]
        r = [rkv_ref[0, p] for p in ps]
        k0 = [rkv_ref[1, p] for p in ps]
        v = [rkv_ref[2, p] for p in ps]
        lw = [lw_ref[p] for p in ps]
        ag = [ag_ref[p] for p in ps]
        if has_vres:
            v = [vi + (vf_ref[0, p] - vi) * vg_ref[p] for vi, p in zip(v, ps)]
        kk = [ki * kk_ref[p] for ki, p in zip(k0, ps)]
        k = [ki * (1.0 + (ai - 1.0) * ka_ref[p]) for ki, ai, p in zip(k0, ag, ps)]
        sums = [_dot_sel_right(jnp.concatenate([kki * kki, ri * ki * rk_ref[p]], axis=0), ones_bd)
                for kki, ri, ki, p in zip(kk, r, k, ps)]
        kk = each(lambda kki, si: kki / jnp.maximum(jnp.sqrt(si[:CHUNK]), 1e-12), kk, sums)
        bonus = each(lambda si, vi: si[CHUNK:] * vi, sums, v)

        cs = each(lambda lwi: _dot_sel_left(tril, lwi), lw)
        e_pos = each(jnp.exp, cs)
        e_neg = each(lambda ci: jnp.exp(-ci), cs)
        w_all = each(lambda ei: ei[CHUNK - 1:CHUNK, :], e_pos)
        a_s = each(lambda kki, ci, lwi: _stack_heads(-kki * jnp.exp(ci - lwi), lane_lo), kk, cs, lw)
        r_s = each(lambda ri, ei: _stack_heads(ri * ei, lane_lo), r, e_pos)
        b_t = each(lambda kki, ai, ei: kki * ai * ei, kk, ag, e_neg)
        k_t = each(lambda ki, ei: ki * ei, k, e_neg)
        v_s = each(lambda vi: _stack_heads(vi, lane_lo).astype(BF16), v)

        gram = each(lambda asi, rsi, bti, kti: _dot_nt(
            jnp.concatenate([asi, rsi], axis=0),
            jnp.concatenate([_stack_heads(bti, lane_lo), _stack_heads(kti, lane_lo)], axis=0)),
            a_s, r_s, b_t, k_t)
        pw = each(lambda gm: jnp.where(strict, gm[:two_l, :two_l], 0.0).astype(BF16), gram)
        a_ak = each(lambda gm: jnp.where(strict, gm[:two_l, two_l:], 0.0).astype(BF16), gram)
        a_rb = each(lambda gm: jnp.where(incl, gm[two_l:, :two_l], 0.0).astype(BF16), gram)
        a_rk = each(lambda gm: jnp.where(incl, gm[two_l:, two_l:], 0.0).astype(BF16), gram)

        z = each(lambda asi, aki, vsi: jnp.concatenate(
            [asi, jnp.dot(aki, vsi, preferred_element_type=F32)], axis=1), a_s, a_ak, v_s)
        n_sq = CHUNK.bit_length() - 1
        for s in range(n_sq):
            z = each(lambda zi, pi: zi + jnp.dot(pi, zi.astype(BF16), preferred_element_type=F32), z, pw)
            if s + 1 < n_sq:
                pw = each(lambda pi: jnp.dot(pi, pi, preferred_element_type=F32).astype(BF16), pw)
        zb = each(lambda zi: zi.astype(BF16), z)

        rz = each(lambda ai, zi: jnp.dot(ai, zi, preferred_element_type=F32), a_rb, zb)
        r_hat = each(lambda rsi, rzi: (rsi + rzi[:, :LANES]).astype(BF16), r_s, rz)
        y_hat = each(lambda rzi, ai, vsi: rzi[:, LANES:] + jnp.dot(ai, vsi, preferred_element_type=F32),
                     rz, a_rk, v_s)
        bz = each(lambda bti, wi, zi: _dot_tn(_stack_heads(bti * wi, lane_lo), zi), b_t, w_all, zb)
        m_mat = each(lambda wi, bzi: (jnp.where(diag, wi, 0.0) + bzi[:, :LANES]).astype(BF16), w_all, bz)
        n_mat = each(lambda bzi, kti, wi, vsi: bzi[:, LANES:] + _dot_tn(_stack_heads(kti * wi, lane_lo), vsi),
                     bz, k_t, w_all, v_s)

        hb = [st_ref[p].astype(BF16) for p in ps]
        y2 = each(lambda rh, hi, yh: jnp.dot(rh, hi, preferred_element_type=F32) + yh, r_hat, hb, y_hat)
        st = each(lambda mm, hi, nm: jnp.dot(mm, hi, preferred_element_type=F32) + nm, m_mat, hb, n_mat)
        for p, si in zip(ps, st):
            st_ref[p] = si
        y = each(lambda yi: yi[:CHUNK] + yi[CHUNK:], y2)

        dy = each(lambda yi: yi - _dot_sel_right(yi, ones_bd) * (1.0 / RW_HEAD), y)
        var = each(lambda di: _dot_sel_right(di * di, ones_bd) * (1.0 / RW_HEAD), dy)
        for p, di, vi, bi in zip(ps, dy, var, bonus):
            yn = di * lax.rsqrt(vi + RW_GN_EPS) * lnw_ref[p] + lnb_ref[p]
            z_ref[p] = jnp.where(live, (yn + bi) * gg_ref[p], 0.0).astype(z_ref.dtype)
        return carry

    lax.fori_loop(0, npair // group, group_body, 0)


def _rw_scan(rkv, lw, ag, gg, vres, kk, ka, rk, lnw, lnb, batch):
    _, npair, m, _ = rkv.shape
    nc = m // batch // CHUNK
    row_map3 = lambda bi, c: (0, bi * nc + c, 0)
    row_map4 = lambda bi, c: (0, 0, bi * nc + c, 0)
    par = pl.BlockSpec((npair, 1, LANES), lambda bi, c: (0, 0, 0))
    tile = pl.BlockSpec((npair, CHUNK, LANES), row_map3)
    in_specs = [pl.BlockSpec((3, npair, CHUNK, LANES), row_map4), tile, tile, tile]
    args = [rkv, lw, ag, gg]
    if vres is not None:
        vfirst, vgate = vres
        in_specs += [pl.BlockSpec((1, npair, CHUNK, LANES), lambda bi, c: (2, 0, bi * nc + c, 0)), tile]
        args += [vfirst, vgate]
    in_specs += [par] * 5
    args += [kk, ka, rk, lnw, lnb]
    return pl.pallas_call(
        functools.partial(_rw_scan_kernel, has_vres=vres is not None),
        grid=(batch, nc),
        in_specs=in_specs,
        out_specs=tile,
        out_shape=jax.ShapeDtypeStruct((npair, m, LANES), BF16),
        scratch_shapes=[pltpu.VMEM((npair, 2 * CHUNK, LANES), F32)],
        compiler_params=_cparams("parallel", "arbitrary"),
        name="rw_scan",
    )(*args)


def _proj_kernel(x_ref, w_ref, h_ref, o_ref, *, pair_major):
    if pair_major:
        x = jnp.concatenate([x_ref[p] for p in range(x_ref.shape[0])], axis=1)
    else:
        x = x_ref[...]
    o_ref[...] = h_ref[...] + jnp.dot(x, w_ref[...], preferred_element_type=F32)


def _proj_residual(x, w, layer, h, pair_major):
    m, d = h.shape
    kdim = w.shape[1]
    tm = _pick_tile(m, 640)
    if pair_major:
        x_spec = pl.BlockSpec((x.shape[0], tm, LANES), lambda i: (0, i, 0))
    else:
        x_spec = pl.BlockSpec((tm, kdim), lambda i: (i, 0))
    return pl.pallas_call(
        functools.partial(_proj_kernel, pair_major=pair_major),
        grid=(m // tm,),
        in_specs=[x_spec,
                  pl.BlockSpec((None, kdim, d), lambda i: (layer, 0, 0)),
                  pl.BlockSpec((tm, d), lambda i: (i, 0))],
        out_specs=pl.BlockSpec((tm, d), lambda i: (i, 0)),
        out_shape=jax.ShapeDtypeStruct((m, d), F32),
        compiler_params=_cparams("parallel"),
        name="proj_residual",
    )(x, w, h)


def _norm_matmul_kernel(h_ref, g_ref, w_ref, o_ref, xn_ref):
    @pl.when(pl.program_id(1) == 0)
    def _():
        xn_ref[...] = _rms(h_ref[...], g_ref[...], NORM_EPS).astype(BF16)

    o_ref[...] = jnp.dot(xn_ref[...], w_ref[...], preferred_element_type=F32).astype(o_ref.dtype)


def _norm_matmul(h, g, w, layer, n, out_dtype):
    m, d = h.shape
    tm = _pick_tile(m, 640)
    tn = _pick_tile(n, 1024)
    return pl.pallas_call(
        _norm_matmul_kernel,
        grid=(m // tm, n // tn),
        in_specs=[
            pl.BlockSpec((tm, d), lambda i, j: (i, 0)),
            pl.BlockSpec((1, d), lambda i, j: (0, 0)),
            pl.BlockSpec((None, d, tn), lambda i, j: (layer, 0, j)),
        ],
        out_specs=pl.BlockSpec((tm, tn), lambda i, j: (i, j)),
        out_shape=jax.ShapeDtypeStruct((m, n), out_dtype),
        scratch_shapes=[pltpu.VMEM((tm, d), BF16)],
        compiler_params=_cparams("parallel", "arbitrary"),
        name="gla_in",
    )(h, g, w)


def _gla_gate_kernel(h_ref, g_ref, wz_ref, wa_ref, ba_ref, o_ref):
    xn = _rms(h_ref[...], g_ref[...], NORM_EPS).astype(BF16)
    za = jnp.dot(xn, wz_ref[...], preferred_element_type=F32)
    u = _dot(za, wa_ref[...]) + ba_ref[...]
    o_ref[...] = (jnp.minimum(u, 0.0) - jnp.log1p(jnp.exp(-jnp.abs(u)))) * (1.0 / GLA_TAU)


def _gla_gate(h, g, wz, wa, ba):
    m, d = h.shape
    dk = wa.shape[1]
    tm = _pick_tile(m, 640)
    return pl.pallas_call(
        _gla_gate_kernel,
        grid=(m // tm,),
        in_specs=[
            pl.BlockSpec((tm, d), lambda i: (i, 0)),
            pl.BlockSpec((1, d), lambda i: (0, 0)),
            pl.BlockSpec((d, LANES), lambda i: (0, 0)),
            pl.BlockSpec((LANES, dk), lambda i: (0, 0)),
            pl.BlockSpec((1, dk), lambda i: (0, 0)),
        ],
        out_specs=pl.BlockSpec((tm, dk), lambda i: (i, 0)),
        out_shape=jax.ShapeDtypeStruct((m, dk), F32),
        compiler_params=_cparams("parallel"),
        name="gla_gate",
    )(h, g, wz, wa, ba)


def _gla_chunk_kernel(q_ref, k_ref, v_ref, gate_ref, gl_ref, gn_ref, z_ref, st_ref, *, scale):
    c = pl.program_id(1)
    nh, hv, hk = st_ref.shape

    @pl.when(c == 0)
    def _():
        st_ref[...] = jnp.zeros_like(st_ref)

    row = lax.broadcasted_iota(jnp.int32, (CHUNK, CHUNK), 0)
    col = lax.broadcasted_iota(jnp.int32, (CHUNK, CHUNK), 1)
    causal = row >= col
    tril = jnp.where(causal, 1.0, 0.0).astype(BF16)

    gl = gl_ref[...]
    live = jnp.logical_or(c > 0, lax.broadcasted_iota(jnp.int32, gl.shape, 0) >= N_DUMMY)
    gl = jnp.where(live, gl, 0.0)
    bc = _dot_sel_left(tril, gl)
    b_last = bc[CHUNK - 1:CHUNK, :]
    e_last = jnp.exp(b_last)
    k = k_ref[...]
    q_t = (q_ref[...] * scale * jnp.exp(bc)).astype(BF16)
    k_t = (k * jnp.exp(-bc)).astype(BF16)
    k_h = (k * jnp.exp(b_last - bc)).astype(BF16)

    heads = range(nh)
    ksl = lambda x, h: x[:, h * hk:(h + 1) * hk]
    vsl = lambda x, h: x[:, h * hv:(h + 1) * hv]
    v = [vsl(v_ref[...], h).astype(BF16) for h in heads]
    att = [jnp.where(causal, _dot_nt(ksl(q_t, h), ksl(k_t, h)), 0.0).astype(BF16) for h in heads]
    st = [st_ref[h] for h in heads]
    o = [_dot_nt(ksl(q_t, h), st[h]) + jnp.dot(att[h], v[h], preferred_element_type=F32) for h in heads]
    for h in heads:
        st_ref[h] = st[h] * ksl(e_last, h) + _dot_tn(v[h], ksl(k_h, h))
    for h in heads:
        on = o[h] * lax.rsqrt(jnp.mean(o[h] * o[h], axis=-1, keepdims=True) + GLA_HEAD_EPS)
        gate = vsl(gate_ref[...], h)
        z_ref[:, h * hv:(h + 1) * hv] = (on * vsl(gn_ref[...], h) * (gate * _sigmoid(gate))).astype(z_ref.dtype)


def _gla_chunk(p, glog, gn_w, batch, d):
    m = p.shape[0]
    nc = m // batch // CHUNK
    dk = d // 2
    hk = dk // GLA_HEADS
    hv = d // GLA_HEADS
    rows = lambda bi, c: bi * nc + c
    return pl.pallas_call(
        functools.partial(_gla_chunk_kernel, scale=hk ** -0.5),
        grid=(batch, nc),
        in_specs=[
            pl.BlockSpec((CHUNK, dk), lambda bi, c: (rows(bi, c), 0)),
            pl.BlockSpec((CHUNK, dk), lambda bi, c: (rows(bi, c), 1)),
            pl.BlockSpec((CHUNK, d), lambda bi, c: (rows(bi, c), 1)),
            pl.BlockSpec((CHUNK, d), lambda bi, c: (rows(bi, c), 2)),
            pl.BlockSpec((CHUNK, dk), lambda bi, c: (rows(bi, c), 0)),
            pl.BlockSpec((1, d), lambda bi, c: (0, 0)),
        ],
        out_specs=pl.BlockSpec((CHUNK, d), lambda bi, c: (rows(bi, c), 0)),
        out_shape=jax.ShapeDtypeStruct((m, d), BF16),
        scratch_shapes=[pltpu.VMEM((GLA_HEADS, hv, hk), F32)],
        compiler_params=_cparams("parallel", "arbitrary"),
        name="gla_chunk",
    )(p, p, p, p, glog, gn_w)


def _rwkv_layer(h, batch, j, v_first, norm_g, rw_mix, rw_w_rkv, rw_w0, rw_w1, rw_w2, rw_a0, rw_a1, rw_a2,
                rw_v0, rw_v1, rw_v2, rw_g1, rw_g2, rw_k_k, rw_k_a, rw_r_k, rw_ln_w, rw_ln_b, rw_w_o):
    m, d = h.shape
    npair = d // LANES
    row = lambda t: t.reshape(1, d)
    pairs = lambda t: t.reshape(npair, 1, LANES)
    mix = rw_mix[j][jnp.array([0, 2, 3, 1, 4, 5])]
    xs = _rw_prep(h.reshape(batch, m // batch, d), row(norm_g), mix).reshape(6, m, d)
    rkv = _rkv(xs, rw_w_rkv, j)
    zeros = jnp.zeros((1, d), F32)
    lw = _lora(xs, 3, *_pad_rank(rw_w1[j], rw_w2[j]), row(rw_w0[j]), "tanh", "decay")
    ag = _lora(xs, 4, *_pad_rank(rw_a1[j], rw_a2[j]), row(rw_a0[j]), "none", "sigmoid")
    gg = _lora(xs, 5, *_pad_rank(rw_g1[j], rw_g2[j]), zeros, "sigmoid", "none")
    vres = None
    if j > 0:
        vgate = _lora(xs, 2, *_pad_rank(rw_v1[j - 1], rw_v2[j - 1]), row(rw_v0[j - 1]), "none", "sigmoid")
        vres = (v_first, vgate)
    z = _rw_scan(rkv, lw, ag, gg, vres, pairs(rw_k_k[j]), pairs(rw_k_a[j]), pairs(rw_r_k[j]),
                 pairs(rw_ln_w[j]), pairs(rw_ln_b[j]), batch)
    h = _proj_residual(z, rw_w_o, j, h, pair_major=True)
    return h, rkv


def _gla_layer(h, batch, j, norm_g, gla_w_in, gla_w_in_bf, gla_w_a2, gla_b_a, gla_gn_w, gla_w_o):
    m, d = h.shape
    dk = d // 2
    n_main = 2 * dk + 2 * d
    w_in = gla_w_in[j]
    rank = w_in.shape[1] - n_main
    g = norm_g.reshape(1, d)
    p = _norm_matmul(h, g, gla_w_in_bf, j, n_main, F32)
    wz = jnp.pad(w_in[:, n_main:], ((0, 0), (0, LANES - rank))).astype(BF16)
    wa = jnp.pad(gla_w_a2[j], ((0, LANES - rank), (0, 0)))
    glog = _gla_gate(h, g, wz, wa, gla_b_a[j].reshape(1, dk))
    z = _gla_chunk(p, glog, gla_gn_w[j].reshape(1, d), batch, d)
    return _proj_residual(z, gla_w_o, j, h, pair_major=False)


def kernel(x, meta, norm_mix, norm_mlp, norm_f, mlp_w1, mlp_w2, rw_mix, rw_w_rkv, rw_w0, rw_w1, rw_w2, rw_a0, rw_a1, rw_a2, rw_v0, rw_v1, rw_v2, rw_g1, rw_g2, rw_k_k, rw_k_a, rw_r_k, rw_ln_w, rw_ln_b, rw_w_o, gla_w_in, gla_w_a2, gla_b_a, gla_gn_w, gla_w_o):
    batch, seq, d = x.shape
    depth = norm_mix.shape[0]
    assert seq % CHUNK == 0 and d % (2 * LANES) == 0 and meta.shape[0] == N_META
    tp = LEAD + seq
    lead = jnp.concatenate([jnp.zeros((N_DUMMY, d), x.dtype), meta.astype(x.dtype)], axis=0)
    h = jnp.concatenate([jnp.broadcast_to(lead[None], (batch, LEAD, d)), x], axis=1).reshape(batch * tp, d)
    gf = norm_f.reshape(1, d)
    w1_bf, w2_bf = mlp_w1.astype(BF16), mlp_w2.astype(BF16)
    rkv_bf, rwo_bf = rw_w_rkv.astype(BF16), rw_w_o.astype(BF16)
    gla_w_in_bf, gla_w_o_bf = gla_w_in.astype(BF16), gla_w_o.astype(BF16)
    v_first = None
    for i in range(depth):
        j = i // 2
        if i % 2 == 0:
            h, rkv = _rwkv_layer(h, batch, j, v_first, norm_mix[i], rw_mix, rkv_bf, rw_w0, rw_w1, rw_w2,
                                 rw_a0, rw_a1, rw_a2, rw_v0, rw_v1, rw_v2, rw_g1, rw_g2, rw_k_k, rw_k_a,
                                 rw_r_k.reshape(rw_r_k.shape[0], d), rw_ln_w, rw_ln_b, rwo_bf)
            if j == 0:
                v_first = rkv
        else:
            h = _gla_layer(h, batch, j, norm_mix[i], gla_w_in, gla_w_in_bf, gla_w_a2, gla_b_a, gla_gn_w, gla_w_o_bf)
        h = _mlp(h, norm_mlp[i].reshape(1, d), w1_bf, w2_bf, i, gf, final=(i == depth - 1))
    return h.reshape(batch, tp, d)[:, LEAD:]
```

```python
import functools

import jax
import jax.numpy as jnp
from jax import lax
from jax.experimental import pallas as pl
from jax.experimental.pallas import tpu as pltpu

F32 = jnp.float32
BF16 = jnp.bfloat16

N_META = 16
CHUNK = 64
LEAD = CHUNK
N_DUMMY = LEAD - N_META
NORM_EPS = 1e-6

RW_HEAD = 64
LANES = 128
RW_GN_EPS = 64e-5
RW_DECAY_SCALE = 0.6065306597126334
PAIR_GROUP = 16

GLA_HEADS = 4
GLA_TAU = 16.0
GLA_HEAD_EPS = 1e-5

VMEM_LIMIT = 56 * 1024 * 1024
MLP_TILES = ((640, 1024), (832, 512), (832, 1024))


def _cparams(*sem):
    return pltpu.CompilerParams(dimension_semantics=sem, vmem_limit_bytes=VMEM_LIMIT)


def _dot(a, b):
    return jnp.dot(a.astype(BF16), b.astype(BF16), preferred_element_type=F32)


def _dot_nt(a, b):
    return lax.dot_general(a.astype(BF16), b.astype(BF16), (((1,), (1,)), ((), ())),
                           preferred_element_type=F32)


def _dot_tn(a, b):
    return lax.dot_general(a.astype(BF16), b.astype(BF16), (((0,), (0,)), ((), ())),
                           preferred_element_type=F32)


def _split2(x):
    hi = x.astype(BF16)
    return hi, (x - hi.astype(F32)).astype(BF16)


def _dot_sel_left(sel, x):
    hi, lo = _split2(x)
    return jnp.dot(sel, hi, preferred_element_type=F32) + jnp.dot(sel, lo, preferred_element_type=F32)


def _dot_sel_right(x, sel):
    hi, lo = _split2(x)
    return jnp.dot(hi, sel, preferred_element_type=F32) + jnp.dot(lo, sel, preferred_element_type=F32)


def _rms(x, g, eps):
    return x * lax.rsqrt(jnp.mean(x * x, axis=-1, keepdims=True) + eps) * g


def _sigmoid(x):
    return 1.0 / (1.0 + jnp.exp(-x))


def _pick_tile(n, target):
    best = None
    for t in range(16, min(n, target) + 1, 16):
        if n % t == 0:
            best = t
    assert best is not None, (n, target)
    return best


def _mlp_kernel(h_ref, g_ref, w1_ref, w2_ref, gf_ref, o_ref, xn_ref, *, final):
    ff_axis = 2 if final else 1
    if final:
        o_ref = o_ref.at[0]
    j = pl.program_id(ff_axis)

    @pl.when(j == 0)
    def _():
        x = h_ref[...]
        xn_ref[...] = _rms(x, g_ref[...], NORM_EPS).astype(BF16)
        o_ref[...] = x

    hid = jnp.dot(xn_ref[...], w1_ref[...], preferred_element_type=F32)
    hid = jnp.maximum(hid, 0.0)
    hid = hid * hid
    o_ref[...] += jnp.dot(hid.astype(BF16), w2_ref[...], preferred_element_type=F32)

    if final:
        @pl.when(j == pl.num_programs(ff_axis) - 1)
        def _():
            o_ref[...] = _rms(o_ref[...], gf_ref[...], NORM_EPS)


def _mlp(h, g, w1, w2, layer, gf, tm_target, tf_target):
    m, d = h.shape
    ff = w1.shape[2]
    tm = _pick_tile(m, tm_target)
    tf = _pick_tile(ff, tf_target)
    return pl.pallas_call(
        functools.partial(_mlp_kernel, final=False),
        grid=(m // tm, ff // tf),
        in_specs=[
            pl.BlockSpec((tm, d), lambda i, j: (i, 0)),
            pl.BlockSpec((1, d), lambda i, j: (0, 0)),
            pl.BlockSpec((None, d, tf), lambda i, j: (layer, 0, j)),
            pl.BlockSpec((None, tf, d), lambda i, j: (layer, j, 0)),
            pl.BlockSpec((1, d), lambda i, j: (0, 0)),
        ],
        out_specs=pl.BlockSpec((tm, d), lambda i, j: (i, 0)),
        out_shape=jax.ShapeDtypeStruct((m, d), F32),
        scratch_shapes=[pltpu.VMEM((tm, d), BF16)],
        compiler_params=_cparams("parallel", "arbitrary"),
        name="mlp",
    )(h, g, w1, w2, gf)


def _mlp_final(h3, g, w1, w2, layer, gf, tm_target, tf_target):
    b, tp, d = h3.shape
    seq = tp - LEAD
    ff = w1.shape[2]
    tm = _pick_tile(seq, tm_target)
    tf = _pick_tile(ff, tf_target)
    return pl.pallas_call(
        functools.partial(_mlp_kernel, final=True),
        grid=(b, seq // tm, ff // tf),
        in_specs=[
            pl.BlockSpec((pl.Element(tm), pl.Element(d)),
                         lambda bi, i, j: (pl.multiple_of(bi * tp + LEAD + i * tm, 16), 0)),
            pl.BlockSpec((1, d), lambda bi, i, j: (0, 0)),
            pl.BlockSpec((None, d, tf), lambda bi, i, j: (layer, 0, j)),
            pl.BlockSpec((None, tf, d), lambda bi, i, j: (layer, j, 0)),
            pl.BlockSpec((1, d), lambda bi, i, j: (0, 0)),
        ],
        out_specs=pl.BlockSpec((1, tm, d), lambda bi, i, j: (bi, i, 0)),
        out_shape=jax.ShapeDtypeStruct((b, seq, d), F32),
        scratch_shapes=[pltpu.VMEM((tm, d), BF16)],
        compiler_params=_cparams("parallel", "parallel", "arbitrary"),
        name="mlp_final",
    )(h3.reshape(b * tp, d), g, w1, w2, gf)


def _rw_prep_kernel(h_ref, hp_ref, g_ref, mix_ref, o_ref):
    i = pl.program_id(1)
    g = g_ref[...]
    hn = _rms(h_ref[0], g, NORM_EPS)
    pn = _rms(hp_ref[0], g, NORM_EPS)[7:8]
    pn = jnp.where(i == 0, 0.0, pn)
    row = lax.broadcasted_iota(jnp.int32, hn.shape, 0)
    prev = jnp.where(row == 0, pn, pltpu.roll(hn, 1, axis=0))
    xx = prev - hn
    for s in range(6):
        o_ref[s, 0] = (hn + xx * mix_ref[s:s + 1, :]).astype(BF16)


def _rw_prep(h3, g, mix):
    b, tp, d = h3.shape
    tm = _pick_tile(tp, 416)
    nb = tm // 8
    return pl.pallas_call(
        _rw_prep_kernel,
        grid=(b, tp // tm),
        in_specs=[
            pl.BlockSpec((1, tm, d), lambda bi, i: (bi, i, 0)),
            pl.BlockSpec((1, 8, d), lambda bi, i: (bi, jnp.maximum(i * nb - 1, 0), 0)),
            pl.BlockSpec((1, d), lambda bi, i: (0, 0)),
            pl.BlockSpec((6, d), lambda bi, i: (0, 0)),
        ],
        out_specs=pl.BlockSpec((6, 1, tm, d), lambda bi, i: (0, bi, i, 0)),
        out_shape=jax.ShapeDtypeStruct((6, b, tp, d), BF16),
        compiler_params=_cparams("parallel", "parallel"),
        name="rw_prep",
    )(h3, h3, g, mix)


def _to_pairs(o_ref, lead, val):
    for p in range(val.shape[1] // LANES):
        o_ref[lead + (p,)] = val[:, p * LANES:(p + 1) * LANES].astype(o_ref.dtype)


def _rkv_kernel(x_ref, w_ref, o_ref):
    _to_pairs(o_ref, (0,), jnp.dot(x_ref[0], w_ref[0], preferred_element_type=F32))


def _rkv(xs, w, layer):
    _, m, d = xs.shape
    tm = _pick_tile(m, 640)
    npair = d // LANES
    return pl.pallas_call(
        _rkv_kernel,
        grid=(3, m // tm),
        in_specs=[
            pl.BlockSpec((1, tm, d), lambda gi, i: (gi, i, 0)),
            pl.BlockSpec((None, 1, d, d), lambda gi, i: (layer, gi, 0, 0)),
        ],
        out_specs=pl.BlockSpec((1, npair, tm, LANES), lambda gi, i: (gi, 0, i, 0)),
        out_shape=jax.ShapeDtypeStruct((3, npair, m, LANES), F32),
        compiler_params=_cparams("parallel", "parallel"),
        name="rw_rkv",
    )(xs, w)


def _lora_kernel(x_ref, w1_ref, w2_ref, b_ref, o_ref, *, mid, out):
    z = jnp.dot(x_ref[0], w1_ref[...], preferred_element_type=F32)
    if mid == "tanh":
        z = jnp.tanh(z)
    elif mid == "sigmoid":
        z = _sigmoid(z)
    y = jnp.dot(z.astype(BF16), w2_ref[...], preferred_element_type=F32)
    if out == "decay":
        y = -RW_DECAY_SCALE * _sigmoid(y + b_ref[...])
    elif out == "sigmoid":
        y = _sigmoid(y + b_ref[...])
    _to_pairs(o_ref, (), y)


def _lora(xs, slot, w1, w2, bias, mid, out):
    _, m, d = xs.shape
    r = w1.shape[1]
    tm = _pick_tile(m, 640)
    npair = d // LANES
    return pl.pallas_call(
        functools.partial(_lora_kernel, mid=mid, out=out),
        grid=(m // tm,),
        in_specs=[
            pl.BlockSpec((1, tm, d), lambda i: (slot, i, 0)),
            pl.BlockSpec((d, r), lambda i: (0, 0)),
            pl.BlockSpec((r, d), lambda i: (0, 0)),
            pl.BlockSpec((1, d), lambda i: (0, 0)),
        ],
        out_specs=pl.BlockSpec((npair, tm, LANES), lambda i: (0, i, 0)),
        out_shape=jax.ShapeDtypeStruct((npair, m, LANES), F32),
        compiler_params=_cparams("parallel"),
        name="rw_lora_" + out,
    )(xs, w1, w2, bias)


def _pad_rank(w1, w2):
    r = w1.shape[1]
    rp = -(-r // LANES) * LANES
    return (jnp.pad(w1, ((0, 0), (0, rp - r))).astype(BF16),
            jnp.pad(w2, ((0, rp - r), (0, 0))).astype(BF16))


def _stack_heads(x, lane_lo):
    return jnp.concatenate([jnp.where(lane_lo, x, 0.0), jnp.where(lane_lo, 0.0, x)], axis=0)


def _rw_scan_kernel(*refs, has_vres):
    if has_vres:
        (rkv_ref, lw_ref, ag_ref, gg_ref, vf_ref, vg_ref,
         kk_ref, ka_ref, rk_ref, lnw_ref, lnb_ref, z_ref, st_ref) = refs
    else:
        (rkv_ref, lw_ref, ag_ref, gg_ref,
         kk_ref, ka_ref, rk_ref, lnw_ref, lnb_ref, z_ref, st_ref) = refs
    c = pl.program_id(1)
    npair = z_ref.shape[0]
    group = min(PAIR_GROUP, npair)
    assert npair % group == 0
    two_l = 2 * CHUNK

    @pl.when(c == 0)
    def _():
        st_ref[...] = jnp.zeros_like(st_ref)

    row = lax.broadcasted_iota(jnp.int32, (two_l, two_l), 0)
    col = lax.broadcasted_iota(jnp.int32, (two_l, two_l), 1)
    t_row = row % CHUNK
    t_col = col % CHUNK
    strict = t_row > t_col
    incl = t_row >= t_col
    same_head = (row // CHUNK) == (col // CHUNK)
    ones_bd = jnp.where(same_head, 1.0, 0.0).astype(BF16)
    tril = jnp.where(lax.broadcasted_iota(jnp.int32, (CHUNK, CHUNK), 0)
                     >= lax.broadcasted_iota(jnp.int32, (CHUNK, CHUNK), 1), 1.0, 0.0).astype(BF16)
    diag = row == col
    lane_lo = lax.broadcasted_iota(jnp.int32, (CHUNK, LANES), 1) < RW_HEAD
    live = jnp.logical_or(c > 0, lax.broadcasted_iota(jnp.int32, (CHUNK, LANES), 0) >= N_DUMMY)

    def each(fn, *lists):
        return [fn(*xs) for xs in zip(*lists)]

    def group_body(gi, carry):
        ps = [gi * group + i for i in range(group)]
        r = [rkv_ref[0, p] for p in ps]
        k0 = [rkv_ref[1, p] for p in ps]
        v = [rkv_ref[2, p] for p in ps]
        lw = [lw_ref[p] for p in ps]
        ag = [ag_ref[p] for p in ps]
        if has_vres:
            v = [vi + (vf_ref[0, p] - vi) * vg_ref[p] for vi, p in zip(v, ps)]
        kk = [ki * kk_ref[p] for ki, p in zip(k0, ps)]
        k = [ki * (1.0 + (ai - 1.0) * ka_ref[p]) for ki, ai, p in zip(k0, ag, ps)]
        sums = [_dot_sel_right(jnp.concatenate([kki * kki, ri * ki * rk_ref[p]], axis=0), ones_bd)
                for kki, ri, ki, p in zip(kk, r, k, ps)]
        kk = each(lambda kki, si: kki / jnp.maximum(jnp.sqrt(si[:CHUNK]), 1e-12), kk, sums)
        bonus = each(lambda si, vi: si[CHUNK:] * vi, sums, v)

        cs = each(lambda lwi: _dot_sel_left(tril, lwi), lw)
        e_pos = each(jnp.exp, cs)
        e_neg = each(lambda ci: jnp.exp(-ci), cs)
        w_all = each(lambda ei: ei[CHUNK - 1:CHUNK, :], e_pos)
        a_s = each(lambda kki, ci, lwi: _stack_heads(-kki * jnp.exp(ci - lwi), lane_lo), kk, cs, lw)
        r_s = each(lambda ri, ei: _stack_heads(ri * ei, lane_lo), r, e_pos)
        b_t = each(lambda kki, ai, ei: kki * ai * ei, kk, ag, e_neg)
        k_t = each(lambda ki, ei: ki * ei, k, e_neg)
        v_s = each(lambda vi: _stack_heads(vi, lane_lo).astype(BF16), v)

        gram = each(lambda asi, rsi, bti, kti: _dot_nt(
            jnp.concatenate([asi, rsi], axis=0),
            jnp.concatenate([_stack_heads(bti, lane_lo), _stack_heads(kti, lane_lo)], axis=0)),
            a_s, r_s, b_t, k_t)
        pw = each(lambda gm: jnp.where(strict, gm[:two_l, :two_l], 0.0).astype(BF16), gram)
        a_ak = each(lambda gm: jnp.where(strict, gm[:two_l, two_l:], 0.0).astype(BF16), gram)
        a_rb = each(lambda gm: jnp.where(incl, gm[two_l:, :two_l], 0.0).astype(BF16), gram)
        a_rk = each(lambda gm: jnp.where(incl, gm[two_l:, two_l:], 0.0).astype(BF16), gram)

        z = each(lambda asi, aki, vsi: jnp.concatenate(
            [asi, jnp.dot(aki, vsi, preferred_element_type=F32)], axis=1), a_s, a_ak, v_s)
        n_sq = CHUNK.bit_length() - 1
        for s in range(n_sq):
            z = each(lambda zi, pi: zi + jnp.dot(pi, zi.astype(BF16), preferred_element_type=F32), z, pw)
            if s + 1 < n_sq:
                pw = each(lambda pi: jnp.dot(pi, pi, preferred_element_type=F32).astype(BF16), pw)
        zb = each(lambda zi: zi.astype(BF16), z)

        rz = each(lambda ai, zi: jnp.dot(ai, zi, preferred_element_type=F32), a_rb, zb)
        r_hat = each(lambda rsi, rzi: (rsi + rzi[:, :LANES]).astype(BF16), r_s, rz)
        y_hat = each(lambda rzi, ai, vsi: rzi[:, LANES:] + jnp.dot(ai, vsi, preferred_element_type=F32),
                     rz, a_rk, v_s)
        bz = each(lambda bti, wi, zi: _dot_tn(_stack_heads(bti * wi, lane_lo), zi), b_t, w_all, zb)
        m_mat = each(lambda wi, bzi: (jnp.where(diag, wi, 0.0) + bzi[:, :LANES]).astype(BF16), w_all, bz)
        n_mat = each(lambda bzi, kti, wi, vsi: bzi[:, LANES:] + _dot_tn(_stack_heads(kti * wi, lane_lo), vsi),
                     bz, k_t, w_all, v_s)

        hb = [st_ref[p].astype(BF16) for p in ps]
        y2 = each(lambda rh, hi, yh: jnp.dot(rh, hi, preferred_element_type=F32) + yh, r_hat, hb, y_hat)
        st = each(lambda mm, hi, nm: jnp.dot(mm, hi, preferred_element_type=F32) + nm, m_mat, hb, n_mat)
        for p, si in zip(ps, st):
            st_ref[p] = si
        y = each(lambda yi: yi[:CHUNK] + yi[CHUNK:], y2)

        dy = each(lambda yi: yi - _dot_sel_right(yi, ones_bd) * (1.0 / RW_HEAD), y)
        var = each(lambda di: _dot_sel_right(di * di, ones_bd) * (1.0 / RW_HEAD), dy)
        for p, di, vi, bi in zip(ps, dy, var, bonus):
            yn = di * lax.rsqrt(vi + RW_GN_EPS) * lnw_ref[p] + lnb_ref[p]
            z_ref[p] = jnp.where(live, (yn + bi) * gg_ref[p], 0.0).astype(z_ref.dtype)
        return carry

    lax.fori_loop(0, npair // group, group_body, 0)


def _rw_scan(rkv, lw, ag, gg, vres, kk, ka, rk, lnw, lnb, batch):
    _, npair, m, _ = rkv.shape
    nc = m // batch // CHUNK
    row_map3 = lambda bi, c: (0, bi * nc + c, 0)
    row_map4 = lambda bi, c: (0, 0, bi * nc + c, 0)
    par = pl.BlockSpec((npair, 1, LANES), lambda bi, c: (0, 0, 0))
    tile = pl.BlockSpec((npair, CHUNK, LANES), row_map3)
    in_specs = [pl.BlockSpec((3, npair, CHUNK, LANES), row_map4), tile, tile, tile]
    args = [rkv, lw, ag, gg]
    if vres is not None:
        vfirst, vgate = vres
        in_specs += [pl.BlockSpec((1, npair, CHUNK, LANES), lambda bi, c: (2, 0, bi * nc + c, 0)), tile]
        args += [vfirst, vgate]
    in_specs += [par] * 5
    args += [kk, ka, rk, lnw, lnb]
    return pl.pallas_call(
        functools.partial(_rw_scan_kernel, has_vres=vres is not None),
        grid=(batch, nc),
        in_specs=in_specs,
        out_specs=tile,
        out_shape=jax.ShapeDtypeStruct((npair, m, LANES), BF16),
        scratch_shapes=[pltpu.VMEM((npair, 2 * CHUNK, LANES), F32)],
        compiler_params=_cparams("parallel", "arbitrary"),
        name="rw_scan",
    )(*args)


def _proj_kernel(x_ref, w_ref, h_ref, o_ref, *, pair_major):
    if pair_major:
        x = jnp.concatenate([x_ref[p] for p in range(x_ref.shape[0])], axis=1)
    else:
        x = x_ref[...]
    o_ref[...] = h_ref[...] + jnp.dot(x, w_ref[...], preferred_element_type=F32)


def _proj_residual(x, w, layer, h, pair_major):
    m, d = h.shape
    kdim = w.shape[1]
    tm = _pick_tile(m, 640)
    if pair_major:
        x_spec = pl.BlockSpec((x.shape[0], tm, LANES), lambda i: (0, i, 0))
    else:
        x_spec = pl.BlockSpec((tm, kdim), lambda i: (i, 0))
    return pl.pallas_call(
        functools.partial(_proj_kernel, pair_major=pair_major),
        grid=(m // tm,),
        in_specs=[x_spec,
                  pl.BlockSpec((None, kdim, d), lambda i: (layer, 0, 0)),
                  pl.BlockSpec((tm, d), lambda i: (i, 0))],
        out_specs=pl.BlockSpec((tm, d), lambda i: (i, 0)),
        out_shape=jax.ShapeDtypeStruct((m, d), F32),
        compiler_params=_cparams("parallel"),
        name="proj_residual",
    )(x, w, h)


def _gla_in_kernel(h_ref, g_ref, w_ref, wz_ref, wa_ref, ba_ref, p_ref, gl_ref, xn_ref):
    @pl.when(pl.program_id(1) == 0)
    def _():
        xn = _rms(h_ref[...], g_ref[...], NORM_EPS).astype(BF16)
        xn_ref[...] = xn
        za = jnp.dot(xn, wz_ref[...], preferred_element_type=F32)
        u = _dot(za, wa_ref[...]) + ba_ref[...]
        gl_ref[...] = (jnp.minimum(u, 0.0) - jnp.log1p(jnp.exp(-jnp.abs(u)))) * (1.0 / GLA_TAU)

    p_ref[...] = jnp.dot(xn_ref[...], w_ref[...], preferred_element_type=F32).astype(p_ref.dtype)


def _gla_in(h, g, w, layer, n, wz, wa, ba):
    m, d = h.shape
    dk = wa.shape[1]
    tm = _pick_tile(m, 640)
    tn = _pick_tile(n, 2048)
    return pl.pallas_call(
        _gla_in_kernel,
        grid=(m // tm, n // tn),
        in_specs=[
            pl.BlockSpec((tm, d), lambda i, j: (i, 0)),
            pl.BlockSpec((1, d), lambda i, j: (0, 0)),
            pl.BlockSpec((None, d, tn), lambda i, j: (layer, 0, j)),
            pl.BlockSpec((d, LANES), lambda i, j: (0, 0)),
            pl.BlockSpec((LANES, dk), lambda i, j: (0, 0)),
            pl.BlockSpec((1, dk), lambda i, j: (0, 0)),
        ],
        out_specs=[pl.BlockSpec((tm, tn), lambda i, j: (i, j)),
                   pl.BlockSpec((tm, dk), lambda i, j: (i, 0))],
        out_shape=[jax.ShapeDtypeStruct((m, n), BF16), jax.ShapeDtypeStruct((m, dk), F32)],
        scratch_shapes=[pltpu.VMEM((tm, d), BF16)],
        compiler_params=_cparams("parallel", "arbitrary"),
        name="gla_in",
    )(h, g, w, wz, wa, ba)


def _gla_chunk_kernel(q_ref, k_ref, v_ref, gate_ref, gl_ref, gn_ref, z_ref, st_ref, *, scale):
    c = pl.program_id(1)
    nh, hv, hk = st_ref.shape

    @pl.when(c == 0)
    def _():
        st_ref[...] = jnp.zeros_like(st_ref)

    row = lax.broadcasted_iota(jnp.int32, (CHUNK, CHUNK), 0)
    col = lax.broadcasted_iota(jnp.int32, (CHUNK, CHUNK), 1)
    causal = row >= col
    tril = jnp.where(causal, 1.0, 0.0).astype(BF16)

    gl = gl_ref[...]
    live = jnp.logical_or(c > 0, lax.broadcasted_iota(jnp.int32, gl.shape, 0) >= N_DUMMY)
    gl = jnp.where(live, gl, 0.0)
    bc = _dot_sel_left(tril, gl)
    b_last = bc[CHUNK - 1:CHUNK, :]
    e_last = jnp.exp(b_last)
    k = k_ref[...]
    q_t = (q_ref[...] * scale * jnp.exp(bc)).astype(BF16)
    k_t = (k * jnp.exp(-bc)).astype(BF16)
    k_h = (k * jnp.exp(b_last - bc)).astype(BF16)

    heads = range(nh)
    ksl = lambda x, h: x[:, h * hk:(h + 1) * hk]
    vsl = lambda x, h: x[:, h * hv:(h + 1) * hv]
    v = [vsl(v_ref[...], h).astype(BF16) for h in heads]
    att = [jnp.where(causal, _dot_nt(ksl(q_t, h), ksl(k_t, h)), 0.0).astype(BF16) for h in heads]
    st = [st_ref[h] for h in heads]
    o = [_dot_nt(ksl(q_t, h), st[h]) + jnp.dot(att[h], v[h], preferred_element_type=F32) for h in heads]
    for h in heads:
        st_ref[h] = st[h] * ksl(e_last, h) + _dot_tn(v[h], ksl(k_h, h))
    for h in heads:
        on = o[h] * lax.rsqrt(jnp.mean(o[h] * o[h], axis=-1, keepdims=True) + GLA_HEAD_EPS)
        gate = vsl(gate_ref[...], h).astype(F32)
        z_ref[:, h * hv:(h + 1) * hv] = (on * vsl(gn_ref[...], h) * (gate * _sigmoid(gate))).astype(z_ref.dtype)


def _gla_chunk(p, glog, gn_w, batch, d):
    m = p.shape[0]
    nc = m // batch // CHUNK
    dk = d // 2
    hk = dk // GLA_HEADS
    hv = d // GLA_HEADS
    rows = lambda bi, c: bi * nc + c
    return pl.pallas_call(
        functools.partial(_gla_chunk_kernel, scale=hk ** -0.5),
        grid=(batch, nc),
        in_specs=[
            pl.BlockSpec((CHUNK, dk), lambda bi, c: (rows(bi, c), 0)),
            pl.BlockSpec((CHUNK, dk), lambda bi, c: (rows(bi, c), 1)),
            pl.BlockSpec((CHUNK, d), lambda bi, c: (rows(bi, c), 1)),
            pl.BlockSpec((CHUNK, d), lambda bi, c: (rows(bi, c), 2)),
            pl.BlockSpec((CHUNK, dk), lambda bi, c: (rows(bi, c), 0)),
            pl.BlockSpec((1, d), lambda bi, c: (0, 0)),
        ],
        out_specs=pl.BlockSpec((CHUNK, d), lambda bi, c: (rows(bi, c), 0)),
        out_shape=jax.ShapeDtypeStruct((m, d), BF16),
        scratch_shapes=[pltpu.VMEM((GLA_HEADS, hv, hk), F32)],
        compiler_params=_cparams("parallel", "arbitrary"),
        name="gla_chunk",
    )(p, p, p, p, glog, gn_w)


def _rwkv_layer(h, batch, j, v_first, norm_g, rw_mix, rw_w_rkv, rw_w0, rw_w1, rw_w2, rw_a0, rw_a1, rw_a2,
                rw_v0, rw_v1, rw_v2, rw_g1, rw_g2, rw_k_k, rw_k_a, rw_r_k, rw_ln_w, rw_ln_b, rw_w_o):
    m, d = h.shape
    npair = d // LANES
    row = lambda t: t.reshape(1, d)
    pairs = lambda t: t.reshape(npair, 1, LANES)
    mix = rw_mix[j][jnp.array([0, 2, 3, 1, 4, 5])]
    xs = _rw_prep(h.reshape(batch, m // batch, d), row(norm_g), mix).reshape(6, m, d)
    rkv = _rkv(xs, rw_w_rkv, j)
    zeros = jnp.zeros((1, d), F32)
    lw = _lora(xs, 3, *_pad_rank(rw_w1[j], rw_w2[j]), row(rw_w0[j]), "tanh", "decay")
    ag = _lora(xs, 4, *_pad_rank(rw_a1[j], rw_a2[j]), row(rw_a0[j]), "none", "sigmoid")
    gg = _lora(xs, 5, *_pad_rank(rw_g1[j], rw_g2[j]), zeros, "sigmoid", "none")
    vres = None
    if j > 0:
        vgate = _lora(xs, 2, *_pad_rank(rw_v1[j - 1], rw_v2[j - 1]), row(rw_v0[j - 1]), "none", "sigmoid")
        vres = (v_first, vgate)
    z = _rw_scan(rkv, lw, ag, gg, vres, pairs(rw_k_k[j]), pairs(rw_k_a[j]), pairs(rw_r_k[j]),
                 pairs(rw_ln_w[j]), pairs(rw_ln_b[j]), batch)
    h = _proj_residual(z, rw_w_o, j, h, pair_major=True)
    return h, rkv


def _gla_layer(h, batch, j, norm_g, gla_w_in, gla_w_in_bf, gla_w_a2, gla_b_a, gla_gn_w, gla_w_o):
    m, d = h.shape
    dk = d // 2
    n_main = 2 * dk + 2 * d
    w_in = gla_w_in[j]
    rank = w_in.shape[1] - n_main
    g = norm_g.reshape(1, d)
    wz = jnp.pad(w_in[:, n_main:], ((0, 0), (0, LANES - rank))).astype(BF16)
    wa = jnp.pad(gla_w_a2[j], ((0, LANES - rank), (0, 0)))
    p, glog = _gla_in(h, g, gla_w_in_bf, j, n_main, wz, wa, gla_b_a[j].reshape(1, dk))
    z = _gla_chunk(p, glog, gla_gn_w[j].reshape(1, d), batch, d)
    return _proj_residual(z, gla_w_o, j, h, pair_major=False)


def kernel(x, meta, norm_mix, norm_mlp, norm_f, mlp_w1, mlp_w2, rw_mix, rw_w_rkv, rw_w0, rw_w1, rw_w2, rw_a0, rw_a1, rw_a2, rw_v0, rw_v1, rw_v2, rw_g1, rw_g2, rw_k_k, rw_k_a, rw_r_k, rw_ln_w, rw_ln_b, rw_w_o, gla_w_in, gla_w_a2, gla_b_a, gla_gn_w, gla_w_o):
    batch, seq, d = x.shape
    depth = norm_mix.shape[0]
    assert seq % CHUNK == 0 and d % (2 * LANES) == 0 and meta.shape[0] == N_META
    tp = LEAD + seq
    lead = jnp.concatenate([jnp.zeros((N_DUMMY, d), x.dtype), meta.astype(x.dtype)], axis=0)
    h = jnp.concatenate([jnp.broadcast_to(lead[None], (batch, LEAD, d)), x], axis=1).reshape(batch * tp, d)
    gf = norm_f.reshape(1, d)
    w1_bf, w2_bf = mlp_w1.astype(BF16), mlp_w2.astype(BF16)
    rkv_bf, rwo_bf = rw_w_rkv.astype(BF16), rw_w_o.astype(BF16)
    gla_w_in_bf, gla_w_o_bf = gla_w_in.astype(BF16), gla_w_o.astype(BF16)
    v_first = None
    for i in range(depth):
        j = i // 2
        if i % 2 == 0:
            h, rkv = _rwkv_layer(h, batch, j, v_first, norm_mix[i], rw_mix, rkv_bf, rw_w0, rw_w1, rw_w2,
                                 rw_a0, rw_a1, rw_a2, rw_v0, rw_v1, rw_v2, rw_g1, rw_g2, rw_k_k, rw_k_a,
                                 rw_r_k.reshape(rw_r_k.shape[0], d), rw_ln_w, rw_ln_b, rwo_bf)
            if j == 0:
                v_first = rkv
        else:
            h = _gla_layer(h, batch, j, norm_mix[i], gla_w_in, gla_w_in_bf, gla_w_a2, gla_b_a, gla_gn_w, gla_w_o_bf)
        g_mlp = norm_mlp[i].reshape(1, d)
        if i == depth - 1:
            return _mlp_final(h.reshape(batch, tp, d), g_mlp, w1_bf, w2_bf, i, gf, 512, 1024)
        h = _mlp(h, g_mlp, w1_bf, w2_bf, i, gf, *MLP_TILES[i % len(MLP_TILES)])
```

```python
import functools

import jax
import jax.numpy as jnp
from jax import lax
from jax.experimental import pallas as pl
from jax.experimental.pallas import tpu as pltpu

F32 = jnp.float32
BF16 = jnp.bfloat16

N_META = 16
CHUNK = 64
LEAD = CHUNK
N_DUMMY = LEAD - N_META
NORM_EPS = 1e-6

RW_HEAD = 64
LANES = 128
RW_GN_EPS = 64e-5
RW_DECAY_SCALE = 0.6065306597126334
PAIR_GROUP = 16

GLA_HEADS = 4
GLA_TAU = 16.0
GLA_HEAD_EPS = 1e-5

VMEM_LIMIT = 56 * 1024 * 1024
MLP_TILES = ((640, 1024), (832, 512), (832, 1024))


def _cparams(*sem):
    return pltpu.CompilerParams(dimension_semantics=sem, vmem_limit_bytes=VMEM_LIMIT)


def _dot(a, b):
    return jnp.dot(a.astype(BF16), b.astype(BF16), preferred_element_type=F32)


def _dot_nt(a, b):
    return lax.dot_general(a.astype(BF16), b.astype(BF16), (((1,), (1,)), ((), ())),
                           preferred_element_type=F32)


def _dot_tn(a, b):
    return lax.dot_general(a.astype(BF16), b.astype(BF16), (((0,), (0,)), ((), ())),
                           preferred_element_type=F32)


def _split2(x):
    hi = x.astype(BF16)
    return hi, (x - hi.astype(F32)).astype(BF16)


def _dot_sel_left(sel, x):
    hi, lo = _split2(x)
    return jnp.dot(sel, hi, preferred_element_type=F32) + jnp.dot(sel, lo, preferred_element_type=F32)


def _rms(x, g, eps):
    return x * lax.rsqrt(jnp.mean(x * x, axis=-1, keepdims=True) + eps) * g


def _sigmoid(x):
    return 1.0 / (1.0 + jnp.exp(-x))


def _pick_tile(n, target):
    best = None
    for t in range(16, min(n, target) + 1, 16):
        if n % t == 0:
            best = t
    assert best is not None, (n, target)
    return best


def _mlp_kernel(h_ref, g_ref, w1_ref, w2_ref, gf_ref, o_ref, xn_ref, *, final):
    ff_axis = 2 if final else 1
    if final:
        o_ref = o_ref.at[0]
    j = pl.program_id(ff_axis)

    @pl.when(j == 0)
    def _():
        x = h_ref[...]
        xn_ref[...] = _rms(x, g_ref[...], NORM_EPS).astype(BF16)
        o_ref[...] = x

    hid = jnp.dot(xn_ref[...], w1_ref[...], preferred_element_type=F32)
    hid = jnp.maximum(hid, 0.0)
    hid = hid * hid
    o_ref[...] += jnp.dot(hid.astype(BF16), w2_ref[...], preferred_element_type=F32)

    if final:
        @pl.when(j == pl.num_programs(ff_axis) - 1)
        def _():
            o_ref[...] = _rms(o_ref[...], gf_ref[...], NORM_EPS)


def _mlp(h, g, w1, w2, layer, gf, tm_target, tf_target):
    m, d = h.shape
    ff = w1.shape[2]
    tm = _pick_tile(m, tm_target)
    tf = _pick_tile(ff, tf_target)
    return pl.pallas_call(
        functools.partial(_mlp_kernel, final=False),
        grid=(m // tm, ff // tf),
        in_specs=[
            pl.BlockSpec((tm, d), lambda i, j: (i, 0)),
            pl.BlockSpec((1, d), lambda i, j: (0, 0)),
            pl.BlockSpec((None, d, tf), lambda i, j: (layer, 0, j)),
            pl.BlockSpec((None, tf, d), lambda i, j: (layer, j, 0)),
            pl.BlockSpec((1, d), lambda i, j: (0, 0)),
        ],
        out_specs=pl.BlockSpec((tm, d), lambda i, j: (i, 0)),
        out_shape=jax.ShapeDtypeStruct((m, d), F32),
        scratch_shapes=[pltpu.VMEM((tm, d), BF16)],
        compiler_params=_cparams("parallel", "arbitrary"),
        name="mlp",
    )(h, g, w1, w2, gf)


def _mlp_final(h3, g, w1, w2, layer, gf, tm_target, tf_target):
    b, tp, d = h3.shape
    seq = tp - LEAD
    ff = w1.shape[2]
    tm = _pick_tile(seq, tm_target)
    tf = _pick_tile(ff, tf_target)
    return pl.pallas_call(
        functools.partial(_mlp_kernel, final=True),
        grid=(b, seq // tm, ff // tf),
        in_specs=[
            pl.BlockSpec((pl.Element(tm), pl.Element(d)),
                         lambda bi, i, j: (pl.multiple_of(bi * tp + LEAD + i * tm, 16), 0)),
            pl.BlockSpec((1, d), lambda bi, i, j: (0, 0)),
            pl.BlockSpec((None, d, tf), lambda bi, i, j: (layer, 0, j)),
            pl.BlockSpec((None, tf, d), lambda bi, i, j: (layer, j, 0)),
            pl.BlockSpec((1, d), lambda bi, i, j: (0, 0)),
        ],
        out_specs=pl.BlockSpec((1, tm, d), lambda bi, i, j: (bi, i, 0)),
        out_shape=jax.ShapeDtypeStruct((b, seq, d), F32),
        scratch_shapes=[pltpu.VMEM((tm, d), BF16)],
        compiler_params=_cparams("parallel", "parallel", "arbitrary"),
        name="mlp_final",
    )(h3.reshape(b * tp, d), g, w1, w2, gf)


def _shifted_norm(h_ref, hp_ref, g_ref):
    g = g_ref[...]
    hn = _rms(h_ref[0], g, NORM_EPS)
    pn = _rms(hp_ref[0], g, NORM_EPS)[7:8]
    pn = jnp.where(pl.program_id(1) == 0, 0.0, pn)
    row = lax.broadcasted_iota(jnp.int32, hn.shape, 0)
    prev = jnp.where(row == 0, pn, pltpu.roll(hn, 1, axis=0))
    return hn, prev - hn


def _to_pairs(o_ref, lead, val):
    for p in range(val.shape[1] // LANES):
        o_ref[lead + (p,)] = val[:, p * LANES:(p + 1) * LANES].astype(o_ref.dtype)


def _shift_specs(tm, d):
    nb = tm // 8
    return [
        pl.BlockSpec((1, tm, d), lambda bi, i: (bi, i, 0)),
        pl.BlockSpec((1, 8, d), lambda bi, i: (bi, jnp.maximum(i * nb - 1, 0), 0)),
        pl.BlockSpec((1, d), lambda bi, i: (0, 0)),
        pl.BlockSpec((6, d), lambda bi, i: (0, 0)),
    ]


RKV_MIX_ROWS = (0, 2, 3)


def _rkv_kernel(h_ref, hp_ref, g_ref, mix_ref, w_ref, o_ref):
    hn, xx = _shifted_norm(h_ref, hp_ref, g_ref)
    for s, mrow in enumerate(RKV_MIX_ROWS):
        x = (hn + xx * mix_ref[mrow:mrow + 1, :]).astype(BF16)
        _to_pairs(o_ref, (s,), jnp.dot(x, w_ref[s], preferred_element_type=F32))


def _rkv(h3, g, mix, w, layer):
    b, tp, d = h3.shape
    tm = _pick_tile(tp, 208)
    nt = tp // tm
    npair = d // LANES
    return pl.pallas_call(
        _rkv_kernel,
        grid=(b, nt),
        in_specs=_shift_specs(tm, d) + [
            pl.BlockSpec((None, 3, d, d), lambda bi, i: (layer, 0, 0, 0), pipeline_mode=pl.Buffered(1)),
        ],
        out_specs=pl.BlockSpec((3, npair, tm, LANES), lambda bi, i: (0, 0, bi * nt + i, 0)),
        out_shape=jax.ShapeDtypeStruct((3, npair, b * tp, LANES), F32),
        compiler_params=_cparams("parallel", "parallel"),
        name="rw_rkv",
    )(h3, h3, g, mix, w)


def _lora_kernel(*refs, branches):
    h_ref, hp_ref, g_ref, mix_ref = refs[:4]
    nb = len(branches)
    w_refs = refs[4:4 + 3 * nb]
    o_refs = refs[4 + 3 * nb:]
    hn, xx = _shifted_norm(h_ref, hp_ref, g_ref)
    for bi, (mrow, mid, out) in enumerate(branches):
        w1_ref, w2_ref, b_ref = w_refs[3 * bi:3 * bi + 3]
        x = (hn + xx * mix_ref[mrow:mrow + 1, :]).astype(BF16)
        z = jnp.dot(x, w1_ref[...], preferred_element_type=F32)
        if mid == "tanh":
            z = jnp.tanh(z)
        elif mid == "sigmoid":
            z = _sigmoid(z)
        y = jnp.dot(z.astype(BF16), w2_ref[...], preferred_element_type=F32)
        if out == "decay":
            y = -RW_DECAY_SCALE * _sigmoid(y + b_ref[...])
        elif out == "sigmoid":
            y = _sigmoid(y + b_ref[...])
        _to_pairs(o_refs[bi], (), y)


def _lora(h3, g, mix, branches, weights):
    b, tp, d = h3.shape
    tm = _pick_tile(tp, 416)
    nt = tp // tm
    npair = d // LANES
    w_specs, w_args = [], []
    for w1, w2, bias in weights:
        r = w1.shape[1]
        w_specs += [pl.BlockSpec((d, r), lambda bi, i: (0, 0)),
                    pl.BlockSpec((r, d), lambda bi, i: (0, 0)),
                    pl.BlockSpec((1, d), lambda bi, i: (0, 0))]
        w_args += [w1, w2, bias]
    out_spec = pl.BlockSpec((npair, tm, LANES), lambda bi, i: (0, bi * nt + i, 0))
    out_shape = jax.ShapeDtypeStruct((npair, b * tp, LANES), F32)
    return pl.pallas_call(
        functools.partial(_lora_kernel, branches=branches),
        grid=(b, nt),
        in_specs=_shift_specs(tm, d) + w_specs,
        out_specs=[out_spec] * len(branches),
        out_shape=[out_shape] * len(branches),
        compiler_params=_cparams("parallel", "parallel"),
        name="rw_lora",
    )(h3, h3, g, mix, *w_args)


def _pad_rank(w1, w2):
    r = w1.shape[1]
    rp = -(-r // LANES) * LANES
    return (jnp.pad(w1, ((0, 0), (0, rp - r))).astype(BF16),
            jnp.pad(w2, ((0, rp - r), (0, 0))).astype(BF16))


def _stack_heads(x, lane_lo):
    return jnp.concatenate([jnp.where(lane_lo, x, 0.0), jnp.where(lane_lo, 0.0, x)], axis=0)


def _rw_scan_kernel(*refs, has_vres):
    if has_vres:
        (rkv_ref, lw_ref, ag_ref, gg_ref, vf_ref, vg_ref,
         kk_ref, ka_ref, rk_ref, lnw_ref, lnb_ref, z_ref, st_ref) = refs
    else:
        (rkv_ref, lw_ref, ag_ref, gg_ref,
         kk_ref, ka_ref, rk_ref, lnw_ref, lnb_ref, z_ref, st_ref) = refs
    c = pl.program_id(1)
    npair = z_ref.shape[0]
    group = min(PAIR_GROUP, npair)
    assert npair % group == 0
    two_l = 2 * CHUNK

    @pl.when(c == 0)
    def _():
        st_ref[...] = jnp.zeros_like(st_ref)

    row = lax.broadcasted_iota(jnp.int32, (two_l, two_l), 0)
    col = lax.broadcasted_iota(jnp.int32, (two_l, two_l), 1)
    t_row = row % CHUNK
    t_col = col % CHUNK
    strict = t_row > t_col
    incl = t_row >= t_col
    same_head = (row // CHUNK) == (col // CHUNK)
    tril = jnp.where(lax.broadcasted_iota(jnp.int32, (CHUNK, CHUNK), 0)
                     >= lax.broadcasted_iota(jnp.int32, (CHUNK, CHUNK), 1), 1.0, 0.0).astype(BF16)
    diag = row == col
    lane_lo = lax.broadcasted_iota(jnp.int32, (CHUNK, LANES), 1) < RW_HEAD
    live = jnp.logical_or(c > 0, lax.broadcasted_iota(jnp.int32, (CHUNK, LANES), 0) >= N_DUMMY)

    def each(fn, *lists):
        return [fn(*xs) for xs in zip(*lists)]

    def head_sum(x):
        lo = jnp.sum(jnp.where(lane_lo, x, 0.0), axis=-1, keepdims=True)
        hi = jnp.sum(jnp.where(lane_lo, 0.0, x), axis=-1, keepdims=True)
        return jnp.where(lane_lo, lo, hi)

    def swap_halves(x):
        return pltpu.roll(x, RW_HEAD, axis=1)

    def group_body(gi, carry):
        ps = [gi * group + i for i in range(group)]
        r = [rkv_ref[0, p] for p in ps]
        k0 = [rkv_ref[1, p] for p in ps]
        v = [rkv_ref[2, p] for p in ps]
        lw = [lw_ref[p] for p in ps]
        ag = [ag_ref[p] for p in ps]
        if has_vres:
            v = [vi + (vf_ref[0, p] - vi) * vg_ref[p] for vi, p in zip(v, ps)]
        kk = [ki * kk_ref[p] for ki, p in zip(k0, ps)]
        k = [ki * (1.0 + (ai - 1.0) * ka_ref[p]) for ki, ai, p in zip(k0, ag, ps)]
        kk = each(lambda kki: kki / jnp.maximum(jnp.sqrt(head_sum(kki * kki)), 1e-12), kk)
        bonus = [head_sum(ri * ki * rk_ref[p]) * vi for ri, ki, vi, p in zip(r, k, v, ps)]

        cs = each(lambda lwi: _dot_sel_left(tril, lwi), lw)
        e_pos = each(jnp.exp, cs)
        e_neg = each(lambda ci: jnp.exp(-ci), cs)
        w_all = each(lambda ei: ei[CHUNK - 1:CHUNK, :], e_pos)
        a_s = each(lambda kki, ci, lwi: _stack_heads(-kki * jnp.exp(ci - lwi), lane_lo), kk, cs, lw)
        r_s = each(lambda ri, ei: _stack_heads(ri * ei, lane_lo), r, e_pos)
        b_t = each(lambda kki, ai, ei: kki * ai * ei, kk, ag, e_neg)
        k_t = each(lambda ki, ei: ki * ei, k, e_neg)
        v_s = each(lambda vi: _stack_heads(vi, lane_lo).astype(BF16), v)
        bh_t = each(lambda bti, wi: _stack_heads(bti * wi, lane_lo).T.astype(BF16), b_t, w_all)
        kh_t = each(lambda kti, wi: _stack_heads(kti * wi, lane_lo).T.astype(BF16), k_t, w_all)

        gram = each(lambda asi, rsi, bti, kti: _dot_nt(
            jnp.concatenate([asi, rsi], axis=0),
            jnp.concatenate([_stack_heads(bti, lane_lo), _stack_heads(kti, lane_lo)], axis=0)),
            a_s, r_s, b_t, k_t)
        pw = each(lambda gm: jnp.where(strict, gm[:two_l, :two_l], 0.0).astype(BF16), gram)
        a_ak = each(lambda gm: jnp.where(strict, gm[:two_l, two_l:], 0.0).astype(BF16), gram)
        a_rb = each(lambda gm: jnp.where(incl, gm[two_l:, :two_l], 0.0).astype(BF16), gram)
        a_rk = each(lambda gm: jnp.where(incl, gm[two_l:, two_l:], 0.0).astype(BF16), gram)

        xv = each(lambda aki, ari, khi, vsi: jnp.dot(jnp.concatenate([aki, ari, khi], axis=0), vsi,
                                                     preferred_element_type=F32), a_ak, a_rk, kh_t, v_s)

        z = each(lambda asi, xi: asi + swap_halves(xi[:two_l]), a_s, xv)
        n_sq = CHUNK.bit_length() - 1
        for s in range(n_sq):
            if s + 1 < n_sq:
                prod = each(lambda pi, zi: jnp.dot(pi, jnp.concatenate([zi.astype(BF16), pi], axis=1),
                                                   preferred_element_type=F32), pw, z)
                z = each(lambda zi, pr: zi + pr[:, :LANES], z, prod)
                pw = each(lambda pr: pr[:, LANES:].astype(BF16), prod)
            else:
                z = each(lambda zi, pi: zi + jnp.dot(pi, zi.astype(BF16), preferred_element_type=F32), z, pw)
        zb = each(lambda zi: zi.astype(BF16), z)

        rb = each(lambda ai, bhi, zi: jnp.dot(jnp.concatenate([ai, bhi], axis=0), zi,
                                              preferred_element_type=F32), a_rb, bh_t, zb)
        r_hat = each(lambda rsi, rbi: rsi + jnp.where(same_head, rbi[:two_l], 0.0), r_s, rb)
        y_hat = each(lambda rbi, xi: swap_halves(jnp.where(same_head, 0.0, rbi[:two_l])) + xi[two_l:2 * two_l],
                     rb, xv)
        m_mat = each(lambda wi, rbi: jnp.where(diag, wi, 0.0) + jnp.where(same_head, rbi[two_l:], 0.0),
                     w_all, rb)
        n_mat = each(lambda rbi, xi: swap_halves(jnp.where(same_head, 0.0, rbi[two_l:])) + xi[2 * two_l:],
                     rb, xv)

        hb = [st_ref[p].astype(BF16) for p in ps]
        ys = each(lambda rh, mm, hi: jnp.dot(jnp.concatenate([rh, mm], axis=0).astype(BF16), hi,
                                             preferred_element_type=F32), r_hat, m_mat, hb)
        for p, ysi, nm in zip(ps, ys, n_mat):
            st_ref[p] = ysi[two_l:] + nm
        y2 = each(lambda ysi, yh: ysi[:two_l] + yh, ys, y_hat)
        y = each(lambda yi: yi[:CHUNK] + yi[CHUNK:], y2)

        dy = each(lambda yi: yi - head_sum(yi) * (1.0 / RW_HEAD), y)
        var = each(lambda di: head_sum(di * di) * (1.0 / RW_HEAD), dy)
        for p, di, vi, bi in zip(ps, dy, var, bonus):
            yn = di * lax.rsqrt(vi + RW_GN_EPS) * lnw_ref[p] + lnb_ref[p]
            z_ref[p] = jnp.where(live, (yn + bi) * gg_ref[p], 0.0).astype(z_ref.dtype)
        return carry

    lax.fori_loop(0, npair // group, group_body, 0)


def _rw_scan(rkv, lw, ag, gg, vres, kk, ka, rk, lnw, lnb, batch):
    _, npair, m, _ = rkv.shape
    nc = m // batch // CHUNK
    row_map3 = lambda bi, c: (0, bi * nc + c, 0)
    row_map4 = lambda bi, c: (0, 0, bi * nc + c, 0)
    par = pl.BlockSpec((npair, 1, LANES), lambda bi, c: (0, 0, 0))
    tile = pl.BlockSpec((npair, CHUNK, LANES), row_map3)
    in_specs = [pl.BlockSpec((3, npair, CHUNK, LANES), row_map4), tile, tile, tile]
    args = [rkv, lw, ag, gg]
    if vres is not None:
        vfirst, vgate = vres
        in_specs += [pl.BlockSpec((1, npair, CHUNK, LANES), lambda bi, c: (2, 0, bi * nc + c, 0)), tile]
        args += [vfirst, vgate]
    in_specs += [par] * 5
    args += [kk, ka, rk, lnw, lnb]
    return pl.pallas_call(
        functools.partial(_rw_scan_kernel, has_vres=vres is not None),
        grid=(batch, nc),
        in_specs=in_specs,
        out_specs=tile,
        out_shape=jax.ShapeDtypeStruct((npair, m, LANES), BF16),
        scratch_shapes=[pltpu.VMEM((npair, 2 * CHUNK, LANES), F32)],
        compiler_params=_cparams("parallel", "arbitrary"),
        name="rw_scan",
    )(*args)


def _proj_kernel(x_ref, w_ref, h_ref, o_ref, *, pair_major):
    if pair_major:
        x = jnp.concatenate([x_ref[p] for p in range(x_ref.shape[0])], axis=1)
    else:
        x = x_ref[...]
    o_ref[...] = h_ref[...] + jnp.dot(x, w_ref[...], preferred_element_type=F32)


def _proj_residual(x, w, layer, h, pair_major):
    m, d = h.shape
    kdim = w.shape[1]
    tm = _pick_tile(m, 640)
    if pair_major:
        x_spec = pl.BlockSpec((x.shape[0], tm, LANES), lambda i: (0, i, 0))
    else:
        x_spec = pl.BlockSpec((tm, kdim), lambda i: (i, 0))
    return pl.pallas_call(
        functools.partial(_proj_kernel, pair_major=pair_major),
        grid=(m // tm,),
        in_specs=[x_spec,
                  pl.BlockSpec((None, kdim, d), lambda i: (layer, 0, 0)),
                  pl.BlockSpec((tm, d), lambda i: (i, 0))],
        out_specs=pl.BlockSpec((tm, d), lambda i: (i, 0)),
        out_shape=jax.ShapeDtypeStruct((m, d), F32),
        compiler_params=_cparams("parallel"),
        name="proj_residual",
    )(x, w, h)


def _gla_in_kernel(h_ref, g_ref, w_ref, wz_ref, wa_ref, ba_ref, p_ref, gl_ref, xn_ref):
    @pl.when(pl.program_id(1) == 0)
    def _():
        xn = _rms(h_ref[...], g_ref[...], NORM_EPS).astype(BF16)
        xn_ref[...] = xn
        za = jnp.dot(xn, wz_ref[...], preferred_element_type=F32)
        u = _dot(za, wa_ref[...]) + ba_ref[...]
        gl_ref[...] = (jnp.minimum(u, 0.0) - jnp.log1p(jnp.exp(-jnp.abs(u)))) * (1.0 / GLA_TAU)

    p_ref[...] = jnp.dot(xn_ref[...], w_ref[...], preferred_element_type=F32).astype(p_ref.dtype)


def _gla_in(h, g, w, layer, n, wz, wa, ba):
    m, d = h.shape
    dk = wa.shape[1]
    tm = _pick_tile(m, 640)
    tn = _pick_tile(n, 2048)
    return pl.pallas_call(
        _gla_in_kernel,
        grid=(m // tm, n // tn),
        in_specs=[
            pl.BlockSpec((tm, d), lambda i, j: (i, 0)),
            pl.BlockSpec((1, d), lambda i, j: (0, 0)),
            pl.BlockSpec((None, d, tn), lambda i, j: (layer, 0, j)),
            pl.BlockSpec((d, LANES), lambda i, j: (0, 0)),
            pl.BlockSpec((LANES, dk), lambda i, j: (0, 0)),
            pl.BlockSpec((1, dk), lambda i, j: (0, 0)),
        ],
        out_specs=[pl.BlockSpec((tm, tn), lambda i, j: (i, j)),
                   pl.BlockSpec((tm, dk), lambda i, j: (i, 0))],
        out_shape=[jax.ShapeDtypeStruct((m, n), BF16), jax.ShapeDtypeStruct((m, dk), F32)],
        scratch_shapes=[pltpu.VMEM((tm, d), BF16)],
        compiler_params=_cparams("parallel", "arbitrary"),
        name="gla_in",
    )(h, g, w, wz, wa, ba)


def _gla_chunk_kernel(q_ref, k_ref, v_ref, gate_ref, gl_ref, gn_ref, z_ref, st_ref, *, scale):
    c = pl.program_id(1)
    nh, hv, hk = st_ref.shape

    @pl.when(c == 0)
    def _():
        st_ref[...] = jnp.zeros_like(st_ref)

    row = lax.broadcasted_iota(jnp.int32, (CHUNK, CHUNK), 0)
    col = lax.broadcasted_iota(jnp.int32, (CHUNK, CHUNK), 1)
    causal = row >= col
    tril = jnp.where(causal, 1.0, 0.0).astype(BF16)

    gl = gl_ref[...]
    live = jnp.logical_or(c > 0, lax.broadcasted_iota(jnp.int32, gl.shape, 0) >= N_DUMMY)
    gl = jnp.where(live, gl, 0.0)
    bc = _dot_sel_left(tril, gl)
    b_last = bc[CHUNK - 1:CHUNK, :]
    e_last = jnp.exp(b_last)
    k = k_ref[...]
    q_t = (q_ref[...] * scale * jnp.exp(bc)).astype(BF16)
    k_t = (k * jnp.exp(-bc)).astype(BF16)
    k_h = (k * jnp.exp(b_last - bc)).astype(BF16)

    heads = range(nh)
    ksl = lambda x, h: x[:, h * hk:(h + 1) * hk]
    vsl = lambda x, h: x[:, h * hv:(h + 1) * hv]
    v = [vsl(v_ref[...], h).astype(BF16) for h in heads]
    att = [jnp.where(causal, _dot_nt(ksl(q_t, h), ksl(k_t, h)), 0.0).astype(BF16) for h in heads]
    st = [st_ref[h] for h in heads]
    o = [_dot_nt(ksl(q_t, h), st[h]) + jnp.dot(att[h], v[h], preferred_element_type=F32) for h in heads]
    for h in heads:
        st_ref[h] = st[h] * ksl(e_last, h) + _dot_tn(v[h], ksl(k_h, h))
    for h in heads:
        on = o[h] * lax.rsqrt(jnp.mean(o[h] * o[h], axis=-1, keepdims=True) + GLA_HEAD_EPS)
        gate = vsl(gate_ref[...], h).astype(F32)
        z_ref[:, h * hv:(h + 1) * hv] = (on * vsl(gn_ref[...], h) * (gate * _sigmoid(gate))).astype(z_ref.dtype)


def _gla_chunk(p, glog, gn_w, batch, d):
    m = p.shape[0]
    nc = m // batch // CHUNK
    dk = d // 2
    hk = dk // GLA_HEADS
    hv = d // GLA_HEADS
    rows = lambda bi, c: bi * nc + c
    return pl.pallas_call(
        functools.partial(_gla_chunk_kernel, scale=hk ** -0.5),
        grid=(batch, nc),
        in_specs=[
            pl.BlockSpec((CHUNK, dk), lambda bi, c: (rows(bi, c), 0)),
            pl.BlockSpec((CHUNK, dk), lambda bi, c: (rows(bi, c), 1)),
            pl.BlockSpec((CHUNK, d), lambda bi, c: (rows(bi, c), 1)),
            pl.BlockSpec((CHUNK, d), lambda bi, c: (rows(bi, c), 2)),
            pl.BlockSpec((CHUNK, dk), lambda bi, c: (rows(bi, c), 0)),
            pl.BlockSpec((1, d), lambda bi, c: (0, 0)),
        ],
        out_specs=pl.BlockSpec((CHUNK, d), lambda bi, c: (rows(bi, c), 0)),
        out_shape=jax.ShapeDtypeStruct((m, d), BF16),
        scratch_shapes=[pltpu.VMEM((GLA_HEADS, hv, hk), F32)],
        compiler_params=_cparams("parallel", "arbitrary"),
        name="gla_chunk",
    )(p, p, p, p, glog, gn_w)


def _rwkv_layer(h, batch, j, v_first, norm_g, rw_mix, rw_w_rkv, rw_w0, rw_w1, rw_w2, rw_a0, rw_a1, rw_a2,
                rw_v0, rw_v1, rw_v2, rw_g1, rw_g2, rw_k_k, rw_k_a, rw_r_k, rw_ln_w, rw_ln_b, rw_w_o):
    m, d = h.shape
    npair = d // LANES
    row = lambda t: t.reshape(1, d)
    pairs = lambda t: t.reshape(npair, 1, LANES)
    h3 = h.reshape(batch, m // batch, d)
    g = row(norm_g)
    rkv = _rkv(h3, g, rw_mix[j], rw_w_rkv, j)
    branches = [(1, "tanh", "decay"), (4, "none", "sigmoid"), (5, "sigmoid", "none")]
    weights = [_pad_rank(rw_w1[j], rw_w2[j]) + (row(rw_w0[j]),),
               _pad_rank(rw_a1[j], rw_a2[j]) + (row(rw_a0[j]),),
               _pad_rank(rw_g1[j], rw_g2[j]) + (jnp.zeros((1, d), F32),)]
    if j > 0:
        branches.append((3, "none", "sigmoid"))
        weights.append(_pad_rank(rw_v1[j - 1], rw_v2[j - 1]) + (row(rw_v0[j - 1]),))
    outs = _lora(h3, g, rw_mix[j], tuple(branches), weights)
    lw, ag, gg = outs[:3]
    vres = (v_first, outs[3]) if j > 0 else None
    z = _rw_scan(rkv, lw, ag, gg, vres, pairs(rw_k_k[j]), pairs(rw_k_a[j]), pairs(rw_r_k[j]),
                 pairs(rw_ln_w[j]), pairs(rw_ln_b[j]), batch)
    h = _proj_residual(z, rw_w_o, j, h, pair_major=True)
    return h, rkv


def _gla_layer(h, batch, j, norm_g, gla_w_in, gla_w_in_bf, gla_w_a2, gla_b_a, gla_gn_w, gla_w_o):
    m, d = h.shape
    dk = d // 2
    n_main = 2 * dk + 2 * d
    w_in = gla_w_in[j]
    rank = w_in.shape[1] - n_main
    g = norm_g.reshape(1, d)
    wz = jnp.pad(w_in[:, n_main:], ((0, 0), (0, LANES - rank))).astype(BF16)
    wa = jnp.pad(gla_w_a2[j], ((0, LANES - rank), (0, 0)))
    p, glog = _gla_in(h, g, gla_w_in_bf, j, n_main, wz, wa, gla_b_a[j].reshape(1, dk))
    z = _gla_chunk(p, glog, gla_gn_w[j].reshape(1, d), batch, d)
    return _proj_residual(z, gla_w_o, j, h, pair_major=False)


def kernel(x, meta, norm_mix, norm_mlp, norm_f, mlp_w1, mlp_w2, rw_mix, rw_w_rkv, rw_w0, rw_w1, rw_w2, rw_a0, rw_a1, rw_a2, rw_v0, rw_v1, rw_v2, rw_g1, rw_g2, rw_k_k, rw_k_a, rw_r_k, rw_ln_w, rw_ln_b, rw_w_o, gla_w_in, gla_w_a2, gla_b_a, gla_gn_w, gla_w_o):
    batch, seq, d = x.shape
    depth = norm_mix.shape[0]
    assert seq % CHUNK == 0 and d % (2 * LANES) == 0 and meta.shape[0] == N_META
    tp = LEAD + seq
    lead = jnp.concatenate([jnp.zeros((N_DUMMY, d), x.dtype), meta.astype(x.dtype)], axis=0)
    h = jnp.concatenate([jnp.broadcast_to(lead[None], (batch, LEAD, d)), x], axis=1).reshape(batch * tp, d)
    gf = norm_f.reshape(1, d)
    w1_bf, w2_bf = mlp_w1.astype(BF16), mlp_w2.astype(BF16)
    rkv_bf, rwo_bf = rw_w_rkv.astype(BF16), rw_w_o.astype(BF16)
    gla_w_in_bf, gla_w_o_bf = gla_w_in.astype(BF16), gla_w_o.astype(BF16)
    v_first = None
    for i in range(depth):
        j = i // 2
        if i % 2 == 0:
            h, rkv = _rwkv_layer(h, batch, j, v_first, norm_mix[i], rw_mix, rkv_bf, rw_w0, rw_w1, rw_w2,
                                 rw_a0, rw_a1, rw_a2, rw_v0, rw_v1, rw_v2, rw_g1, rw_g2, rw_k_k, rw_k_a,
                                 rw_r_k.reshape(rw_r_k.shape[0], d), rw_ln_w, rw_ln_b, rwo_bf)
            if j == 0:
                v_first = rkv
        else:
            h = _gla_layer(h, batch, j, norm_mix[i], gla_w_in, gla_w_in_bf, gla_w_a2, gla_b_a, gla_gn_w, gla_w_o_bf)
        g_mlp = norm_mlp[i].reshape(1, d)
        if i == depth - 1:
            return _mlp_final(h.reshape(batch, tp, d), g_mlp, w1_bf, w2_bf, i, gf, 512, 1024)
        h = _mlp(h, g_mlp, w1_bf, w2_bf, i, gf, *MLP_TILES[i % len(MLP_TILES)])
```

```python
import functools

import jax
import jax.numpy as jnp
from jax import lax
from jax.experimental import pallas as pl
from jax.experimental.pallas import tpu as pltpu

F32 = jnp.float32
BF16 = jnp.bfloat16

N_META = 16
CHUNK = 64
LEAD = CHUNK
N_DUMMY = LEAD - N_META
NORM_EPS = 1e-6

RW_HEAD = 64
LANES = 128
RW_GN_EPS = 64e-5
RW_DECAY_SCALE = 0.6065306597126334
PAIR_GROUP = 16

GLA_HEADS = 4
GLA_TAU = 16.0
GLA_HEAD_EPS = 1e-5
GLA_SEQ_GROUP = 4

VMEM_LIMIT = 56 * 1024 * 1024
MLP_TILE = (832, 1024)
MLP_FINAL_TILE = (512, 1024)


def _cparams(*sem):
    return pltpu.CompilerParams(dimension_semantics=sem, vmem_limit_bytes=VMEM_LIMIT)


def _dot(a, b):
    return jnp.dot(a.astype(BF16), b.astype(BF16), preferred_element_type=F32)


def _dot_nt(a, b):
    return lax.dot_general(a.astype(BF16), b.astype(BF16), (((1,), (1,)), ((), ())),
                           preferred_element_type=F32)


def _dot_tn(a, b):
    return lax.dot_general(a.astype(BF16), b.astype(BF16), (((0,), (0,)), ((), ())),
                           preferred_element_type=F32)


def _split2(x):
    hi = x.astype(BF16)
    return hi, (x - hi.astype(F32)).astype(BF16)


def _dot_sel_left(sel, x):
    hi, lo = _split2(x)
    return jnp.dot(sel, hi, preferred_element_type=F32) + jnp.dot(sel, lo, preferred_element_type=F32)


def _rms(x, g, eps):
    return x * lax.rsqrt(jnp.mean(x * x, axis=-1, keepdims=True) + eps) * g


def _sigmoid(x):
    return 1.0 / (1.0 + jnp.exp(-x))


def _pick_tile(n, target):
    best = None
    for t in range(16, min(n, target) + 1, 16):
        if n % t == 0:
            best = t
    assert best is not None, (n, target)
    return best


def _mlp_kernel(h_ref, g_ref, w1_ref, w2_ref, gf_ref, o_ref, xn_ref, *, final):
    ff_axis = 2 if final else 1
    if final:
        o_ref = o_ref.at[0]
    j = pl.program_id(ff_axis)

    @pl.when(j == 0)
    def _():
        x = h_ref[...]
        xn_ref[...] = _rms(x, g_ref[...], NORM_EPS).astype(BF16)
        o_ref[...] = x

    hid = jnp.dot(xn_ref[...], w1_ref[...], preferred_element_type=F32)
    hid = jnp.maximum(hid, 0.0)
    hid = hid * hid
    o_ref[...] += jnp.dot(hid.astype(BF16), w2_ref[...], preferred_element_type=F32)

    if final:
        @pl.when(j == pl.num_programs(ff_axis) - 1)
        def _():
            o_ref[...] = _rms(o_ref[...], gf_ref[...], NORM_EPS)


def _mlp(h, g, w1, w2, layer, gf, tm_target, tf_target):
    m, d = h.shape
    ff = w1.shape[2]
    tm = _pick_tile(m, tm_target)
    tf = _pick_tile(ff, tf_target)
    return pl.pallas_call(
        functools.partial(_mlp_kernel, final=False),
        grid=(m // tm, ff // tf),
        in_specs=[
            pl.BlockSpec((tm, d), lambda i, j: (i, 0)),
            pl.BlockSpec((1, d), lambda i, j: (0, 0)),
            pl.BlockSpec((None, d, tf), lambda i, j: (layer, 0, j)),
            pl.BlockSpec((None, tf, d), lambda i, j: (layer, j, 0)),
            pl.BlockSpec((1, d), lambda i, j: (0, 0)),
        ],
        out_specs=pl.BlockSpec((tm, d), lambda i, j: (i, 0)),
        out_shape=jax.ShapeDtypeStruct((m, d), F32),
        scratch_shapes=[pltpu.VMEM((tm, d), BF16)],
        compiler_params=_cparams("parallel", "arbitrary"),
        name="mlp",
    )(h, g, w1, w2, gf)


def _mlp_final(h3, g, w1, w2, layer, gf, tm_target, tf_target):
    b, tp, d = h3.shape
    seq = tp - LEAD
    ff = w1.shape[2]
    tm = _pick_tile(seq, tm_target)
    tf = _pick_tile(ff, tf_target)
    return pl.pallas_call(
        functools.partial(_mlp_kernel, final=True),
        grid=(b, seq // tm, ff // tf),
        in_specs=[
            pl.BlockSpec((pl.Element(tm), pl.Element(d)),
                         lambda bi, i, j: (pl.multiple_of(bi * tp + LEAD + i * tm, 16), 0)),
            pl.BlockSpec((1, d), lambda bi, i, j: (0, 0)),
            pl.BlockSpec((None, d, tf), lambda bi, i, j: (layer, 0, j)),
            pl.BlockSpec((None, tf, d), lambda bi, i, j: (layer, j, 0)),
            pl.BlockSpec((1, d), lambda bi, i, j: (0, 0)),
        ],
        out_specs=pl.BlockSpec((1, tm, d), lambda bi, i, j: (bi, i, 0)),
        out_shape=jax.ShapeDtypeStruct((b, seq, d), F32),
        scratch_shapes=[pltpu.VMEM((tm, d), BF16)],
        compiler_params=_cparams("parallel", "parallel", "arbitrary"),
        name="mlp_final",
    )(h3.reshape(b * tp, d), g, w1, w2, gf)


def _shifted_norm(h_ref, hp_ref, g_ref):
    g = g_ref[...]
    hn = _rms(h_ref[0], g, NORM_EPS)
    pn = _rms(hp_ref[0], g, NORM_EPS)[7:8]
    pn = jnp.where(pl.program_id(1) == 0, 0.0, pn)
    row = lax.broadcasted_iota(jnp.int32, hn.shape, 0)
    prev = jnp.where(row == 0, pn, pltpu.roll(hn, 1, axis=0))
    return hn, prev - hn


def _to_pairs(o_ref, lead, val):
    for p in range(val.shape[1] // LANES):
        o_ref[lead + (p,)] = val[:, p * LANES:(p + 1) * LANES].astype(o_ref.dtype)


def _shift_specs(tm, d):
    nb = tm // 8
    return [
        pl.BlockSpec((1, tm, d), lambda bi, i: (bi, i, 0)),
        pl.BlockSpec((1, 8, d), lambda bi, i: (bi, jnp.maximum(i * nb - 1, 0), 0)),
        pl.BlockSpec((1, d), lambda bi, i: (0, 0)),
        pl.BlockSpec((6, d), lambda bi, i: (0, 0)),
    ]


RKV_MIX_ROWS = (0, 2, 3)


def _rkv_kernel(h_ref, hp_ref, g_ref, mix_ref, w_ref, o_ref):
    hn, xx = _shifted_norm(h_ref, hp_ref, g_ref)
    for s, mrow in enumerate(RKV_MIX_ROWS):
        x = (hn + xx * mix_ref[mrow:mrow + 1, :]).astype(BF16)
        _to_pairs(o_ref, (s,), jnp.dot(x, w_ref[s], preferred_element_type=F32))


def _rkv(h3, g, mix, w, layer):
    b, tp, d = h3.shape
    tm = _pick_tile(tp, 208)
    nt = tp // tm
    npair = d // LANES
    return pl.pallas_call(
        _rkv_kernel,
        grid=(b, nt),
        in_specs=_shift_specs(tm, d) + [
            pl.BlockSpec((None, 3, d, d), lambda bi, i: (layer, 0, 0, 0), pipeline_mode=pl.Buffered(1)),
        ],
        out_specs=pl.BlockSpec((3, npair, tm, LANES), lambda bi, i: (0, 0, bi * nt + i, 0)),
        out_shape=jax.ShapeDtypeStruct((3, npair, b * tp, LANES), F32),
        compiler_params=_cparams("parallel", "parallel"),
        name="rw_rkv",
    )(h3, h3, g, mix, w)


def _lora_kernel(*refs, branches):
    h_ref, hp_ref, g_ref, mix_ref = refs[:4]
    nb = len(branches)
    w_refs = refs[4:4 + 3 * nb]
    o_refs = refs[4 + 3 * nb:]
    hn, xx = _shifted_norm(h_ref, hp_ref, g_ref)
    for bi, (mrow, mid, out) in enumerate(branches):
        w1_ref, w2_ref, b_ref = w_refs[3 * bi:3 * bi + 3]
        x = (hn + xx * mix_ref[mrow:mrow + 1, :]).astype(BF16)
        z = jnp.dot(x, w1_ref[...], preferred_element_type=F32)
        if mid == "tanh":
            z = jnp.tanh(z)
        elif mid == "sigmoid":
            z = _sigmoid(z)
        y = jnp.dot(z.astype(BF16), w2_ref[...], preferred_element_type=F32)
        if out == "decay":
            y = -RW_DECAY_SCALE * _sigmoid(y + b_ref[...])
        elif out == "sigmoid":
            y = _sigmoid(y + b_ref[...])
        _to_pairs(o_refs[bi], (), y)


def _lora(h3, g, mix, branches, weights):
    b, tp, d = h3.shape
    tm = _pick_tile(tp, 416)
    nt = tp // tm
    npair = d // LANES
    w_specs, w_args = [], []
    for w1, w2, bias in weights:
        r = w1.shape[1]
        w_specs += [pl.BlockSpec((d, r), lambda bi, i: (0, 0)),
                    pl.BlockSpec((r, d), lambda bi, i: (0, 0)),
                    pl.BlockSpec((1, d), lambda bi, i: (0, 0))]
        w_args += [w1, w2, bias]
    out_spec = pl.BlockSpec((npair, tm, LANES), lambda bi, i: (0, bi * nt + i, 0))
    out_shape = jax.ShapeDtypeStruct((npair, b * tp, LANES), F32)
    return pl.pallas_call(
        functools.partial(_lora_kernel, branches=branches),
        grid=(b, nt),
        in_specs=_shift_specs(tm, d) + w_specs,
        out_specs=[out_spec] * len(branches),
        out_shape=[out_shape] * len(branches),
        compiler_params=_cparams("parallel", "parallel"),
        name="rw_lora",
    )(h3, h3, g, mix, *w_args)


def _pad_rank(w1, w2):
    r = w1.shape[1]
    rp = -(-r // LANES) * LANES
    return (jnp.pad(w1, ((0, 0), (0, rp - r))).astype(BF16),
            jnp.pad(w2, ((0, rp - r), (0, 0))).astype(BF16))


def _stack_heads(x, lane_lo):
    return jnp.concatenate([jnp.where(lane_lo, x, 0.0), jnp.where(lane_lo, 0.0, x)], axis=0)


def _rw_scan_kernel(*refs, has_vres):
    if has_vres:
        (rkv_ref, lw_ref, ag_ref, gg_ref, vf_ref, vg_ref,
         kk_ref, ka_ref, rk_ref, lnw_ref, lnb_ref, z_ref, st_ref) = refs
    else:
        (rkv_ref, lw_ref, ag_ref, gg_ref,
         kk_ref, ka_ref, rk_ref, lnw_ref, lnb_ref, z_ref, st_ref) = refs
    c = pl.program_id(1)
    npair = z_ref.shape[0]
    group = min(PAIR_GROUP, npair)
    assert npair % group == 0
    two_l = 2 * CHUNK

    @pl.when(c == 0)
    def _():
        st_ref[...] = jnp.zeros_like(st_ref)

    row = lax.broadcasted_iota(jnp.int32, (two_l, two_l), 0)
    col = lax.broadcasted_iota(jnp.int32, (two_l, two_l), 1)
    t_row = row % CHUNK
    t_col = col % CHUNK
    strict = t_row > t_col
    incl = t_row >= t_col
    same_head = (row // CHUNK) == (col // CHUNK)
    tril = jnp.where(lax.broadcasted_iota(jnp.int32, (CHUNK, CHUNK), 0)
                     >= lax.broadcasted_iota(jnp.int32, (CHUNK, CHUNK), 1), 1.0, 0.0).astype(BF16)
    diag = row == col
    lane_lo = lax.broadcasted_iota(jnp.int32, (CHUNK, LANES), 1) < RW_HEAD
    live = jnp.logical_or(c > 0, lax.broadcasted_iota(jnp.int32, (CHUNK, LANES), 0) >= N_DUMMY)

    def each(fn, *lists):
        return [fn(*xs) for xs in zip(*lists)]

    def head_sum(x):
        lo = jnp.sum(jnp.where(lane_lo, x, 0.0), axis=-1, keepdims=True)
        hi = jnp.sum(jnp.where(lane_lo, 0.0, x), axis=-1, keepdims=True)
        return jnp.where(lane_lo, lo, hi)

    def swap_halves(x):
        return pltpu.roll(x, RW_HEAD, axis=1)

    def group_body(gi, carry):
        ps = [gi * group + i for i in range(group)]
        r = [rkv_ref[0, p] for p in ps]
        k0 = [rkv_ref[1, p] for p in ps]
        v = [rkv_ref[2, p] for p in ps]
        lw = [lw_ref[p] for p in ps]
        ag = [ag_ref[p] for p in ps]
        if has_vres:
            v = [vi + (vf_ref[0, p] - vi) * vg_ref[p] for vi, p in zip(v, ps)]
        kk = [ki * kk_ref[p] for ki, p in zip(k0, ps)]
        k = [ki * (1.0 + (ai - 1.0) * ka_ref[p]) for ki, ai, p in zip(k0, ag, ps)]
        kk = each(lambda kki: kki / jnp.maximum(jnp.sqrt(head_sum(kki * kki)), 1e-12), kk)
        bonus = [head_sum(ri * ki * rk_ref[p]) * vi for ri, ki, vi, p in zip(r, k, v, ps)]

        cs = each(lambda lwi: _dot_sel_left(tril, lwi), lw)
        e_pos = each(jnp.exp, cs)
        e_neg = each(lambda ci: jnp.exp(-ci), cs)
        w_all = each(lambda ei: ei[CHUNK - 1:CHUNK, :], e_pos)
        a_s = each(lambda kki, ci, lwi: _stack_heads(-kki * jnp.exp(ci - lwi), lane_lo), kk, cs, lw)
        r_s = each(lambda ri, ei: _stack_heads(ri * ei, lane_lo), r, e_pos)
        b_t = each(lambda kki, ai, ei: kki * ai * ei, kk, ag, e_neg)
        k_t = each(lambda ki, ei: ki * ei, k, e_neg)
        v_s = each(lambda vi: _stack_heads(vi, lane_lo).astype(BF16), v)
        bh_t = each(lambda bti, wi: _stack_heads(bti * wi, lane_lo).T.astype(BF16), b_t, w_all)
        kh_t = each(lambda kti, wi: _stack_heads(kti * wi, lane_lo).T.astype(BF16), k_t, w_all)

        gram = each(lambda asi, rsi, bti, kti: _dot_nt(
            jnp.concatenate([asi, rsi], axis=0),
            jnp.concatenate([_stack_heads(bti, lane_lo), _stack_heads(kti, lane_lo)], axis=0)),
            a_s, r_s, b_t, k_t)
        pw = each(lambda gm: jnp.where(strict, gm[:two_l, :two_l], 0.0).astype(BF16), gram)
        a_ak = each(lambda gm: jnp.where(strict, gm[:two_l, two_l:], 0.0).astype(BF16), gram)
        a_rb = each(lambda gm: jnp.where(incl, gm[two_l:, :two_l], 0.0).astype(BF16), gram)
        a_rk = each(lambda gm: jnp.where(incl, gm[two_l:, two_l:], 0.0).astype(BF16), gram)

        xv = each(lambda aki, ari, khi, vsi: jnp.dot(jnp.concatenate([aki, ari, khi], axis=0), vsi,
                                                     preferred_element_type=F32), a_ak, a_rk, kh_t, v_s)

        z = each(lambda asi, xi: asi + swap_halves(xi[:two_l]), a_s, xv)
        n_sq = CHUNK.bit_length() - 1
        for s in range(n_sq):
            if s + 1 < n_sq:
                prod = each(lambda pi, zi: jnp.dot(pi, jnp.concatenate([zi.astype(BF16), pi], axis=1),
                                                   preferred_element_type=F32), pw, z)
                z = each(lambda zi, pr: zi + pr[:, :LANES], z, prod)
                pw = each(lambda pr: pr[:, LANES:].astype(BF16), prod)
            else:
                z = each(lambda zi, pi: zi + jnp.dot(pi, zi.astype(BF16), preferred_element_type=F32), z, pw)
        zb = each(lambda zi: zi.astype(BF16), z)

        rb = each(lambda ai, bhi, zi: jnp.dot(jnp.concatenate([ai, bhi], axis=0), zi,
                                              preferred_element_type=F32), a_rb, bh_t, zb)
        r_hat = each(lambda rsi, rbi: rsi + jnp.where(same_head, rbi[:two_l], 0.0), r_s, rb)
        y_hat = each(lambda rbi, xi: swap_halves(jnp.where(same_head, 0.0, rbi[:two_l])) + xi[two_l:2 * two_l],
                     rb, xv)
        m_mat = each(lambda wi, rbi: jnp.where(diag, wi, 0.0) + jnp.where(same_head, rbi[two_l:], 0.0),
                     w_all, rb)
        n_mat = each(lambda rbi, xi: swap_halves(jnp.where(same_head, 0.0, rbi[two_l:])) + xi[2 * two_l:],
                     rb, xv)

        hb = [st_ref[p].astype(BF16) for p in ps]
        ys = each(lambda rh, mm, hi: jnp.dot(jnp.concatenate([rh, mm], axis=0).astype(BF16), hi,
                                             preferred_element_type=F32), r_hat, m_mat, hb)
        for p, ysi, nm in zip(ps, ys, n_mat):
            st_ref[p] = ysi[two_l:] + nm
        y2 = each(lambda ysi, yh: ysi[:two_l] + yh, ys, y_hat)
        y = each(lambda yi: yi[:CHUNK] + yi[CHUNK:], y2)

        dy = each(lambda yi: yi - head_sum(yi) * (1.0 / RW_HEAD), y)
        var = each(lambda di: head_sum(di * di) * (1.0 / RW_HEAD), dy)
        for p, di, vi, bi in zip(ps, dy, var, bonus):
            yn = di * lax.rsqrt(vi + RW_GN_EPS) * lnw_ref[p] + lnb_ref[p]
            z_ref[p] = jnp.where(live, (yn + bi) * gg_ref[p], 0.0).astype(z_ref.dtype)
        return carry

    lax.fori_loop(0, npair // group, group_body, 0)


def _rw_scan(rkv, lw, ag, gg, vres, kk, ka, rk, lnw, lnb, batch):
    _, npair, m, _ = rkv.shape
    nc = m // batch // CHUNK
    row_map3 = lambda bi, c: (0, bi * nc + c, 0)
    row_map4 = lambda bi, c: (0, 0, bi * nc + c, 0)
    par = pl.BlockSpec((npair, 1, LANES), lambda bi, c: (0, 0, 0))
    tile = pl.BlockSpec((npair, CHUNK, LANES), row_map3)
    in_specs = [pl.BlockSpec((3, npair, CHUNK, LANES), row_map4), tile, tile, tile]
    args = [rkv, lw, ag, gg]
    if vres is not None:
        vfirst, vgate = vres
        in_specs += [pl.BlockSpec((1, npair, CHUNK, LANES), lambda bi, c: (2, 0, bi * nc + c, 0)), tile]
        args += [vfirst, vgate]
    in_specs += [par] * 5
    args += [kk, ka, rk, lnw, lnb]
    return pl.pallas_call(
        functools.partial(_rw_scan_kernel, has_vres=vres is not None),
        grid=(batch, nc),
        in_specs=in_specs,
        out_specs=tile,
        out_shape=jax.ShapeDtypeStruct((npair, m, LANES), BF16),
        scratch_shapes=[pltpu.VMEM((npair, 2 * CHUNK, LANES), F32)],
        compiler_params=_cparams("parallel", "arbitrary"),
        name="rw_scan",
    )(*args)


def _proj_kernel(x_ref, w_ref, h_ref, o_ref, *, pair_major):
    if pair_major:
        x = jnp.concatenate([x_ref[p] for p in range(x_ref.shape[0])], axis=1)
    else:
        x = x_ref[...]
    o_ref[...] = h_ref[...] + jnp.dot(x, w_ref[...], preferred_element_type=F32)


def _proj_residual(x, w, layer, h, pair_major):
    m, d = h.shape
    kdim = w.shape[1]
    tm = _pick_tile(m, 640)
    if pair_major:
        x_spec = pl.BlockSpec((x.shape[0], tm, LANES), lambda i: (0, i, 0))
    else:
        x_spec = pl.BlockSpec((tm, kdim), lambda i: (i, 0))
    return pl.pallas_call(
        functools.partial(_proj_kernel, pair_major=pair_major),
        grid=(m // tm,),
        in_specs=[x_spec,
                  pl.BlockSpec((None, kdim, d), lambda i: (layer, 0, 0)),
                  pl.BlockSpec((tm, d), lambda i: (i, 0))],
        out_specs=pl.BlockSpec((tm, d), lambda i: (i, 0)),
        out_shape=jax.ShapeDtypeStruct((m, d), F32),
        compiler_params=_cparams("parallel"),
        name="proj_residual",
    )(x, w, h)


def _gla_in_kernel(h_ref, g_ref, w_ref, wz_ref, wa_ref, ba_ref, p_ref, gl_ref, xn_ref):
    @pl.when(pl.program_id(1) == 0)
    def _():
        xn = _rms(h_ref[...], g_ref[...], NORM_EPS).astype(BF16)
        xn_ref[...] = xn
        za = jnp.dot(xn, wz_ref[...], preferred_element_type=F32)
        u = _dot(za, wa_ref[...]) + ba_ref[...]
        gl_ref[...] = (jnp.minimum(u, 0.0) - jnp.log1p(jnp.exp(-jnp.abs(u)))) * (1.0 / GLA_TAU)

    p_ref[...] = jnp.dot(xn_ref[...], w_ref[...], preferred_element_type=F32).astype(p_ref.dtype)


def _gla_in(h, g, w, layer, n, wz, wa, ba):
    m, d = h.shape
    dk = wa.shape[1]
    tm = _pick_tile(m, 640)
    tn = _pick_tile(n, 2048)
    return pl.pallas_call(
        _gla_in_kernel,
        grid=(m // tm, n // tn),
        in_specs=[
            pl.BlockSpec((tm, d), lambda i, j: (i, 0)),
            pl.BlockSpec((1, d), lambda i, j: (0, 0)),
            pl.BlockSpec((None, d, tn), lambda i, j: (layer, 0, j)),
            pl.BlockSpec((d, LANES), lambda i, j: (0, 0)),
            pl.BlockSpec((LANES, dk), lambda i, j: (0, 0)),
            pl.BlockSpec((1, dk), lambda i, j: (0, 0)),
        ],
        out_specs=[pl.BlockSpec((tm, tn), lambda i, j: (i, j)),
                   pl.BlockSpec((tm, dk), lambda i, j: (i, 0))],
        out_shape=[jax.ShapeDtypeStruct((m, n), BF16), jax.ShapeDtypeStruct((m, dk), F32)],
        scratch_shapes=[pltpu.VMEM((tm, d), BF16)],
        compiler_params=_cparams("parallel", "arbitrary"),
        name="gla_in",
    )(h, g, w, wz, wa, ba)


def _gla_chunk_kernel(q_ref, k_ref, v_ref, gate_ref, gl_ref, gn_ref, z_ref, st_ref, *, scale):
    c = pl.program_id(1)
    nb, nh, hv, hk = st_ref.shape

    @pl.when(c == 0)
    def _():
        st_ref[...] = jnp.zeros_like(st_ref)

    row = lax.broadcasted_iota(jnp.int32, (CHUNK, CHUNK), 0)
    col = lax.broadcasted_iota(jnp.int32, (CHUNK, CHUNK), 1)
    causal = row >= col
    tril = jnp.where(causal, 1.0, 0.0).astype(BF16)
    live = jnp.logical_or(c > 0, lax.broadcasted_iota(jnp.int32, gl_ref.shape[1:], 0) >= N_DUMMY)
    ksl = lambda x, h: x[:, h * hk:(h + 1) * hk]
    vsl = lambda x, h: x[:, h * hv:(h + 1) * hv]
    seqs = range(nb)
    cells = [(b, h) for b in seqs for h in range(nh)]

    gl = [jnp.where(live, gl_ref[b], 0.0) for b in seqs]
    bc = [_dot_sel_left(tril, gl[b]) for b in seqs]
    b_last = [bc[b][CHUNK - 1:CHUNK, :] for b in seqs]
    e_last = [jnp.exp(b_last[b]) for b in seqs]
    q_t = [(q_ref[b] * scale * jnp.exp(bc[b])).astype(BF16) for b in seqs]
    k_t = [(k_ref[b] * jnp.exp(-bc[b])).astype(BF16) for b in seqs]
    k_h = [(k_ref[b] * jnp.exp(b_last[b] - bc[b])).astype(BF16) for b in seqs]
    v = {(b, h): vsl(v_ref[b], h).astype(BF16) for b, h in cells}
    att = {(b, h): jnp.where(causal, _dot_nt(ksl(q_t[b], h), ksl(k_t[b], h)), 0.0).astype(BF16) for b, h in cells}
    st = {(b, h): st_ref[b, h] for b, h in cells}
    o = {(b, h): _dot_nt(ksl(q_t[b], h), st[b, h]) + jnp.dot(att[b, h], v[b, h], preferred_element_type=F32)
         for b, h in cells}
    for b, h in cells:
        st_ref[b, h] = st[b, h] * ksl(e_last[b], h) + _dot_tn(v[b, h], ksl(k_h[b], h))
    for b, h in cells:
        on = o[b, h] * lax.rsqrt(jnp.mean(o[b, h] * o[b, h], axis=-1, keepdims=True) + GLA_HEAD_EPS)
        gate = vsl(gate_ref[b], h).astype(F32)
        z_ref[b, :, h * hv:(h + 1) * hv] = (on * vsl(gn_ref[...], h) * (gate * _sigmoid(gate))).astype(z_ref.dtype)


def _gla_chunk(p, glog, gn_w, batch, d):
    tp = p.shape[0] // batch
    nc = tp // CHUNK
    dk = d // 2
    hk = dk // GLA_HEADS
    hv = d // GLA_HEADS
    p3 = p.reshape(batch, tp, p.shape[1])
    nb = GLA_SEQ_GROUP if batch % GLA_SEQ_GROUP == 0 else 1
    z = pl.pallas_call(
        functools.partial(_gla_chunk_kernel, scale=hk ** -0.5),
        grid=(batch // nb, nc),
        in_specs=[
            pl.BlockSpec((nb, CHUNK, dk), lambda g, c: (g, c, 0)),
            pl.BlockSpec((nb, CHUNK, dk), lambda g, c: (g, c, 1)),
            pl.BlockSpec((nb, CHUNK, d), lambda g, c: (g, c, 1)),
            pl.BlockSpec((nb, CHUNK, d), lambda g, c: (g, c, 2)),
            pl.BlockSpec((nb, CHUNK, dk), lambda g, c: (g, c, 0)),
            pl.BlockSpec((1, d), lambda g, c: (0, 0)),
        ],
        out_specs=pl.BlockSpec((nb, CHUNK, d), lambda g, c: (g, c, 0)),
        out_shape=jax.ShapeDtypeStruct((batch, tp, d), BF16),
        scratch_shapes=[pltpu.VMEM((nb, GLA_HEADS, hv, hk), F32)],
        compiler_params=_cparams("parallel", "arbitrary"),
        name="gla_chunk",
    )(p3, p3, p3, p3, glog.reshape(batch, tp, dk), gn_w)
    return z.reshape(batch * tp, d)


def _rwkv_layer(h, batch, j, v_first, norm_g, rw_mix, rw_w_rkv, rw_w0, rw_w1, rw_w2, rw_a0, rw_a1, rw_a2,
                rw_v0, rw_v1, rw_v2, rw_g1, rw_g2, rw_k_k, rw_k_a, rw_r_k, rw_ln_w, rw_ln_b, rw_w_o):
    m, d = h.shape
    npair = d // LANES
    row = lambda t: t.reshape(1, d)
    pairs = lambda t: t.reshape(npair, 1, LANES)
    h3 = h.reshape(batch, m // batch, d)
    g = row(norm_g)
    rkv = _rkv(h3, g, rw_mix[j], rw_w_rkv, j)
    branches = [(1, "tanh", "decay"), (4, "none", "sigmoid"), (5, "sigmoid", "none")]
    weights = [_pad_rank(rw_w1[j], rw_w2[j]) + (row(rw_w0[j]),),
               _pad_rank(rw_a1[j], rw_a2[j]) + (row(rw_a0[j]),),
               _pad_rank(rw_g1[j], rw_g2[j]) + (jnp.zeros((1, d), F32),)]
    if j > 0:
        branches.append((3, "none", "sigmoid"))
        weights.append(_pad_rank(rw_v1[j - 1], rw_v2[j - 1]) + (row(rw_v0[j - 1]),))
    outs = _lora(h3, g, rw_mix[j], tuple(branches), weights)
    lw, ag, gg = outs[:3]
    vres = (v_first, outs[3]) if j > 0 else None
    z = _rw_scan(rkv, lw, ag, gg, vres, pairs(rw_k_k[j]), pairs(rw_k_a[j]), pairs(rw_r_k[j]),
                 pairs(rw_ln_w[j]), pairs(rw_ln_b[j]), batch)
    h = _proj_residual(z, rw_w_o, j, h, pair_major=True)
    return h, rkv


def _gla_layer(h, batch, j, norm_g, gla_w_in, gla_w_in_bf, gla_w_a2, gla_b_a, gla_gn_w, gla_w_o):
    m, d = h.shape
    dk = d // 2
    n_main = 2 * dk + 2 * d
    w_in = gla_w_in[j]
    rank = w_in.shape[1] - n_main
    g = norm_g.reshape(1, d)
    wz = jnp.pad(w_in[:, n_main:], ((0, 0), (0, LANES - rank))).astype(BF16)
    wa = jnp.pad(gla_w_a2[j], ((0, LANES - rank), (0, 0)))
    p, glog = _gla_in(h, g, gla_w_in_bf, j, n_main, wz, wa, gla_b_a[j].reshape(1, dk))
    z = _gla_chunk(p, glog, gla_gn_w[j].reshape(1, d), batch, d)
    return _proj_residual(z, gla_w_o, j, h, pair_major=False)


def kernel(x, meta, norm_mix, norm_mlp, norm_f, mlp_w1, mlp_w2, rw_mix, rw_w_rkv, rw_w0, rw_w1, rw_w2, rw_a0, rw_a1, rw_a2, rw_v0, rw_v1, rw_v2, rw_g1, rw_g2, rw_k_k, rw_k_a, rw_r_k, rw_ln_w, rw_ln_b, rw_w_o, gla_w_in, gla_w_a2, gla_b_a, gla_gn_w, gla_w_o):
    batch, seq, d = x.shape
    depth = norm_mix.shape[0]
    assert seq % CHUNK == 0 and d % (2 * LANES) == 0 and meta.shape[0] == N_META
    tp = LEAD + seq
    lead = jnp.concatenate([jnp.zeros((N_DUMMY, d), x.dtype), meta.astype(x.dtype)], axis=0)
    h = jnp.concatenate([jnp.broadcast_to(lead[None], (batch, LEAD, d)), x], axis=1).reshape(batch * tp, d)
    gf = norm_f.reshape(1, d)
    w1_bf, w2_bf = mlp_w1.astype(BF16), mlp_w2.astype(BF16)
    rkv_bf, rwo_bf = rw_w_rkv.astype(BF16), rw_w_o.astype(BF16)
    gla_w_in_bf, gla_w_o_bf = gla_w_in.astype(BF16), gla_w_o.astype(BF16)
    v_first = None
    for i in range(depth):
        j = i // 2
        if i % 2 == 0:
            h, rkv = _rwkv_layer(h, batch, j, v_first, norm_mix[i], rw_mix, rkv_bf, rw_w0, rw_w1, rw_w2,
                                 rw_a0, rw_a1, rw_a2, rw_v0, rw_v1, rw_v2, rw_g1, rw_g2, rw_k_k, rw_k_a,
                                 rw_r_k.reshape(rw_r_k.shape[0], d), rw_ln_w, rw_ln_b, rwo_bf)
            if j == 0:
                v_first = rkv
        else:
            h = _gla_layer(h, batch, j, norm_mix[i], gla_w_in, gla_w_in_bf, gla_w_a2, gla_b_a, gla_gn_w, gla_w_o_bf)
        g_mlp = norm_mlp[i].reshape(1, d)
        if i == depth - 1:
            return _mlp_final(h.reshape(batch, tp, d), g_mlp, w1_bf, w2_bf, i, gf, *MLP_FINAL_TILE)
        h = _mlp(h, g_mlp, w1_bf, w2_bf, i, gf, *MLP_TILE)
```

```python
import functools

import jax
import jax.numpy as jnp
from jax import lax
from jax.experimental import pallas as pl
from jax.experimental.pallas import tpu as pltpu

F32 = jnp.float32
BF16 = jnp.bfloat16

N_META = 16
CHUNK = 64
LEAD = CHUNK
N_DUMMY = LEAD - N_META
NORM_EPS = 1e-6

RW_HEAD = 64
LANES = 128
RW_GN_EPS = 64e-5
RW_DECAY_SCALE = 0.6065306597126334
PAIR_GROUP = 16

GLA_HEADS = 4
GLA_TAU = 16.0
GLA_HEAD_EPS = 1e-5
GLA_SEQ_GROUP = 4

VMEM_LIMIT = 56 * 1024 * 1024
MLP_TILE = (832, 1024)
MLP_FINAL_TILE = (512, 1024)


def _cparams(*sem):
    return pltpu.CompilerParams(dimension_semantics=sem, vmem_limit_bytes=VMEM_LIMIT)


def _dot(a, b):
    return jnp.dot(a.astype(BF16), b.astype(BF16), preferred_element_type=F32)


def _dot_nt(a, b):
    return lax.dot_general(a.astype(BF16), b.astype(BF16), (((1,), (1,)), ((), ())),
                           preferred_element_type=F32)


def _dot_tn(a, b):
    return lax.dot_general(a.astype(BF16), b.astype(BF16), (((0,), (0,)), ((), ())),
                           preferred_element_type=F32)


def _split2(x):
    hi = x.astype(BF16)
    return hi, (x - hi.astype(F32)).astype(BF16)


def _dot_sel_left(sel, x):
    hi, lo = _split2(x)
    return jnp.dot(sel, hi, preferred_element_type=F32) + jnp.dot(sel, lo, preferred_element_type=F32)


def _rms(x, g, eps):
    return x * lax.rsqrt(jnp.mean(x * x, axis=-1, keepdims=True) + eps) * g


def _sigmoid(x):
    return 1.0 / (1.0 + jnp.exp(-x))


def _pick_tile(n, target):
    best = None
    for t in range(16, min(n, target) + 1, 16):
        if n % t == 0:
            best = t
    assert best is not None, (n, target)
    return best


def _mlp_kernel(h_ref, g_ref, w1_ref, w2_ref, gf_ref, o_ref, xn_ref, *, final):
    ff_axis = 2 if final else 1
    if final:
        o_ref = o_ref.at[0]
    j = pl.program_id(ff_axis)

    @pl.when(j == 0)
    def _():
        x = h_ref[...]
        xn_ref[...] = _rms(x, g_ref[...], NORM_EPS).astype(BF16)
        o_ref[...] = x

    hid = jnp.dot(xn_ref[...], w1_ref[...], preferred_element_type=F32)
    hid = jnp.maximum(hid, 0.0)
    hid = hid * hid
    o_ref[...] += jnp.dot(hid.astype(BF16), w2_ref[...], preferred_element_type=F32)

    if final:
        @pl.when(j == pl.num_programs(ff_axis) - 1)
        def _():
            o_ref[...] = _rms(o_ref[...], gf_ref[...], NORM_EPS)


def _mlp(h, g, w1, w2, layer, gf, tm_target, tf_target):
    m, d = h.shape
    ff = w1.shape[2]
    tm = _pick_tile(m, tm_target)
    tf = _pick_tile(ff, tf_target)
    return pl.pallas_call(
        functools.partial(_mlp_kernel, final=False),
        grid=(m // tm, ff // tf),
        in_specs=[
            pl.BlockSpec((tm, d), lambda i, j: (i, 0)),
            pl.BlockSpec((1, d), lambda i, j: (0, 0)),
            pl.BlockSpec((None, d, tf), lambda i, j: (layer, 0, j)),
            pl.BlockSpec((None, tf, d), lambda i, j: (layer, j, 0)),
            pl.BlockSpec((1, d), lambda i, j: (0, 0)),
        ],
        out_specs=pl.BlockSpec((tm, d), lambda i, j: (i, 0)),
        out_shape=jax.ShapeDtypeStruct((m, d), F32),
        scratch_shapes=[pltpu.VMEM((tm, d), BF16)],
        compiler_params=_cparams("parallel", "arbitrary"),
        name="mlp",
    )(h, g, w1, w2, gf)


def _mlp_final(h3, g, w1, w2, layer, gf, tm_target, tf_target):
    b, tp, d = h3.shape
    seq = tp - LEAD
    ff = w1.shape[2]
    tm = _pick_tile(seq, tm_target)
    tf = _pick_tile(ff, tf_target)
    return pl.pallas_call(
        functools.partial(_mlp_kernel, final=True),
        grid=(b, seq // tm, ff // tf),
        in_specs=[
            pl.BlockSpec((pl.Element(tm), pl.Element(d)),
                         lambda bi, i, j: (pl.multiple_of(bi * tp + LEAD + i * tm, 16), 0)),
            pl.BlockSpec((1, d), lambda bi, i, j: (0, 0)),
            pl.BlockSpec((None, d, tf), lambda bi, i, j: (layer, 0, j)),
            pl.BlockSpec((None, tf, d), lambda bi, i, j: (layer, j, 0)),
            pl.BlockSpec((1, d), lambda bi, i, j: (0, 0)),
        ],
        out_specs=pl.BlockSpec((1, tm, d), lambda bi, i, j: (bi, i, 0)),
        out_shape=jax.ShapeDtypeStruct((b, seq, d), F32),
        scratch_shapes=[pltpu.VMEM((tm, d), BF16)],
        compiler_params=_cparams("parallel", "parallel", "arbitrary"),
        name="mlp_final",
    )(h3.reshape(b * tp, d), g, w1, w2, gf)


def _shifted_norm(h_ref, hp_ref, g_ref):
    g = g_ref[...]
    hn = _rms(h_ref[0], g, NORM_EPS)
    pn = _rms(hp_ref[0], g, NORM_EPS)[7:8]
    pn = jnp.where(pl.program_id(1) == 0, 0.0, pn)
    row = lax.broadcasted_iota(jnp.int32, hn.shape, 0)
    prev = jnp.where(row == 0, pn, pltpu.roll(hn, 1, axis=0))
    return hn, prev - hn


def _to_pairs(o_ref, lead, val):
    for p in range(val.shape[1] // LANES):
        o_ref[lead + (p,)] = val[:, p * LANES:(p + 1) * LANES].astype(o_ref.dtype)


def _shift_specs(tm, d):
    nb = tm // 8
    return [
        pl.BlockSpec((1, tm, d), lambda bi, i: (bi, i, 0)),
        pl.BlockSpec((1, 8, d), lambda bi, i: (bi, jnp.maximum(i * nb - 1, 0), 0)),
        pl.BlockSpec((1, d), lambda bi, i: (0, 0)),
        pl.BlockSpec((6, d), lambda bi, i: (0, 0)),
    ]


RKV_MIX_ROWS = (0, 2, 3)


def _rkv_kernel(h_ref, hp_ref, g_ref, mix_ref, w_ref, o_ref):
    hn, xx = _shifted_norm(h_ref, hp_ref, g_ref)
    for s, mrow in enumerate(RKV_MIX_ROWS):
        x = (hn + xx * mix_ref[mrow:mrow + 1, :]).astype(BF16)
        _to_pairs(o_ref, (s,), jnp.dot(x, w_ref[s], preferred_element_type=F32))


def _rkv(h3, g, mix, w, layer):
    b, tp, d = h3.shape
    tm = _pick_tile(tp, 208)
    nt = tp // tm
    npair = d // LANES
    return pl.pallas_call(
        _rkv_kernel,
        grid=(b, nt),
        in_specs=_shift_specs(tm, d) + [
            pl.BlockSpec((None, 3, d, d), lambda bi, i: (layer, 0, 0, 0), pipeline_mode=pl.Buffered(1)),
        ],
        out_specs=pl.BlockSpec((3, npair, tm, LANES), lambda bi, i: (0, 0, bi * nt + i, 0)),
        out_shape=jax.ShapeDtypeStruct((3, npair, b * tp, LANES), F32),
        compiler_params=_cparams("parallel", "parallel"),
        name="rw_rkv",
    )(h3, h3, g, mix, w)


def _lora_kernel(*refs, branches):
    h_ref, hp_ref, g_ref, mix_ref = refs[:4]
    nb = len(branches)
    w_refs = refs[4:4 + 3 * nb]
    o_refs = refs[4 + 3 * nb:]
    hn, xx = _shifted_norm(h_ref, hp_ref, g_ref)
    for bi, (mrow, mid, out) in enumerate(branches):
        w1_ref, w2_ref, b_ref = w_refs[3 * bi:3 * bi + 3]
        x = (hn + xx * mix_ref[mrow:mrow + 1, :]).astype(BF16)
        z = jnp.dot(x, w1_ref[...], preferred_element_type=F32)
        if mid == "tanh":
            z = jnp.tanh(z)
        elif mid == "sigmoid":
            z = _sigmoid(z)
        y = jnp.dot(z.astype(BF16), w2_ref[...], preferred_element_type=F32)
        if out == "decay":
            y = -RW_DECAY_SCALE * _sigmoid(y + b_ref[...])
        elif out == "sigmoid":
            y = _sigmoid(y + b_ref[...])
        _to_pairs(o_refs[bi], (), y)


def _lora(h3, g, mix, branches, weights):
    b, tp, d = h3.shape
    tm = _pick_tile(tp, 416)
    nt = tp // tm
    npair = d // LANES
    w_specs, w_args = [], []
    for w1, w2, bias in weights:
        r = w1.shape[1]
        w_specs += [pl.BlockSpec((d, r), lambda bi, i: (0, 0)),
                    pl.BlockSpec((r, d), lambda bi, i: (0, 0)),
                    pl.BlockSpec((1, d), lambda bi, i: (0, 0))]
        w_args += [w1, w2, bias]
    out_spec = pl.BlockSpec((npair, tm, LANES), lambda bi, i: (0, bi * nt + i, 0))
    out_shape = jax.ShapeDtypeStruct((npair, b * tp, LANES), F32)
    return pl.pallas_call(
        functools.partial(_lora_kernel, branches=branches),
        grid=(b, nt),
        in_specs=_shift_specs(tm, d) + w_specs,
        out_specs=[out_spec] * len(branches),
        out_shape=[out_shape] * len(branches),
        compiler_params=_cparams("parallel", "parallel"),
        name="rw_lora",
    )(h3, h3, g, mix, *w_args)


def _pad_rank(w1, w2):
    r = w1.shape[1]
    rp = -(-r // LANES) * LANES
    return (jnp.pad(w1, ((0, 0), (0, rp - r))).astype(BF16),
            jnp.pad(w2, ((0, rp - r), (0, 0))).astype(BF16))


def _stack_heads(x, lane_lo):
    return jnp.concatenate([jnp.where(lane_lo, x, 0.0), jnp.where(lane_lo, 0.0, x)], axis=0)


def _rw_scan_kernel(*refs, has_vres):
    if has_vres:
        (rkv_ref, lw_ref, ag_ref, gg_ref, vf_ref, vg_ref,
         kk_ref, ka_ref, rk_ref, lnw_ref, lnb_ref, z_ref, st_ref) = refs
    else:
        (rkv_ref, lw_ref, ag_ref, gg_ref,
         kk_ref, ka_ref, rk_ref, lnw_ref, lnb_ref, z_ref, st_ref) = refs
    c = pl.program_id(1)
    npair = z_ref.shape[0]
    group = min(PAIR_GROUP, npair)
    assert npair % group == 0
    two_l = 2 * CHUNK

    @pl.when(c == 0)
    def _():
        st_ref[...] = jnp.zeros_like(st_ref)

    row = lax.broadcasted_iota(jnp.int32, (two_l, two_l), 0)
    col = lax.broadcasted_iota(jnp.int32, (two_l, two_l), 1)
    t_row = row % CHUNK
    t_col = col % CHUNK
    t_half = lax.broadcasted_iota(jnp.int32, (CHUNK, two_l), 0)
    s_half = lax.broadcasted_iota(jnp.int32, (CHUNK, two_l), 1) % CHUNK
    strict_t = t_half > s_half
    incl_t = t_half >= s_half
    same_head = (row // CHUNK) == (col // CHUNK)
    tril = jnp.where(lax.broadcasted_iota(jnp.int32, (CHUNK, CHUNK), 0)
                     >= lax.broadcasted_iota(jnp.int32, (CHUNK, CHUNK), 1), 1.0, 0.0).astype(BF16)
    diag = row == col
    lane_lo = lax.broadcasted_iota(jnp.int32, (CHUNK, LANES), 1) < RW_HEAD
    live = jnp.logical_or(c > 0, lax.broadcasted_iota(jnp.int32, (CHUNK, LANES), 0) >= N_DUMMY)

    def each(fn, *lists):
        return [fn(*xs) for xs in zip(*lists)]

    ones_bd = jnp.where(same_head, 1.0, 0.0).astype(BF16)

    def head_sum_lanes(x):
        lo = jnp.sum(jnp.where(lane_lo, x, 0.0), axis=-1, keepdims=True)
        hi = jnp.sum(jnp.where(lane_lo, 0.0, x), axis=-1, keepdims=True)
        return jnp.where(lane_lo, lo, hi)

    def head_sum_matmul(x):
        hi, lo = _split2(x)
        both = jnp.dot(jnp.concatenate([hi, lo], axis=0), ones_bd, preferred_element_type=F32)
        return both[:x.shape[0]] + both[x.shape[0]:]

    def swap_halves(x):
        return pltpu.roll(x, RW_HEAD, axis=1)

    def group_body(gi, carry):
        ps = [gi * group + i for i in range(group)]
        r = [rkv_ref[0, p] for p in ps]
        k0 = [rkv_ref[1, p] for p in ps]
        v = [rkv_ref[2, p] for p in ps]
        lw = [lw_ref[p] for p in ps]
        ag = [ag_ref[p] for p in ps]
        if has_vres:
            v = [vi + (vf_ref[0, p] - vi) * vg_ref[p] for vi, p in zip(v, ps)]
        kk = [ki * kk_ref[p] for ki, p in zip(k0, ps)]
        k = [ki * (1.0 + (ai - 1.0) * ka_ref[p]) for ki, ai, p in zip(k0, ag, ps)]
        kk = each(lambda kki: kki / jnp.maximum(jnp.sqrt(head_sum_lanes(kki * kki)), 1e-12), kk)
        bonus = [head_sum_lanes(ri * ki * rk_ref[p]) * vi for ri, ki, vi, p in zip(r, k, v, ps)]

        cs = each(lambda lwi: _dot_sel_left(tril, lwi), lw)
        e_pos = each(jnp.exp, cs)
        e_neg = each(lambda ci: jnp.exp(-ci), cs)
        w_all = each(lambda ei: ei[CHUNK - 1:CHUNK, :], e_pos)
        a_t = each(lambda kki, ci, lwi: -kki * jnp.exp(ci - lwi), kk, cs, lw)
        r_t = each(lambda ri, ei: ri * ei, r, e_pos)
        a_s = each(lambda ati: _stack_heads(ati, lane_lo), a_t)
        r_s = each(lambda rti: _stack_heads(rti, lane_lo), r_t)
        b_t = each(lambda kki, ai, ei: kki * ai * ei, kk, ag, e_neg)
        k_t = each(lambda ki, ei: ki * ei, k, e_neg)
        v_s = each(lambda vi: _stack_heads(vi, lane_lo).astype(BF16), v)
        bh_t = each(lambda bti, wi: _stack_heads(bti * wi, lane_lo).T.astype(BF16), b_t, w_all)
        kh_t = each(lambda kti, wi: _stack_heads(kti * wi, lane_lo).T.astype(BF16), k_t, w_all)

        gram = each(lambda ati, rti, bti, kti: _dot_nt(
            jnp.concatenate([ati, rti], axis=0),
            jnp.concatenate([_stack_heads(bti, lane_lo), _stack_heads(kti, lane_lo)], axis=0)),
            a_t, r_t, b_t, k_t)
        block = lambda gm, keep, rows, cols: _stack_heads(
            jnp.where(keep, gm[rows * CHUNK:(rows + 1) * CHUNK, cols * two_l:(cols + 1) * two_l], 0.0),
            lane_lo).astype(BF16)
        pw = each(lambda gm: block(gm, strict_t, 0, 0), gram)
        a_ak = each(lambda gm: block(gm, strict_t, 0, 1), gram)
        a_rb = each(lambda gm: block(gm, incl_t, 1, 0), gram)
        a_rk = each(lambda gm: block(gm, incl_t, 1, 1), gram)

        xv = each(lambda aki, ari, khi, vsi: jnp.dot(jnp.concatenate([aki, ari, khi], axis=0), vsi,
                                                     preferred_element_type=F32), a_ak, a_rk, kh_t, v_s)

        z = each(lambda asi, xi: asi + swap_halves(xi[:two_l]), a_s, xv)
        n_sq = CHUNK.bit_length() - 1
        for s in range(n_sq):
            if s + 1 < n_sq:
                prod = each(lambda pi, zi: jnp.dot(pi, jnp.concatenate([zi.astype(BF16), pi], axis=1),
                                                   preferred_element_type=F32), pw, z)
                z = each(lambda zi, pr: zi + pr[:, :LANES], z, prod)
                pw = each(lambda pr: pr[:, LANES:].astype(BF16), prod)
            else:
                z = each(lambda zi, pi: zi + jnp.dot(pi, zi.astype(BF16), preferred_element_type=F32), z, pw)
        zb = each(lambda zi: zi.astype(BF16), z)

        rb = each(lambda ai, bhi, zi: jnp.dot(jnp.concatenate([ai, bhi], axis=0), zi,
                                              preferred_element_type=F32), a_rb, bh_t, zb)
        r_hat = each(lambda rsi, rbi: rsi + jnp.where(same_head, rbi[:two_l], 0.0), r_s, rb)
        y_hat = each(lambda rbi, xi: swap_halves(jnp.where(same_head, 0.0, rbi[:two_l])) + xi[two_l:2 * two_l],
                     rb, xv)
        m_mat = each(lambda wi, rbi: jnp.where(diag, wi, 0.0) + jnp.where(same_head, rbi[two_l:], 0.0),
                     w_all, rb)
        n_mat = each(lambda rbi, xi: swap_halves(jnp.where(same_head, 0.0, rbi[two_l:])) + xi[2 * two_l:],
                     rb, xv)

        hb = [st_ref[p].astype(BF16) for p in ps]
        ys = each(lambda rh, mm, hi: jnp.dot(jnp.concatenate([rh, mm], axis=0).astype(BF16), hi,
                                             preferred_element_type=F32), r_hat, m_mat, hb)
        for p, ysi, nm in zip(ps, ys, n_mat):
            st_ref[p] = ysi[two_l:] + nm
        y2 = each(lambda ysi, yh: ysi[:two_l] + yh, ys, y_hat)
        y = each(lambda yi: yi[:CHUNK] + yi[CHUNK:], y2)

        dy = each(lambda yi: yi - head_sum_matmul(yi) * (1.0 / RW_HEAD), y)
        var = each(lambda di: head_sum_matmul(di * di) * (1.0 / RW_HEAD), dy)
        for p, di, vi, bi in zip(ps, dy, var, bonus):
            yn = di * lax.rsqrt(vi + RW_GN_EPS) * lnw_ref[p] + lnb_ref[p]
            z_ref[p] = jnp.where(live, (yn + bi) * gg_ref[p], 0.0).astype(z_ref.dtype)
        return carry

    lax.fori_loop(0, npair // group, group_body, 0)


def _rw_scan(rkv, lw, ag, gg, vres, kk, ka, rk, lnw, lnb, batch):
    _, npair, m, _ = rkv.shape
    nc = m // batch // CHUNK
    row_map3 = lambda bi, c: (0, bi * nc + c, 0)
    row_map4 = lambda bi, c: (0, 0, bi * nc + c, 0)
    par = pl.BlockSpec((npair, 1, LANES), lambda bi, c: (0, 0, 0))
    tile = pl.BlockSpec((npair, CHUNK, LANES), row_map3)
    in_specs = [pl.BlockSpec((3, npair, CHUNK, LANES), row_map4), tile, tile, tile]
    args = [rkv, lw, ag, gg]
    if vres is not None:
        vfirst, vgate = vres
        in_specs += [pl.BlockSpec((1, npair, CHUNK, LANES), lambda bi, c: (2, 0, bi * nc + c, 0)), tile]
        args += [vfirst, vgate]
    in_specs += [par] * 5
    args += [kk, ka, rk, lnw, lnb]
    return pl.pallas_call(
        functools.partial(_rw_scan_kernel, has_vres=vres is not None),
        grid=(batch, nc),
        in_specs=in_specs,
        out_specs=tile,
        out_shape=jax.ShapeDtypeStruct((npair, m, LANES), BF16),
        scratch_shapes=[pltpu.VMEM((npair, 2 * CHUNK, LANES), F32)],
        compiler_params=_cparams("parallel", "arbitrary"),
        name="rw_scan",
    )(*args)


def _proj_kernel(x_ref, w_ref, h_ref, o_ref, *, pair_major):
    if pair_major:
        x = jnp.concatenate([x_ref[p] for p in range(x_ref.shape[0])], axis=1)
    else:
        x = x_ref[...]
    o_ref[...] = h_ref[...] + jnp.dot(x, w_ref[...], preferred_element_type=F32)


def _proj_residual(x, w, layer, h, pair_major):
    m, d = h.shape
    kdim = w.shape[1]
    tm = _pick_tile(m, 640)
    if pair_major:
        x_spec = pl.BlockSpec((x.shape[0], tm, LANES), lambda i: (0, i, 0))
    else:
        x_spec = pl.BlockSpec((tm, kdim), lambda i: (i, 0))
    return pl.pallas_call(
        functools.partial(_proj_kernel, pair_major=pair_major),
        grid=(m // tm,),
        in_specs=[x_spec,
                  pl.BlockSpec((None, kdim, d), lambda i: (layer, 0, 0)),
                  pl.BlockSpec((tm, d), lambda i: (i, 0))],
        out_specs=pl.BlockSpec((tm, d), lambda i: (i, 0)),
        out_shape=jax.ShapeDtypeStruct((m, d), F32),
        compiler_params=_cparams("parallel"),
        name="proj_residual",
    )(x, w, h)


def _gla_in_kernel(h_ref, g_ref, w_ref, wz_ref, wa_ref, ba_ref, p_ref, gl_ref, xn_ref):
    @pl.when(pl.program_id(1) == 0)
    def _():
        xn = _rms(h_ref[...], g_ref[...], NORM_EPS).astype(BF16)
        xn_ref[...] = xn
        za = jnp.dot(xn, wz_ref[...], preferred_element_type=F32)
        u = _dot(za, wa_ref[...]) + ba_ref[...]
        gl_ref[...] = (jnp.minimum(u, 0.0) - jnp.log1p(jnp.exp(-jnp.abs(u)))) * (1.0 / GLA_TAU)

    p_ref[...] = jnp.dot(xn_ref[...], w_ref[...], preferred_element_type=F32).astype(p_ref.dtype)


def _gla_in(h, g, w, layer, n, wz, wa, ba):
    m, d = h.shape
    dk = wa.shape[1]
    tm = _pick_tile(m, 640)
    tn = _pick_tile(n, 2048)
    return pl.pallas_call(
        _gla_in_kernel,
        grid=(m // tm, n // tn),
        in_specs=[
            pl.BlockSpec((tm, d), lambda i, j: (i, 0)),
            pl.BlockSpec((1, d), lambda i, j: (0, 0)),
            pl.BlockSpec((None, d, tn), lambda i, j: (layer, 0, j)),
            pl.BlockSpec((d, LANES), lambda i, j: (0, 0)),
            pl.BlockSpec((LANES, dk), lambda i, j: (0, 0)),
            pl.BlockSpec((1, dk), lambda i, j: (0, 0)),
        ],
        out_specs=[pl.BlockSpec((tm, tn), lambda i, j: (i, j)),
                   pl.BlockSpec((tm, dk), lambda i, j: (i, 0))],
        out_shape=[jax.ShapeDtypeStruct((m, n), BF16), jax.ShapeDtypeStruct((m, dk), F32)],
        scratch_shapes=[pltpu.VMEM((tm, d), BF16)],
        compiler_params=_cparams("parallel", "arbitrary"),
        name="gla_in",
    )(h, g, w, wz, wa, ba)


def _gla_chunk_kernel(q_ref, k_ref, v_ref, gate_ref, gl_ref, gn_ref, z_ref, st_ref, *, scale):
    c = pl.program_id(1)
    nb, nh, hv, hk = st_ref.shape

    @pl.when(c == 0)
    def _():
        st_ref[...] = jnp.zeros_like(st_ref)

    row = lax.broadcasted_iota(jnp.int32, (CHUNK, CHUNK), 0)
    col = lax.broadcasted_iota(jnp.int32, (CHUNK, CHUNK), 1)
    causal = row >= col
    tril = jnp.where(causal, 1.0, 0.0).astype(BF16)
    live = jnp.logical_or(c > 0, lax.broadcasted_iota(jnp.int32, gl_ref.shape[1:], 0) >= N_DUMMY)
    ksl = lambda x, h: x[:, h * hk:(h + 1) * hk]
    vsl = lambda x, h: x[:, h * hv:(h + 1) * hv]
    seqs = range(nb)
    cells = [(b, h) for b in seqs for h in range(nh)]

    gl = [jnp.where(live, gl_ref[b], 0.0) for b in seqs]
    bc = [_dot_sel_left(tril, gl[b]) for b in seqs]
    b_last = [bc[b][CHUNK - 1:CHUNK, :] for b in seqs]
    e_last = [jnp.exp(b_last[b]) for b in seqs]
    q_t = [(q_ref[b] * scale * jnp.exp(bc[b])).astype(BF16) for b in seqs]
    k_t = [(k_ref[b] * jnp.exp(-bc[b])).astype(BF16) for b in seqs]
    k_h = [(k_ref[b] * jnp.exp(b_last[b] - bc[b])).astype(BF16) for b in seqs]
    v = {(b, h): vsl(v_ref[b], h).astype(BF16) for b, h in cells}
    att = {(b, h): jnp.where(causal, _dot_nt(ksl(q_t[b], h), ksl(k_t[b], h)), 0.0).astype(BF16) for b, h in cells}
    st = {(b, h): st_ref[b, h] for b, h in cells}
    o = {(b, h): _dot_nt(ksl(q_t[b], h), st[b, h]) + jnp.dot(att[b, h], v[b, h], preferred_element_type=F32)
         for b, h in cells}
    for b, h in cells:
        st_ref[b, h] = st[b, h] * ksl(e_last[b], h) + _dot_tn(v[b, h], ksl(k_h[b], h))
    for b, h in cells:
        on = o[b, h] * lax.rsqrt(jnp.mean(o[b, h] * o[b, h], axis=-1, keepdims=True) + GLA_HEAD_EPS)
        gate = vsl(gate_ref[b], h).astype(F32)
        z_ref[b, :, h * hv:(h + 1) * hv] = (on * vsl(gn_ref[...], h) * (gate * _sigmoid(gate))).astype(z_ref.dtype)


def _gla_chunk(p, glog, gn_w, batch, d):
    tp = p.shape[0] // batch
    nc = tp // CHUNK
    dk = d // 2
    hk = dk // GLA_HEADS
    hv = d // GLA_HEADS
    p3 = p.reshape(batch, tp, p.shape[1])
    nb = GLA_SEQ_GROUP if batch % GLA_SEQ_GROUP == 0 else 1
    z = pl.pallas_call(
        functools.partial(_gla_chunk_kernel, scale=hk ** -0.5),
        grid=(batch // nb, nc),
        in_specs=[
            pl.BlockSpec((nb, CHUNK, dk), lambda g, c: (g, c, 0)),
            pl.BlockSpec((nb, CHUNK, dk), lambda g, c: (g, c, 1)),
            pl.BlockSpec((nb, CHUNK, d), lambda g, c: (g, c, 1)),
            pl.BlockSpec((nb, CHUNK, d), lambda g, c: (g, c, 2)),
            pl.BlockSpec((nb, CHUNK, dk), lambda g, c: (g, c, 0)),
            pl.BlockSpec((1, d), lambda g, c: (0, 0)),
        ],
        out_specs=pl.BlockSpec((nb, CHUNK, d), lambda g, c: (g, c, 0)),
        out_shape=jax.ShapeDtypeStruct((batch, tp, d), BF16),
        scratch_shapes=[pltpu.VMEM((nb, GLA_HEADS, hv, hk), F32)],
        compiler_params=_cparams("parallel", "arbitrary"),
        name="gla_chunk",
    )(p3, p3, p3, p3, glog.reshape(batch, tp, dk), gn_w)
    return z.reshape(batch * tp, d)


def _rwkv_layer(h, batch, j, v_first, norm_g, rw_mix, rw_w_rkv, rw_w0, rw_w1, rw_w2, rw_a0, rw_a1, rw_a2,
                rw_v0, rw_v1, rw_v2, rw_g1, rw_g2, rw_k_k, rw_k_a, rw_r_k, rw_ln_w, rw_ln_b, rw_w_o):
    m, d = h.shape
    npair = d // LANES
    row = lambda t: t.reshape(1, d)
    pairs = lambda t: t.reshape(npair, 1, LANES)
    h3 = h.reshape(batch, m // batch, d)
    g = row(norm_g)
    rkv = _rkv(h3, g, rw_mix[j], rw_w_rkv, j)
    branches = [(1, "tanh", "decay"), (4, "none", "sigmoid"), (5, "sigmoid", "none")]
    weights = [_pad_rank(rw_w1[j], rw_w2[j]) + (row(rw_w0[j]),),
               _pad_rank(rw_a1[j], rw_a2[j]) + (row(rw_a0[j]),),
               _pad_rank(rw_g1[j], rw_g2[j]) + (jnp.zeros((1, d), F32),)]
    if j > 0:
        branches.append((3, "none", "sigmoid"))
        weights.append(_pad_rank(rw_v1[j - 1], rw_v2[j - 1]) + (row(rw_v0[j - 1]),))
    outs = _lora(h3, g, rw_mix[j], tuple(branches), weights)
    lw, ag, gg = outs[:3]
    vres = (v_first, outs[3]) if j > 0 else None
    z = _rw_scan(rkv, lw, ag, gg, vres, pairs(rw_k_k[j]), pairs(rw_k_a[j]), pairs(rw_r_k[j]),
                 pairs(rw_ln_w[j]), pairs(rw_ln_b[j]), batch)
    h = _proj_residual(z, rw_w_o, j, h, pair_major=True)
    return h, rkv


def _gla_layer(h, batch, j, norm_g, gla_w_in, gla_w_in_bf, gla_w_a2, gla_b_a, gla_gn_w, gla_w_o):
    m, d = h.shape
    dk = d // 2
    n_main = 2 * dk + 2 * d
    w_in = gla_w_in[j]
    rank = w_in.shape[1] - n_main
    g = norm_g.reshape(1, d)
    wz = jnp.pad(w_in[:, n_main:], ((0, 0), (0, LANES - rank))).astype(BF16)
    wa = jnp.pad(gla_w_a2[j], ((0, LANES - rank), (0, 0)))
    p, glog = _gla_in(h, g, gla_w_in_bf, j, n_main, wz, wa, gla_b_a[j].reshape(1, dk))
    z = _gla_chunk(p, glog, gla_gn_w[j].reshape(1, d), batch, d)
    return _proj_residual(z, gla_w_o, j, h, pair_major=False)


def kernel(x, meta, norm_mix, norm_mlp, norm_f, mlp_w1, mlp_w2, rw_mix, rw_w_rkv, rw_w0, rw_w1, rw_w2, rw_a0, rw_a1, rw_a2, rw_v0, rw_v1, rw_v2, rw_g1, rw_g2, rw_k_k, rw_k_a, rw_r_k, rw_ln_w, rw_ln_b, rw_w_o, gla_w_in, gla_w_a2, gla_b_a, gla_gn_w, gla_w_o):
    batch, seq, d = x.shape
    depth = norm_mix.shape[0]
    assert seq % CHUNK == 0 and d % (2 * LANES) == 0 and meta.shape[0] == N_META
    tp = LEAD + seq
    lead = jnp.concatenate([jnp.zeros((N_DUMMY, d), x.dtype), meta.astype(x.dtype)], axis=0)
    h = jnp.concatenate([jnp.broadcast_to(lead[None], (batch, LEAD, d)), x], axis=1).reshape(batch * tp, d)
    gf = norm_f.reshape(1, d)
    w1_bf, w2_bf = mlp_w1.astype(BF16), mlp_w2.astype(BF16)
    rkv_bf, rwo_bf = rw_w_rkv.astype(BF16), rw_w_o.astype(BF16)
    gla_w_in_bf, gla_w_o_bf = gla_w_in.astype(BF16), gla_w_o.astype(BF16)
    v_first = None
    for i in range(depth):
        j = i // 2
        if i % 2 == 0:
            h, rkv = _rwkv_layer(h, batch, j, v_first, norm_mix[i], rw_mix, rkv_bf, rw_w0, rw_w1, rw_w2,
                                 rw_a0, rw_a1, rw_a2, rw_v0, rw_v1, rw_v2, rw_g1, rw_g2, rw_k_k, rw_k_a,
                                 rw_r_k.reshape(rw_r_k.shape[0], d), rw_ln_w, rw_ln_b, rwo_bf)
            if j == 0:
                v_first = rkv
        else:
            h = _gla_layer(h, batch, j, norm_mix[i], gla_w_in, gla_w_in_bf, gla_w_a2, gla_b_a, gla_gn_w, gla_w_o_bf)
        g_mlp = norm_mlp[i].reshape(1, d)
        if i == depth - 1:
            return _mlp_final(h.reshape(batch, tp, d), g_mlp, w1_bf, w2_bf, i, gf, *MLP_FINAL_TILE)
        h = _mlp(h, g_mlp, w1_bf, w2_bf, i, gf, *MLP_TILE)
```

```python
import functools

import jax
import jax.numpy as jnp
from jax import lax
from jax.experimental import pallas as pl
from jax.experimental.pallas import tpu as pltpu

F32 = jnp.float32
BF16 = jnp.bfloat16

N_META = 16
CHUNK = 64
LEAD = CHUNK
N_DUMMY = LEAD - N_META
NORM_EPS = 1e-6

RW_HEAD = 64
LANES = 128
RW_GN_EPS = 64e-5
RW_DECAY_SCALE = 0.6065306597126334
PAIR_GROUP = 16

GLA_HEADS = 4
GLA_TAU = 16.0
GLA_HEAD_EPS = 1e-5
GLA_SEQ_GROUP = 4

VMEM_LIMIT = 56 * 1024 * 1024
MLP_TILE = (832, 1024)
MLP_FINAL_TILE = (512, 1024)


def _cparams(*sem):
    return pltpu.CompilerParams(dimension_semantics=sem, vmem_limit_bytes=VMEM_LIMIT)


def _dot(a, b):
    return jnp.dot(a.astype(BF16), b.astype(BF16), preferred_element_type=F32)


def _dot_nt(a, b):
    return lax.dot_general(a.astype(BF16), b.astype(BF16), (((1,), (1,)), ((), ())),
                           preferred_element_type=F32)


def _dot_tn(a, b):
    return lax.dot_general(a.astype(BF16), b.astype(BF16), (((0,), (0,)), ((), ())),
                           preferred_element_type=F32)


def _split2(x):
    hi = x.astype(BF16)
    return hi, (x - hi.astype(F32)).astype(BF16)


def _dot_sel_left(sel, x):
    hi, lo = _split2(x)
    return jnp.dot(sel, hi, preferred_element_type=F32) + jnp.dot(sel, lo, preferred_element_type=F32)


def _rms(x, g, eps):
    return x * lax.rsqrt(jnp.mean(x * x, axis=-1, keepdims=True) + eps) * g


def _sigmoid(x):
    return 1.0 / (1.0 + jnp.exp(-x))


def _pick_tile(n, target):
    best = None
    for t in range(16, min(n, target) + 1, 16):
        if n % t == 0:
            best = t
    assert best is not None, (n, target)
    return best


def _side_cast_specs(sides, n_steps, step_of):
    in_specs, out_specs, out_shapes = [], [], []
    for arr, layer in sides:
        _, rows, cols = arr.shape
        blk = next(b for b in range(16, rows + 1, 16) if rows % b == 0 and rows // b <= n_steps)
        last = rows // blk - 1
        in_specs.append(pl.BlockSpec(
            (None, blk, cols), lambda *ids, layer=layer, last=last: (layer, jnp.minimum(step_of(*ids), last), 0)))
        out_specs.append(pl.BlockSpec(
            (None, blk, cols), lambda *ids, last=last: (0, jnp.minimum(step_of(*ids), last), 0)))
        out_shapes.append(jax.ShapeDtypeStruct((1, rows, cols), BF16))
    return in_specs, out_specs, out_shapes


def _mlp_kernel(*refs, final, n_side):
    h_ref, g_ref, w1_ref, w2_ref, gf_ref = refs[:5]
    side_in = refs[5:5 + n_side]
    o_ref = refs[5 + n_side]
    side_out = refs[6 + n_side:6 + 2 * n_side]
    xn_ref = refs[6 + 2 * n_side]
    ff_axis = 2 if final else 1
    if final:
        o_ref = o_ref.at[0]
    j = pl.program_id(ff_axis)

    @pl.when(j == 0)
    def _():
        x = h_ref[...]
        xn_ref[...] = _rms(x, g_ref[...], NORM_EPS).astype(BF16)
        o_ref[...] = x

    hid = jnp.dot(xn_ref[...], w1_ref[...], preferred_element_type=F32)
    hid = jnp.maximum(hid, 0.0)
    hid = hid * hid
    o_ref[...] += jnp.dot(hid.astype(BF16), w2_ref[...], preferred_element_type=F32)
    for src, dst in zip(side_in, side_out):
        dst[...] = src[...].astype(BF16)

    if final:
        @pl.when(j == pl.num_programs(ff_axis) - 1)
        def _():
            o_ref[...] = _rms(o_ref[...], gf_ref[...], NORM_EPS)


def _mlp(h, g, w1, w2, layer, gf, tm_target, tf_target, sides=()):
    m, d = h.shape
    ff = w1.shape[2]
    tm = _pick_tile(m, tm_target)
    tf = _pick_tile(ff, tf_target)
    nj = ff // tf
    side_in, side_out, side_shapes = _side_cast_specs(sides, (m // tm) * nj, lambda i, j: i * nj + j)
    outs = pl.pallas_call(
        functools.partial(_mlp_kernel, final=False, n_side=len(sides)),
        grid=(m // tm, nj),
        in_specs=[
            pl.BlockSpec((tm, d), lambda i, j: (i, 0)),
            pl.BlockSpec((1, d), lambda i, j: (0, 0)),
            pl.BlockSpec((None, d, tf), lambda i, j: (layer, 0, j)),
            pl.BlockSpec((None, tf, d), lambda i, j: (layer, j, 0)),
            pl.BlockSpec((1, d), lambda i, j: (0, 0)),
        ] + side_in,
        out_specs=[pl.BlockSpec((tm, d), lambda i, j: (i, 0))] + side_out,
        out_shape=[jax.ShapeDtypeStruct((m, d), F32)] + side_shapes,
        scratch_shapes=[pltpu.VMEM((tm, d), BF16)],
        compiler_params=_cparams("arbitrary", "arbitrary"),
        name="mlp",
    )(h, g, w1, w2, gf, *[arr for arr, _ in sides])
    return outs[0], outs[1:]


def _mlp_final(h3, g, w1, w2, layer, gf, tm_target, tf_target):
    b, tp, d = h3.shape
    seq = tp - LEAD
    ff = w1.shape[2]
    tm = _pick_tile(seq, tm_target)
    tf = _pick_tile(ff, tf_target)
    return pl.pallas_call(
        functools.partial(_mlp_kernel, final=True, n_side=0),
        grid=(b, seq // tm, ff // tf),
        in_specs=[
            pl.BlockSpec((pl.Element(tm), pl.Element(d)),
                         lambda bi, i, j: (pl.multiple_of(bi * tp + LEAD + i * tm, 16), 0)),
            pl.BlockSpec((1, d), lambda bi, i, j: (0, 0)),
            pl.BlockSpec((None, d, tf), lambda bi, i, j: (layer, 0, j)),
            pl.BlockSpec((None, tf, d), lambda bi, i, j: (layer, j, 0)),
            pl.BlockSpec((1, d), lambda bi, i, j: (0, 0)),
        ],
        out_specs=pl.BlockSpec((1, tm, d), lambda bi, i, j: (bi, i, 0)),
        out_shape=jax.ShapeDtypeStruct((b, seq, d), F32),
        scratch_shapes=[pltpu.VMEM((tm, d), BF16)],
        compiler_params=_cparams("parallel", "parallel", "arbitrary"),
        name="mlp_final",
    )(h3.reshape(b * tp, d), g, w1, w2, gf)


def _shifted_norm(h_ref, hp_ref, g_ref):
    g = g_ref[...]
    hn = _rms(h_ref[0], g, NORM_EPS)
    pn = _rms(hp_ref[0], g, NORM_EPS)[7:8]
    pn = jnp.where(pl.program_id(1) == 0, 0.0, pn)
    row = lax.broadcasted_iota(jnp.int32, hn.shape, 0)
    prev = jnp.where(row == 0, pn, pltpu.roll(hn, 1, axis=0))
    return hn, prev - hn


def _to_pairs(o_ref, lead, val):
    for p in range(val.shape[1] // LANES):
        o_ref[lead + (p,)] = val[:, p * LANES:(p + 1) * LANES].astype(o_ref.dtype)


def _shift_specs(tm, d):
    nb = tm // 8
    return [
        pl.BlockSpec((1, tm, d), lambda bi, i: (bi, i, 0)),
        pl.BlockSpec((1, 8, d), lambda bi, i: (bi, jnp.maximum(i * nb - 1, 0), 0)),
        pl.BlockSpec((1, d), lambda bi, i: (0, 0)),
        pl.BlockSpec((6, d), lambda bi, i: (0, 0)),
    ]


RKV_MIX_ROWS = (0, 2, 3)


def _rkv_kernel(*refs, n_side):
    h_ref, hp_ref, g_ref, mix_ref, w_ref = refs[:5]
    side_in = refs[5:5 + n_side]
    o_ref = refs[5 + n_side]
    side_out = refs[6 + n_side:]
    hn, xx = _shifted_norm(h_ref, hp_ref, g_ref)
    for s, mrow in enumerate(RKV_MIX_ROWS):
        x = (hn + xx * mix_ref[mrow:mrow + 1, :]).astype(BF16)
        _to_pairs(o_ref, (s,), jnp.dot(x, w_ref[s], preferred_element_type=F32))
    for src, dst in zip(side_in, side_out):
        dst[...] = src[...].astype(BF16)


def _rkv(h3, g, mix, w, layer, sides=()):
    b, tp, d = h3.shape
    tm = _pick_tile(tp, 208)
    nt = tp // tm
    npair = d // LANES
    side_in, side_out, side_shapes = _side_cast_specs(sides, b * nt, lambda bi, i: bi * nt + i)
    outs = pl.pallas_call(
        functools.partial(_rkv_kernel, n_side=len(sides)),
        grid=(b, nt),
        in_specs=_shift_specs(tm, d) + [
            pl.BlockSpec((None, 3, d, d), lambda bi, i: (layer, 0, 0, 0), pipeline_mode=pl.Buffered(1)),
        ] + side_in,
        out_specs=[pl.BlockSpec((3, npair, tm, LANES), lambda bi, i: (0, 0, bi * nt + i, 0))] + side_out,
        out_shape=[jax.ShapeDtypeStruct((3, npair, b * tp, LANES), F32)] + side_shapes,
        compiler_params=_cparams("arbitrary", "arbitrary"),
        name="rw_rkv",
    )(h3, h3, g, mix, w, *[arr for arr, _ in sides])
    return outs[0], outs[1:]


def _lora_kernel(*refs, branches):
    h_ref, hp_ref, g_ref, mix_ref = refs[:4]
    nb = len(branches)
    w_refs = refs[4:4 + 3 * nb]
    o_refs = refs[4 + 3 * nb:]
    hn, xx = _shifted_norm(h_ref, hp_ref, g_ref)
    for bi, (mrow, mid, out) in enumerate(branches):
        w1_ref, w2_ref, b_ref = w_refs[3 * bi:3 * bi + 3]
        x = (hn + xx * mix_ref[mrow:mrow + 1, :]).astype(BF16)
        z = jnp.dot(x, w1_ref[...], preferred_element_type=F32)
        if mid == "tanh":
            z = jnp.tanh(z)
        elif mid == "sigmoid":
            z = _sigmoid(z)
        y = jnp.dot(z.astype(BF16), w2_ref[...], preferred_element_type=F32)
        if out == "decay":
            y = -RW_DECAY_SCALE * _sigmoid(y + b_ref[...])
        elif out == "sigmoid":
            y = _sigmoid(y + b_ref[...])
        _to_pairs(o_refs[bi], (), y)


def _lora(h3, g, mix, branches, weights):
    b, tp, d = h3.shape
    tm = _pick_tile(tp, 416)
    nt = tp // tm
    npair = d // LANES
    w_specs, w_args = [], []
    for w1, w2, bias in weights:
        r = w1.shape[1]
        w_specs += [pl.BlockSpec((d, r), lambda bi, i: (0, 0)),
                    pl.BlockSpec((r, d), lambda bi, i: (0, 0)),
                    pl.BlockSpec((1, d), lambda bi, i: (0, 0))]
        w_args += [w1, w2, bias]
    out_spec = pl.BlockSpec((npair, tm, LANES), lambda bi, i: (0, bi * nt + i, 0))
    out_shape = jax.ShapeDtypeStruct((npair, b * tp, LANES), F32)
    return pl.pallas_call(
        functools.partial(_lora_kernel, branches=branches),
        grid=(b, nt),
        in_specs=_shift_specs(tm, d) + w_specs,
        out_specs=[out_spec] * len(branches),
        out_shape=[out_shape] * len(branches),
        compiler_params=_cparams("parallel", "parallel"),
        name="rw_lora",
    )(h3, h3, g, mix, *w_args)


def _pad_rank(w1, w2):
    r = w1.shape[1]
    rp = -(-r // LANES) * LANES
    return (jnp.pad(w1, ((0, 0), (0, rp - r))).astype(BF16),
            jnp.pad(w2, ((0, rp - r), (0, 0))).astype(BF16))


def _stack_heads(x, lane_lo):
    return jnp.concatenate([jnp.where(lane_lo, x, 0.0), jnp.where(lane_lo, 0.0, x)], axis=0)


def _rw_scan_kernel(*refs, has_vres):
    if has_vres:
        (rkv_ref, lw_ref, ag_ref, gg_ref, vf_ref, vg_ref,
         kk_ref, ka_ref, rk_ref, lnw_ref, lnb_ref, z_ref, st_ref) = refs
    else:
        (rkv_ref, lw_ref, ag_ref, gg_ref,
         kk_ref, ka_ref, rk_ref, lnw_ref, lnb_ref, z_ref, st_ref) = refs
    c = pl.program_id(1)
    npair = z_ref.shape[0]
    group = min(PAIR_GROUP, npair)
    assert npair % group == 0
    two_l = 2 * CHUNK

    @pl.when(c == 0)
    def _():
        st_ref[...] = jnp.zeros_like(st_ref)

    row = lax.broadcasted_iota(jnp.int32, (two_l, two_l), 0)
    col = lax.broadcasted_iota(jnp.int32, (two_l, two_l), 1)
    t_row = row % CHUNK
    t_col = col % CHUNK
    t_half = lax.broadcasted_iota(jnp.int32, (CHUNK, two_l), 0)
    s_half = lax.broadcasted_iota(jnp.int32, (CHUNK, two_l), 1) % CHUNK
    strict_t = t_half > s_half
    incl_t = t_half >= s_half
    same_head = (row // CHUNK) == (col // CHUNK)
    tril = jnp.where(lax.broadcasted_iota(jnp.int32, (CHUNK, CHUNK), 0)
                     >= lax.broadcasted_iota(jnp.int32, (CHUNK, CHUNK), 1), 1.0, 0.0).astype(BF16)
    diag = row == col
    lane_lo = lax.broadcasted_iota(jnp.int32, (CHUNK, LANES), 1) < RW_HEAD
    live = jnp.logical_or(c > 0, lax.broadcasted_iota(jnp.int32, (CHUNK, LANES), 0) >= N_DUMMY)

    def each(fn, *lists):
        return [fn(*xs) for xs in zip(*lists)]

    ones_bd = jnp.where(same_head, 1.0, 0.0).astype(BF16)

    def head_sum_lanes(x):
        lo = jnp.sum(jnp.where(lane_lo, x, 0.0), axis=-1, keepdims=True)
        hi = jnp.sum(jnp.where(lane_lo, 0.0, x), axis=-1, keepdims=True)
        return jnp.where(lane_lo, lo, hi)

    def head_sum_matmul(x):
        hi, lo = _split2(x)
        both = jnp.dot(jnp.concatenate([hi, lo], axis=0), ones_bd, preferred_element_type=F32)
        return both[:x.shape[0]] + both[x.shape[0]:]

    def swap_halves(x):
        return pltpu.roll(x, RW_HEAD, axis=1)

    def group_body(gi, carry):
        ps = [gi * group + i for i in range(group)]
        r = [rkv_ref[0, p] for p in ps]
        k0 = [rkv_ref[1, p] for p in ps]
        v = [rkv_ref[2, p] for p in ps]
        lw = [lw_ref[p] for p in ps]
        ag = [ag_ref[p] for p in ps]
        if has_vres:
            v = [vi + (vf_ref[0, p] - vi) * vg_ref[p] for vi, p in zip(v, ps)]
        kk = [ki * kk_ref[p] for ki, p in zip(k0, ps)]
        k = [ki * (1.0 + (ai - 1.0) * ka_ref[p]) for ki, ai, p in zip(k0, ag, ps)]
        kk = each(lambda kki: kki / jnp.maximum(jnp.sqrt(head_sum_lanes(kki * kki)), 1e-12), kk)
        bonus = [head_sum_lanes(ri * ki * rk_ref[p]) * vi for ri, ki, vi, p in zip(r, k, v, ps)]

        cs = each(lambda lwi: _dot_sel_left(tril, lwi), lw)
        e_pos = each(jnp.exp, cs)
        e_neg = each(lambda ci: jnp.exp(-ci), cs)
        w_all = each(lambda ei: ei[CHUNK - 1:CHUNK, :], e_pos)
        a_t = each(lambda kki, ci, lwi: -kki * jnp.exp(ci - lwi), kk, cs, lw)
        r_t = each(lambda ri, ei: ri * ei, r, e_pos)
        a_s = each(lambda ati: _stack_heads(ati, lane_lo), a_t)
        r_s = each(lambda rti: _stack_heads(rti, lane_lo), r_t)
        b_t = each(lambda kki, ai, ei: kki * ai * ei, kk, ag, e_neg)
        k_t = each(lambda ki, ei: ki * ei, k, e_neg)
        v_s = each(lambda vi: _stack_heads(vi, lane_lo).astype(BF16), v)
        bh_t = each(lambda bti, wi: _stack_heads(bti * wi, lane_lo).T.astype(BF16), b_t, w_all)
        kh_t = each(lambda kti, wi: _stack_heads(kti * wi, lane_lo).T.astype(BF16), k_t, w_all)

        gram = each(lambda ati, rti, bti, kti: _dot_nt(
            jnp.concatenate([ati, rti], axis=0),
            jnp.concatenate([_stack_heads(bti, lane_lo), _stack_heads(kti, lane_lo)], axis=0)),
            a_t, r_t, b_t, k_t)
        block = lambda gm, keep, rows, cols: _stack_heads(
            jnp.where(keep, gm[rows * CHUNK:(rows + 1) * CHUNK, cols * two_l:(cols + 1) * two_l], 0.0),
            lane_lo).astype(BF16)
        pw = each(lambda gm: block(gm, strict_t, 0, 0), gram)
        a_ak = each(lambda gm: block(gm, strict_t, 0, 1), gram)
        a_rb = each(lambda gm: block(gm, incl_t, 1, 0), gram)
        a_rk = each(lambda gm: block(gm, incl_t, 1, 1), gram)

        xv = each(lambda aki, ari, khi, vsi: jnp.dot(jnp.concatenate([aki, ari, khi], axis=0), vsi,
                                                     preferred_element_type=F32), a_ak, a_rk, kh_t, v_s)

        z = each(lambda asi, xi: asi + swap_halves(xi[:two_l]), a_s, xv)
        n_sq = CHUNK.bit_length() - 1
        for s in range(n_sq):
            if s + 1 < n_sq:
                prod = each(lambda pi, zi: jnp.dot(pi, jnp.concatenate([zi.astype(BF16), pi], axis=1),
                                                   preferred_element_type=F32), pw, z)
                z = each(lambda zi, pr: zi + pr[:, :LANES], z, prod)
                pw = each(lambda pr: pr[:, LANES:].astype(BF16), prod)
            else:
                z = each(lambda zi, pi: zi + jnp.dot(pi, zi.astype(BF16), preferred_element_type=F32), z, pw)
        zb = each(lambda zi: zi.astype(BF16), z)

        rb = each(lambda ai, bhi, zi: jnp.dot(jnp.concatenate([ai, bhi], axis=0), zi,
                                              preferred_element_type=F32), a_rb, bh_t, zb)
        r_hat = each(lambda rsi, rbi: rsi + jnp.where(same_head, rbi[:two_l], 0.0), r_s, rb)
        y_hat = each(lambda rbi, xi: swap_halves(jnp.where(same_head, 0.0, rbi[:two_l])) + xi[two_l:2 * two_l],
                     rb, xv)
        m_mat = each(lambda wi, rbi: jnp.where(diag, wi, 0.0) + jnp.where(same_head, rbi[two_l:], 0.0),
                     w_all, rb)
        n_mat = each(lambda rbi, xi: swap_halves(jnp.where(same_head, 0.0, rbi[two_l:])) + xi[2 * two_l:],
                     rb, xv)

        hb = [st_ref[p].astype(BF16) for p in ps]
        ys = each(lambda rh, mm, hi: jnp.dot(jnp.concatenate([rh, mm], axis=0).astype(BF16), hi,
                                             preferred_element_type=F32), r_hat, m_mat, hb)
        for p, ysi, nm in zip(ps, ys, n_mat):
            st_ref[p] = ysi[two_l:] + nm
        y2 = each(lambda ysi, yh: ysi[:two_l] + yh, ys, y_hat)
        y = each(lambda yi: yi[:CHUNK] + yi[CHUNK:], y2)

        dy = each(lambda yi: yi - head_sum_matmul(yi) * (1.0 / RW_HEAD), y)
        var = each(lambda di: head_sum_matmul(di * di) * (1.0 / RW_HEAD), dy)
        for p, di, vi, bi in zip(ps, dy, var, bonus):
            yn = di * lax.rsqrt(vi + RW_GN_EPS) * lnw_ref[p] + lnb_ref[p]
            z_ref[p] = jnp.where(live, (yn + bi) * gg_ref[p], 0.0).astype(z_ref.dtype)
        return carry

    lax.fori_loop(0, npair // group, group_body, 0)


def _rw_scan(rkv, lw, ag, gg, vres, kk, ka, rk, lnw, lnb, batch):
    _, npair, m, _ = rkv.shape
    nc = m // batch // CHUNK
    row_map3 = lambda bi, c: (0, bi * nc + c, 0)
    row_map4 = lambda bi, c: (0, 0, bi * nc + c, 0)
    par = pl.BlockSpec((npair, 1, LANES), lambda bi, c: (0, 0, 0))
    tile = pl.BlockSpec((npair, CHUNK, LANES), row_map3)
    in_specs = [pl.BlockSpec((3, npair, CHUNK, LANES), row_map4), tile, tile, tile]
    args = [rkv, lw, ag, gg]
    if vres is not None:
        vfirst, vgate = vres
        in_specs += [pl.BlockSpec((1, npair, CHUNK, LANES), lambda bi, c: (2, 0, bi * nc + c, 0)), tile]
        args += [vfirst, vgate]
    in_specs += [par] * 5
    args += [kk, ka, rk, lnw, lnb]
    return pl.pallas_call(
        functools.partial(_rw_scan_kernel, has_vres=vres is not None),
        grid=(batch, nc),
        in_specs=in_specs,
        out_specs=tile,
        out_shape=jax.ShapeDtypeStruct((npair, m, LANES), BF16),
        scratch_shapes=[pltpu.VMEM((npair, 2 * CHUNK, LANES), F32)],
        compiler_params=_cparams("parallel", "arbitrary"),
        name="rw_scan",
    )(*args)


def _proj_kernel(x_ref, w_ref, h_ref, o_ref, *, pair_major):
    if pair_major:
        x = jnp.concatenate([x_ref[p] for p in range(x_ref.shape[0])], axis=1)
    else:
        x = x_ref[...]
    o_ref[...] = h_ref[...] + jnp.dot(x, w_ref[...], preferred_element_type=F32)


def _proj_residual(x, w, layer, h, pair_major):
    m, d = h.shape
    kdim = w.shape[1]
    tm = _pick_tile(m, 640)
    if pair_major:
        x_spec = pl.BlockSpec((x.shape[0], tm, LANES), lambda i: (0, i, 0))
    else:
        x_spec = pl.BlockSpec((tm, kdim), lambda i: (i, 0))
    return pl.pallas_call(
        functools.partial(_proj_kernel, pair_major=pair_major),
        grid=(m // tm,),
        in_specs=[x_spec,
                  pl.BlockSpec((None, kdim, d), lambda i: (layer, 0, 0)),
                  pl.BlockSpec((tm, d), lambda i: (i, 0))],
        out_specs=pl.BlockSpec((tm, d), lambda i: (i, 0)),
        out_shape=jax.ShapeDtypeStruct((m, d), F32),
        compiler_params=_cparams("parallel"),
        name="proj_residual",
    )(x, w, h)


def _gla_in_kernel(h_ref, g_ref, w_ref, wz_ref, wa_ref, ba_ref, p_ref, gl_ref, xn_ref):
    @pl.when(pl.program_id(1) == 0)
    def _():
        xn = _rms(h_ref[...], g_ref[...], NORM_EPS).astype(BF16)
        xn_ref[...] = xn
        za = jnp.dot(xn, wz_ref[...], preferred_element_type=F32)
        u = _dot(za, wa_ref[...]) + ba_ref[...]
        gl_ref[...] = (jnp.minimum(u, 0.0) - jnp.log1p(jnp.exp(-jnp.abs(u)))) * (1.0 / GLA_TAU)

    p_ref[...] = jnp.dot(xn_ref[...], w_ref[...], preferred_element_type=F32).astype(p_ref.dtype)


def _gla_in(h, g, w, layer, n, wz, wa, ba):
    m, d = h.shape
    dk = wa.shape[1]
    tm = _pick_tile(m, 640)
    tn = _pick_tile(n, 2048)
    return pl.pallas_call(
        _gla_in_kernel,
        grid=(m // tm, n // tn),
        in_specs=[
            pl.BlockSpec((tm, d), lambda i, j: (i, 0)),
            pl.BlockSpec((1, d), lambda i, j: (0, 0)),
            pl.BlockSpec((None, d, tn), lambda i, j: (layer, 0, j)),
            pl.BlockSpec((d, LANES), lambda i, j: (0, 0)),
            pl.BlockSpec((LANES, dk), lambda i, j: (0, 0)),
            pl.BlockSpec((1, dk), lambda i, j: (0, 0)),
        ],
        out_specs=[pl.BlockSpec((tm, tn), lambda i, j: (i, j)),
                   pl.BlockSpec((tm, dk), lambda i, j: (i, 0))],
        out_shape=[jax.ShapeDtypeStruct((m, n), BF16), jax.ShapeDtypeStruct((m, dk), F32)],
        scratch_shapes=[pltpu.VMEM((tm, d), BF16)],
        compiler_params=_cparams("parallel", "arbitrary"),
        name="gla_in",
    )(h, g, w, wz, wa, ba)


def _gla_chunk_kernel(q_ref, k_ref, v_ref, gate_ref, gl_ref, gn_ref, z_ref, st_ref, *, scale):
    c = pl.program_id(1)
    nb, nh, hv, hk = st_ref.shape

    @pl.when(c == 0)
    def _():
        st_ref[...] = jnp.zeros_like(st_ref)

    row = lax.broadcasted_iota(jnp.int32, (CHUNK, CHUNK), 0)
    col = lax.broadcasted_iota(jnp.int32, (CHUNK, CHUNK), 1)
    causal = row >= col
    tril = jnp.where(causal, 1.0, 0.0).astype(BF16)
    live = jnp.logical_or(c > 0, lax.broadcasted_iota(jnp.int32, gl_ref.shape[1:], 0) >= N_DUMMY)
    ksl = lambda x, h: x[:, h * hk:(h + 1) * hk]
    vsl = lambda x, h: x[:, h * hv:(h + 1) * hv]
    seqs = range(nb)
    cells = [(b, h) for b in seqs for h in range(nh)]

    gl = [jnp.where(live, gl_ref[b], 0.0) for b in seqs]
    bc = [_dot_sel_left(tril, gl[b]) for b in seqs]
    b_last = [bc[b][CHUNK - 1:CHUNK, :] for b in seqs]
    e_last = [jnp.exp(b_last[b]) for b in seqs]
    q_t = [(q_ref[b] * scale * jnp.exp(bc[b])).astype(BF16) for b in seqs]
    k_t = [(k_ref[b] * jnp.exp(-bc[b])).astype(BF16) for b in seqs]
    k_h = [(k_ref[b] * jnp.exp(b_last[b] - bc[b])).astype(BF16) for b in seqs]
    v = {(b, h): vsl(v_ref[b], h).astype(BF16) for b, h in cells}
    att = {(b, h): jnp.where(causal, _dot_nt(ksl(q_t[b], h), ksl(k_t[b], h)), 0.0).astype(BF16) for b, h in cells}
    st = {(b, h): st_ref[b, h] for b, h in cells}
    o = {(b, h): _dot_nt(ksl(q_t[b], h), st[b, h]) + jnp.dot(att[b, h], v[b, h], preferred_element_type=F32)
         for b, h in cells}
    for b, h in cells:
        st_ref[b, h] = st[b, h] * ksl(e_last[b], h) + _dot_tn(v[b, h], ksl(k_h[b], h))
    for b, h in cells:
        on = o[b, h] * lax.rsqrt(jnp.mean(o[b, h] * o[b, h], axis=-1, keepdims=True) + GLA_HEAD_EPS)
        gate = vsl(gate_ref[b], h).astype(F32)
        z_ref[b, :, h * hv:(h + 1) * hv] = (on * vsl(gn_ref[...], h) * (gate * _sigmoid(gate))).astype(z_ref.dtype)


def _gla_chunk(p, glog, gn_w, batch, d):
    tp = p.shape[0] // batch
    nc = tp // CHUNK
    dk = d // 2
    hk = dk // GLA_HEADS
    hv = d // GLA_HEADS
    p3 = p.reshape(batch, tp, p.shape[1])
    nb = GLA_SEQ_GROUP if batch % GLA_SEQ_GROUP == 0 else 1
    z = pl.pallas_call(
        functools.partial(_gla_chunk_kernel, scale=hk ** -0.5),
        grid=(batch // nb, nc),
        in_specs=[
            pl.BlockSpec((nb, CHUNK, dk), lambda g, c: (g, c, 0)),
            pl.BlockSpec((nb, CHUNK, dk), lambda g, c: (g, c, 1)),
            pl.BlockSpec((nb, CHUNK, d), lambda g, c: (g, c, 1)),
            pl.BlockSpec((nb, CHUNK, d), lambda g, c: (g, c, 2)),
            pl.BlockSpec((nb, CHUNK, dk), lambda g, c: (g, c, 0)),
            pl.BlockSpec((1, d), lambda g, c: (0, 0)),
        ],
        out_specs=pl.BlockSpec((nb, CHUNK, d), lambda g, c: (g, c, 0)),
        out_shape=jax.ShapeDtypeStruct((batch, tp, d), BF16),
        scratch_shapes=[pltpu.VMEM((nb, GLA_HEADS, hv, hk), F32)],
        compiler_params=_cparams("parallel", "arbitrary"),
        name="gla_chunk",
    )(p3, p3, p3, p3, glog.reshape(batch, tp, dk), gn_w)
    return z.reshape(batch * tp, d)


def _rwkv_layer(h, batch, j, v_first, norm_g, rw_mix, rw_w_rkv, rw_w0, rw_w1, rw_w2, rw_a0, rw_a1, rw_a2,
                rw_v0, rw_v1, rw_v2, rw_g1, rw_g2, rw_k_k, rw_k_a, rw_r_k, rw_ln_w, rw_ln_b, rw_w_o,
                sides=()):
    m, d = h.shape
    npair = d // LANES
    row = lambda t: t.reshape(1, d)
    pairs = lambda t: t.reshape(npair, 1, LANES)
    h3 = h.reshape(batch, m // batch, d)
    g = row(norm_g)
    rkv, cast = _rkv(h3, g, rw_mix[j], rw_w_rkv, 0, sides)
    if rw_w_o is None:
        rw_w_o = cast[0]
    branches = [(1, "tanh", "decay"), (4, "none", "sigmoid"), (5, "sigmoid", "none")]
    weights = [_pad_rank(rw_w1[j], rw_w2[j]) + (row(rw_w0[j]),),
               _pad_rank(rw_a1[j], rw_a2[j]) + (row(rw_a0[j]),),
               _pad_rank(rw_g1[j], rw_g2[j]) + (jnp.zeros((1, d), F32),)]
    if j > 0:
        branches.append((3, "none", "sigmoid"))
        weights.append(_pad_rank(rw_v1[j - 1], rw_v2[j - 1]) + (row(rw_v0[j - 1]),))
    outs = _lora(h3, g, rw_mix[j], tuple(branches), weights)
    lw, ag, gg = outs[:3]
    vres = (v_first, outs[3]) if j > 0 else None
    z = _rw_scan(rkv, lw, ag, gg, vres, pairs(rw_k_k[j]), pairs(rw_k_a[j]), pairs(rw_r_k[j]),
                 pairs(rw_ln_w[j]), pairs(rw_ln_b[j]), batch)
    h = _proj_residual(z, rw_w_o, 0, h, pair_major=True)
    return h, rkv, cast


def _gla_layer(h, batch, j, norm_g, gla_w_in, gla_w_in_bf, gla_w_a2, gla_b_a, gla_gn_w, gla_w_o):
    m, d = h.shape
    dk = d // 2
    n_main = 2 * dk + 2 * d
    w_in = gla_w_in[j]
    rank = w_in.shape[1] - n_main
    g = norm_g.reshape(1, d)
    wz = jnp.pad(w_in[:, n_main:], ((0, 0), (0, LANES - rank))).astype(BF16)
    wa = jnp.pad(gla_w_a2[j], ((0, LANES - rank), (0, 0)))
    p, glog = _gla_in(h, g, gla_w_in_bf, 0, n_main, wz, wa, gla_b_a[j].reshape(1, dk))
    z = _gla_chunk(p, glog, gla_gn_w[j].reshape(1, d), batch, d)
    return _proj_residual(z, gla_w_o, 0, h, pair_major=False)


def kernel(x, meta, norm_mix, norm_mlp, norm_f, mlp_w1, mlp_w2, rw_mix, rw_w_rkv, rw_w0, rw_w1, rw_w2, rw_a0, rw_a1, rw_a2, rw_v0, rw_v1, rw_v2, rw_g1, rw_g2, rw_k_k, rw_k_a, rw_r_k, rw_ln_w, rw_ln_b, rw_w_o, gla_w_in, gla_w_a2, gla_b_a, gla_gn_w, gla_w_o):
    batch, seq, d = x.shape
    depth = norm_mix.shape[0]
    assert seq % CHUNK == 0 and d % (2 * LANES) == 0 and meta.shape[0] == N_META
    tp = LEAD + seq
    lead = jnp.concatenate([jnp.zeros((N_DUMMY, d), x.dtype), meta.astype(x.dtype)], axis=0)
    h = jnp.concatenate([jnp.broadcast_to(lead[None], (batch, LEAD, d)), x], axis=1).reshape(batch * tp, d)
    gf = norm_f.reshape(1, d)
    rw_w_rkv2 = rw_w_rkv.reshape(rw_w_rkv.shape[0], 3 * d, d)

    def mixer_weights(i):
        j = i // 2
        mixer = [(rw_w_rkv2, j), (rw_w_o, j)] if i % 2 == 0 else [(gla_w_in, j), (gla_w_o, j)]
        return mixer + [(mlp_w1, i), (mlp_w2, i)]

    w_first, first_layer = mixer_weights(0)[0]
    cast = [w_first[first_layer:first_layer + 1].astype(BF16), None, None, None]
    v_first = None
    for i in range(depth):
        j = i // 2
        w_a, w_b, w1_bf, w2_bf = cast
        if i % 2 == 0:
            h, rkv, early = _rwkv_layer(h, batch, j, v_first, norm_mix[i], rw_mix, w_a.reshape(1, 3, d, d), rw_w0,
                                        rw_w1, rw_w2, rw_a0, rw_a1, rw_a2, rw_v0, rw_v1, rw_v2, rw_g1, rw_g2,
                                        rw_k_k, rw_k_a, rw_r_k.reshape(rw_r_k.shape[0], d), rw_ln_w, rw_ln_b, w_b,
                                        sides=mixer_weights(0)[1:] if i == 0 else ())
            if i == 0:
                _, w1_bf, w2_bf = early
            if j == 0:
                v_first = rkv
        else:
            h = _gla_layer(h, batch, j, norm_mix[i], gla_w_in, w_a, gla_w_a2, gla_b_a, gla_gn_w, w_b)
        g_mlp = norm_mlp[i].reshape(1, d)
        if i == depth - 1:
            return _mlp_final(h.reshape(batch, tp, d), g_mlp, w1_bf, w2_bf, 0, gf, *MLP_FINAL_TILE)
        h, cast = _mlp(h, g_mlp, w1_bf, w2_bf, 0, gf, *MLP_TILE, sides=mixer_weights(i + 1))
```

```python
import functools

import jax
import jax.numpy as jnp
from jax import lax
from jax.experimental import pallas as pl
from jax.experimental.pallas import tpu as pltpu

F32 = jnp.float32
BF16 = jnp.bfloat16

N_META = 16
CHUNK = 64
LEAD = CHUNK
N_DUMMY = LEAD - N_META
NORM_EPS = 1e-6

RW_HEAD = 64
LANES = 128
RW_GN_EPS = 64e-5
RW_DECAY_SCALE = 0.6065306597126334
PAIR_GROUP = 16

GLA_HEADS = 4
GLA_TAU = 16.0
GLA_HEAD_EPS = 1e-5
GLA_SEQ_GROUP = 4

VMEM_LIMIT = 56 * 1024 * 1024
MLP_TILE = (832, 1024)
MLP_FINAL_TILE = (512, 1024)


def _cparams(*sem):
    return pltpu.CompilerParams(dimension_semantics=sem, vmem_limit_bytes=VMEM_LIMIT)


def _dot(a, b):
    return jnp.dot(a.astype(BF16), b.astype(BF16), preferred_element_type=F32)


def _dot_nt(a, b):
    return lax.dot_general(a.astype(BF16), b.astype(BF16), (((1,), (1,)), ((), ())),
                           preferred_element_type=F32)


def _dot_tn(a, b):
    return lax.dot_general(a.astype(BF16), b.astype(BF16), (((0,), (0,)), ((), ())),
                           preferred_element_type=F32)


def _split2(x):
    hi = x.astype(BF16)
    return hi, (x - hi.astype(F32)).astype(BF16)


def _dot_sel_left(sel, x):
    hi, lo = _split2(x)
    return jnp.dot(sel, hi, preferred_element_type=F32) + jnp.dot(sel, lo, preferred_element_type=F32)


def _rms(x, g, eps):
    return x * lax.rsqrt(jnp.mean(x * x, axis=-1, keepdims=True) + eps) * g


def _sigmoid(x):
    return 1.0 / (1.0 + jnp.exp(-x))


def _pick_tile(n, target):
    best = None
    for t in range(16, min(n, target) + 1, 16):
        if n % t == 0:
            best = t
    assert best is not None, (n, target)
    return best


def _side_cast_specs(sides, n_steps, step_of):
    in_specs, out_specs, out_shapes = [], [], []
    for arr, layer in sides:
        _, rows, cols = arr.shape
        blk = next(b for b in range(16, rows + 1, 16) if rows % b == 0 and rows // b <= n_steps)
        last = rows // blk - 1
        in_specs.append(pl.BlockSpec(
            (None, blk, cols), lambda *ids, layer=layer, last=last: (layer, jnp.minimum(step_of(*ids), last), 0)))
        out_specs.append(pl.BlockSpec(
            (None, blk, cols), lambda *ids, last=last: (0, jnp.minimum(step_of(*ids), last), 0)))
        out_shapes.append(jax.ShapeDtypeStruct((1, rows, cols), BF16))
    return in_specs, out_specs, out_shapes


def _mlp_kernel(*refs, final, n_side):
    h_ref, g_ref, w1_ref, w2_ref, gf_ref = refs[:5]
    side_in = refs[5:5 + n_side]
    o_ref = refs[5 + n_side]
    side_out = refs[6 + n_side:6 + 2 * n_side]
    xn_ref = refs[6 + 2 * n_side]
    ff_axis = 2 if final else 1
    if final:
        o_ref = o_ref.at[0]
    j = pl.program_id(ff_axis)

    @pl.when(j == 0)
    def _():
        x = h_ref[...]
        xn_ref[...] = _rms(x, g_ref[...], NORM_EPS).astype(BF16)
        o_ref[...] = x

    hid = jnp.dot(xn_ref[...], w1_ref[...], preferred_element_type=F32)
    hid = jnp.maximum(hid, 0.0)
    hid = hid * hid
    o_ref[...] += jnp.dot(hid.astype(BF16), w2_ref[...], preferred_element_type=F32)
    for src, dst in zip(side_in, side_out):
        dst[...] = src[...].astype(BF16)

    if final:
        @pl.when(j == pl.num_programs(ff_axis) - 1)
        def _():
            o_ref[...] = _rms(o_ref[...], gf_ref[...], NORM_EPS)


def _mlp(h, g, w1, w2, layer, gf, tm_target, tf_target, sides=()):
    m, d = h.shape
    ff = w1.shape[2]
    tm = _pick_tile(m, tm_target)
    tf = _pick_tile(ff, tf_target)
    nj = ff // tf
    side_in, side_out, side_shapes = _side_cast_specs(sides, (m // tm) * nj, lambda i, j: i * nj + j)
    outs = pl.pallas_call(
        functools.partial(_mlp_kernel, final=False, n_side=len(sides)),
        grid=(m // tm, nj),
        in_specs=[
            pl.BlockSpec((tm, d), lambda i, j: (i, 0)),
            pl.BlockSpec((1, d), lambda i, j: (0, 0)),
            pl.BlockSpec((None, d, tf), lambda i, j: (layer, 0, j)),
            pl.BlockSpec((None, tf, d), lambda i, j: (layer, j, 0)),
            pl.BlockSpec((1, d), lambda i, j: (0, 0)),
        ] + side_in,
        out_specs=[pl.BlockSpec((tm, d), lambda i, j: (i, 0))] + side_out,
        out_shape=[jax.ShapeDtypeStruct((m, d), F32)] + side_shapes,
        scratch_shapes=[pltpu.VMEM((tm, d), BF16)],
        compiler_params=_cparams("arbitrary", "arbitrary"),
        name="mlp",
    )(h, g, w1, w2, gf, *[arr for arr, _ in sides])
    return outs[0], outs[1:]


def _mlp_final(h3, g, w1, w2, layer, gf, tm_target, tf_target):
    b, tp, d = h3.shape
    seq = tp - LEAD
    ff = w1.shape[2]
    tm = _pick_tile(seq, tm_target)
    tf = _pick_tile(ff, tf_target)
    return pl.pallas_call(
        functools.partial(_mlp_kernel, final=True, n_side=0),
        grid=(b, seq // tm, ff // tf),
        in_specs=[
            pl.BlockSpec((pl.Element(tm), pl.Element(d)),
                         lambda bi, i, j: (pl.multiple_of(bi * tp + LEAD + i * tm, 16), 0)),
            pl.BlockSpec((1, d), lambda bi, i, j: (0, 0)),
            pl.BlockSpec((None, d, tf), lambda bi, i, j: (layer, 0, j)),
            pl.BlockSpec((None, tf, d), lambda bi, i, j: (layer, j, 0)),
            pl.BlockSpec((1, d), lambda bi, i, j: (0, 0)),
        ],
        out_specs=pl.BlockSpec((1, tm, d), lambda bi, i, j: (bi, i, 0)),
        out_shape=jax.ShapeDtypeStruct((b, seq, d), F32),
        scratch_shapes=[pltpu.VMEM((tm, d), BF16)],
        compiler_params=_cparams("parallel", "parallel", "arbitrary"),
        name="mlp_final",
    )(h3.reshape(b * tp, d), g, w1, w2, gf)


def _shifted_norm(h_ref, hp_ref, g_ref):
    g = g_ref[...]
    hn = _rms(h_ref[0], g, NORM_EPS)
    pn = _rms(hp_ref[0], g, NORM_EPS)[7:8]
    pn = jnp.where(pl.program_id(1) == 0, 0.0, pn)
    row = lax.broadcasted_iota(jnp.int32, hn.shape, 0)
    prev = jnp.where(row == 0, pn, pltpu.roll(hn, 1, axis=0))
    return hn, prev - hn


def _to_pairs(o_ref, lead, val):
    for p in range(val.shape[1] // LANES):
        o_ref[lead + (p,)] = val[:, p * LANES:(p + 1) * LANES].astype(o_ref.dtype)


def _shift_specs(tm, d):
    nb = tm // 8
    return [
        pl.BlockSpec((1, tm, d), lambda bi, i: (bi, i, 0)),
        pl.BlockSpec((1, 8, d), lambda bi, i: (bi, jnp.maximum(i * nb - 1, 0), 0)),
        pl.BlockSpec((1, d), lambda bi, i: (0, 0)),
        pl.BlockSpec((6, d), lambda bi, i: (0, 0)),
    ]


RKV_MIX_ROWS = (0, 2, 3)


def _rkv_kernel(*refs, n_side):
    h_ref, hp_ref, g_ref, mix_ref, w_ref = refs[:5]
    side_in = refs[5:5 + n_side]
    o_ref = refs[5 + n_side]
    side_out = refs[6 + n_side:]
    hn, xx = _shifted_norm(h_ref, hp_ref, g_ref)
    for s, mrow in enumerate(RKV_MIX_ROWS):
        x = (hn + xx * mix_ref[mrow:mrow + 1, :]).astype(BF16)
        _to_pairs(o_ref, (s,), jnp.dot(x, w_ref[s], preferred_element_type=F32))
    for src, dst in zip(side_in, side_out):
        dst[...] = src[...].astype(BF16)


def _rkv(h3, g, mix, w, layer, sides=()):
    b, tp, d = h3.shape
    tm = _pick_tile(tp, 208)
    nt = tp // tm
    npair = d // LANES
    side_in, side_out, side_shapes = _side_cast_specs(sides, b * nt, lambda bi, i: bi * nt + i)
    outs = pl.pallas_call(
        functools.partial(_rkv_kernel, n_side=len(sides)),
        grid=(b, nt),
        in_specs=_shift_specs(tm, d) + [
            pl.BlockSpec((None, 3, d, d), lambda bi, i: (layer, 0, 0, 0), pipeline_mode=pl.Buffered(1)),
        ] + side_in,
        out_specs=[pl.BlockSpec((3, npair, tm, LANES), lambda bi, i: (0, 0, bi * nt + i, 0))] + side_out,
        out_shape=[jax.ShapeDtypeStruct((3, npair, b * tp, LANES), F32)] + side_shapes,
        compiler_params=_cparams("arbitrary", "arbitrary"),
        name="rw_rkv",
    )(h3, h3, g, mix, w, *[arr for arr, _ in sides])
    return outs[0], outs[1:]


def _lora_kernel(*refs, branches):
    h_ref, hp_ref, g_ref, mix_ref = refs[:4]
    nb = len(branches)
    w_refs = refs[4:4 + 3 * nb]
    o_refs = refs[4 + 3 * nb:]
    hn, xx = _shifted_norm(h_ref, hp_ref, g_ref)
    for bi, (mrow, mid, out) in enumerate(branches):
        w1_ref, w2_ref, b_ref = w_refs[3 * bi:3 * bi + 3]
        x = (hn + xx * mix_ref[mrow:mrow + 1, :]).astype(BF16)
        z = jnp.dot(x, w1_ref[...], preferred_element_type=F32)
        if mid == "tanh":
            z = jnp.tanh(z)
        elif mid == "sigmoid":
            z = _sigmoid(z)
        y = jnp.dot(z.astype(BF16), w2_ref[...], preferred_element_type=F32)
        if out == "decay":
            y = -RW_DECAY_SCALE * _sigmoid(y + b_ref[...])
        elif out == "sigmoid":
            y = _sigmoid(y + b_ref[...])
        _to_pairs(o_refs[bi], (), y)


def _lora(h3, g, mix, branches, weights):
    b, tp, d = h3.shape
    tm = _pick_tile(tp, 416)
    nt = tp // tm
    npair = d // LANES
    w_specs, w_args = [], []
    for w1, w2, bias in weights:
        r = w1.shape[1]
        w_specs += [pl.BlockSpec((d, r), lambda bi, i: (0, 0)),
                    pl.BlockSpec((r, d), lambda bi, i: (0, 0)),
                    pl.BlockSpec((1, d), lambda bi, i: (0, 0))]
        w_args += [w1, w2, bias]
    out_spec = pl.BlockSpec((npair, tm, LANES), lambda bi, i: (0, bi * nt + i, 0))
    out_shape = jax.ShapeDtypeStruct((npair, b * tp, LANES), F32)
    return pl.pallas_call(
        functools.partial(_lora_kernel, branches=branches),
        grid=(b, nt),
        in_specs=_shift_specs(tm, d) + w_specs,
        out_specs=[out_spec] * len(branches),
        out_shape=[out_shape] * len(branches),
        compiler_params=_cparams("parallel", "parallel"),
        name="rw_lora",
    )(h3, h3, g, mix, *w_args)


def _pad_rank(w1, w2):
    r = w1.shape[1]
    rp = -(-r // LANES) * LANES
    return (jnp.pad(w1, ((0, 0), (0, rp - r))).astype(BF16),
            jnp.pad(w2, ((0, rp - r), (0, 0))).astype(BF16))


def _stack_heads(x, lane_lo):
    return jnp.concatenate([jnp.where(lane_lo, x, 0.0), jnp.where(lane_lo, 0.0, x)], axis=0)


def _rw_scan_kernel(*refs, has_vres):
    if has_vres:
        (rkv_ref, lw_ref, ag_ref, gg_ref, vf_ref, vg_ref,
         kk_ref, ka_ref, rk_ref, lnw_ref, lnb_ref, z_ref, st_ref) = refs
    else:
        (rkv_ref, lw_ref, ag_ref, gg_ref,
         kk_ref, ka_ref, rk_ref, lnw_ref, lnb_ref, z_ref, st_ref) = refs
    c = pl.program_id(1)
    npair = z_ref.shape[0]
    group = min(PAIR_GROUP, npair)
    assert npair % group == 0
    two_l = 2 * CHUNK

    @pl.when(c == 0)
    def _():
        st_ref[...] = jnp.zeros_like(st_ref)

    row = lax.broadcasted_iota(jnp.int32, (two_l, two_l), 0)
    col = lax.broadcasted_iota(jnp.int32, (two_l, two_l), 1)
    t_row = row % CHUNK
    t_col = col % CHUNK
    t_half = lax.broadcasted_iota(jnp.int32, (CHUNK, two_l), 0)
    s_half = lax.broadcasted_iota(jnp.int32, (CHUNK, two_l), 1) % CHUNK
    strict_t = t_half > s_half
    incl_t = t_half >= s_half
    same_head = (row // CHUNK) == (col // CHUNK)
    tril = jnp.where(lax.broadcasted_iota(jnp.int32, (CHUNK, CHUNK), 0)
                     >= lax.broadcasted_iota(jnp.int32, (CHUNK, CHUNK), 1), 1.0, 0.0).astype(BF16)
    diag = row == col
    lane_lo = lax.broadcasted_iota(jnp.int32, (CHUNK, LANES), 1) < RW_HEAD
    live = jnp.logical_or(c > 0, lax.broadcasted_iota(jnp.int32, (CHUNK, LANES), 0) >= N_DUMMY)

    def each(fn, *lists):
        return [fn(*xs) for xs in zip(*lists)]

    ones_bd = jnp.where(same_head, 1.0, 0.0).astype(BF16)

    def head_sum_lanes(x):
        lo = jnp.sum(jnp.where(lane_lo, x, 0.0), axis=-1, keepdims=True)
        hi = jnp.sum(jnp.where(lane_lo, 0.0, x), axis=-1, keepdims=True)
        return jnp.where(lane_lo, lo, hi)

    def head_sum_matmul(x):
        hi, lo = _split2(x)
        both = jnp.dot(jnp.concatenate([hi, lo], axis=0), ones_bd, preferred_element_type=F32)
        return both[:x.shape[0]] + both[x.shape[0]:]

    def swap_halves(x):
        return pltpu.roll(x, RW_HEAD, axis=1)

    def group_body(gi, carry):
        ps = [gi * group + i for i in range(group)]
        r = [rkv_ref[0, p] for p in ps]
        k0 = [rkv_ref[1, p] for p in ps]
        v = [rkv_ref[2, p] for p in ps]
        lw = [lw_ref[p] for p in ps]
        ag = [ag_ref[p] for p in ps]
        if has_vres:
            v = [vi + (vf_ref[0, p] - vi) * vg_ref[p] for vi, p in zip(v, ps)]
        kk = [ki * kk_ref[p] for ki, p in zip(k0, ps)]
        k = [ki * (1.0 + (ai - 1.0) * ka_ref[p]) for ki, ai, p in zip(k0, ag, ps)]
        kk = each(lambda kki: kki / jnp.maximum(jnp.sqrt(head_sum_lanes(kki * kki)), 1e-12), kk)
        bonus = [head_sum_lanes(ri * ki * rk_ref[p]) * vi for ri, ki, vi, p in zip(r, k, v, ps)]

        cs = each(lambda lwi: _dot_sel_left(tril, lwi), lw)
        e_pos = each(jnp.exp, cs)
        e_neg = each(lambda ci: jnp.exp(-ci), cs)
        w_all = each(lambda ei: ei[CHUNK - 1:CHUNK, :], e_pos)
        a_t = each(lambda kki, ci, lwi: -kki * jnp.exp(ci - lwi), kk, cs, lw)
        r_t = each(lambda ri, ei: ri * ei, r, e_pos)
        a_s = each(lambda ati: _stack_heads(ati, lane_lo), a_t)
        r_s = each(lambda rti: _stack_heads(rti, lane_lo), r_t)
        b_t = each(lambda kki, ai, ei: kki * ai * ei, kk, ag, e_neg)
        k_t = each(lambda ki, ei: ki * ei, k, e_neg)
        v_s = each(lambda vi: _stack_heads(vi, lane_lo).astype(BF16), v)
        bh_t = each(lambda bti, wi: _stack_heads(bti * wi, lane_lo).T.astype(BF16), b_t, w_all)
        kh_t = each(lambda kti, wi: _stack_heads(kti * wi, lane_lo).T.astype(BF16), k_t, w_all)

        gram = each(lambda ati, rti, bti, kti: _dot_nt(
            jnp.concatenate([ati, rti], axis=0),
            jnp.concatenate([_stack_heads(bti, lane_lo), _stack_heads(kti, lane_lo)], axis=0)),
            a_t, r_t, b_t, k_t)
        block = lambda gm, keep, rows, cols: _stack_heads(
            jnp.where(keep, gm[rows * CHUNK:(rows + 1) * CHUNK, cols * two_l:(cols + 1) * two_l], 0.0),
            lane_lo).astype(BF16)
        pw = each(lambda gm: block(gm, strict_t, 0, 0), gram)
        a_ak = each(lambda gm: block(gm, strict_t, 0, 1), gram)
        a_rb = each(lambda gm: block(gm, incl_t, 1, 0), gram)
        a_rk = each(lambda gm: block(gm, incl_t, 1, 1), gram)

        xv = each(lambda aki, ari, khi, vsi: jnp.dot(jnp.concatenate([aki, ari, khi], axis=0), vsi,
                                                     preferred_element_type=F32), a_ak, a_rk, kh_t, v_s)

        z = each(lambda asi, xi: asi + swap_halves(xi[:two_l]), a_s, xv)
        n_sq = CHUNK.bit_length() - 1
        for s in range(n_sq):
            if s + 1 < n_sq:
                prod = each(lambda pi, zi: jnp.dot(pi, jnp.concatenate([zi.astype(BF16), pi], axis=1),
                                                   preferred_element_type=F32), pw, z)
                z = each(lambda zi, pr: zi + pr[:, :LANES], z, prod)
                pw = each(lambda pr: pr[:, LANES:].astype(BF16), prod)
            else:
                z = each(lambda zi, pi: zi + jnp.dot(pi, zi.astype(BF16), preferred_element_type=F32), z, pw)
        zb = each(lambda zi: zi.astype(BF16), z)

        rb = each(lambda ai, bhi, zi: jnp.dot(jnp.concatenate([ai, bhi], axis=0), zi,
                                              preferred_element_type=F32), a_rb, bh_t, zb)
        r_hat = each(lambda rsi, rbi: rsi + jnp.where(same_head, rbi[:two_l], 0.0), r_s, rb)
        y_hat = each(lambda rbi, xi: swap_halves(jnp.where(same_head, 0.0, rbi[:two_l])) + xi[two_l:2 * two_l],
                     rb, xv)
        m_mat = each(lambda wi, rbi: jnp.where(diag, wi, 0.0) + jnp.where(same_head, rbi[two_l:], 0.0),
                     w_all, rb)
        n_mat = each(lambda rbi, xi: swap_halves(jnp.where(same_head, 0.0, rbi[two_l:])) + xi[2 * two_l:],
                     rb, xv)

        hb = [st_ref[p].astype(BF16) for p in ps]
        ys = each(lambda rh, mm, hi: jnp.dot(jnp.concatenate([rh, mm], axis=0).astype(BF16), hi,
                                             preferred_element_type=F32), r_hat, m_mat, hb)
        for p, ysi, nm in zip(ps, ys, n_mat):
            st_ref[p] = ysi[two_l:] + nm
        y2 = each(lambda ysi, yh: ysi[:two_l] + yh, ys, y_hat)
        y = each(lambda yi: yi[:CHUNK] + yi[CHUNK:], y2)

        dy = each(lambda yi: yi - head_sum_matmul(yi) * (1.0 / RW_HEAD), y)
        var = each(lambda di: head_sum_matmul(di * di) * (1.0 / RW_HEAD), dy)
        for p, di, vi, bi in zip(ps, dy, var, bonus):
            yn = di * lax.rsqrt(vi + RW_GN_EPS) * lnw_ref[p] + lnb_ref[p]
            z_ref[p] = jnp.where(live, (yn + bi) * gg_ref[p], 0.0).astype(z_ref.dtype)
        return carry

    lax.fori_loop(0, npair // group, group_body, 0)


def _rw_scan(rkv, lw, ag, gg, vres, kk, ka, rk, lnw, lnb, batch):
    _, npair, m, _ = rkv.shape
    nc = m // batch // CHUNK
    row_map3 = lambda bi, c: (0, bi * nc + c, 0)
    row_map4 = lambda bi, c: (0, 0, bi * nc + c, 0)
    par = pl.BlockSpec((npair, 1, LANES), lambda bi, c: (0, 0, 0))
    tile = pl.BlockSpec((npair, CHUNK, LANES), row_map3)
    in_specs = [pl.BlockSpec((3, npair, CHUNK, LANES), row_map4), tile, tile, tile]
    args = [rkv, lw, ag, gg]
    if vres is not None:
        vfirst, vgate = vres
        in_specs += [pl.BlockSpec((1, npair, CHUNK, LANES), lambda bi, c: (2, 0, bi * nc + c, 0)), tile]
        args += [vfirst, vgate]
    in_specs += [par] * 5
    args += [kk, ka, rk, lnw, lnb]
    return pl.pallas_call(
        functools.partial(_rw_scan_kernel, has_vres=vres is not None),
        grid=(batch, nc),
        in_specs=in_specs,
        out_specs=tile,
        out_shape=jax.ShapeDtypeStruct((npair, m, LANES), BF16),
        scratch_shapes=[pltpu.VMEM((npair, 2 * CHUNK, LANES), F32)],
        compiler_params=_cparams("parallel", "arbitrary"),
        name="rw_scan",
    )(*args)


def _proj_kernel(x_ref, w_ref, h_ref, o_ref, *, pair_major):
    if pair_major:
        x = jnp.concatenate([x_ref[p] for p in range(x_ref.shape[0])], axis=1)
    else:
        x = x_ref[...]
    o_ref[...] = h_ref[...] + jnp.dot(x, w_ref[...], preferred_element_type=F32)


def _proj_residual(x, w, layer, h, pair_major):
    m, d = h.shape
    kdim = w.shape[1]
    tm = _pick_tile(m, 640)
    if pair_major:
        x_spec = pl.BlockSpec((x.shape[0], tm, LANES), lambda i: (0, i, 0))
    else:
        x_spec = pl.BlockSpec((tm, kdim), lambda i: (i, 0))
    return pl.pallas_call(
        functools.partial(_proj_kernel, pair_major=pair_major),
        grid=(m // tm,),
        in_specs=[x_spec,
                  pl.BlockSpec((None, kdim, d), lambda i: (layer, 0, 0)),
                  pl.BlockSpec((tm, d), lambda i: (i, 0))],
        out_specs=pl.BlockSpec((tm, d), lambda i: (i, 0)),
        out_shape=jax.ShapeDtypeStruct((m, d), F32),
        compiler_params=_cparams("parallel"),
        name="proj_residual",
    )(x, w, h)


def _gla_in_kernel(h_ref, g_ref, w_ref, wz_ref, wa_ref, ba_ref, p_ref, gl_ref, xn_ref):
    @pl.when(pl.program_id(1) == 0)
    def _():
        xn = _rms(h_ref[...], g_ref[...], NORM_EPS).astype(BF16)
        xn_ref[...] = xn
        za = jnp.dot(xn, wz_ref[...], preferred_element_type=F32)
        u = _dot(za, wa_ref[...]) + ba_ref[...]
        gl_ref[...] = (jnp.minimum(u, 0.0) - jnp.log1p(jnp.exp(-jnp.abs(u)))) * (1.0 / GLA_TAU)

    p_ref[...] = _dot_nt(xn_ref[...], w_ref[...]).astype(p_ref.dtype)


def _gla_in(h, g, w, layer, n, wz, wa, ba):
    m, d = h.shape
    dk = wa.shape[1]
    tm = _pick_tile(m, 640)
    tn = _pick_tile(n, 2048)
    return pl.pallas_call(
        _gla_in_kernel,
        grid=(m // tm, n // tn),
        in_specs=[
            pl.BlockSpec((tm, d), lambda i, j: (i, 0)),
            pl.BlockSpec((1, d), lambda i, j: (0, 0)),
            pl.BlockSpec((None, tn, d), lambda i, j: (layer, j, 0)),
            pl.BlockSpec((d, LANES), lambda i, j: (0, 0)),
            pl.BlockSpec((LANES, dk), lambda i, j: (0, 0)),
            pl.BlockSpec((1, dk), lambda i, j: (0, 0)),
        ],
        out_specs=[pl.BlockSpec((tm, tn), lambda i, j: (i, j)),
                   pl.BlockSpec((tm, dk), lambda i, j: (i, 0))],
        out_shape=[jax.ShapeDtypeStruct((m, n), BF16), jax.ShapeDtypeStruct((m, dk), F32)],
        scratch_shapes=[pltpu.VMEM((tm, d), BF16)],
        compiler_params=_cparams("parallel", "arbitrary"),
        name="gla_in",
    )(h, g, w, wz, wa, ba)


def _gla_chunk_kernel(q_ref, k_ref, v_ref, gate_ref, gl_ref, gn_ref, z_ref, st_ref, *, scale):
    c = pl.program_id(1)
    nb, nh, hv, hk = st_ref.shape

    @pl.when(c == 0)
    def _():
        st_ref[...] = jnp.zeros_like(st_ref)

    row = lax.broadcasted_iota(jnp.int32, (CHUNK, CHUNK), 0)
    col = lax.broadcasted_iota(jnp.int32, (CHUNK, CHUNK), 1)
    causal = row >= col
    tril = jnp.where(causal, 1.0, 0.0).astype(BF16)
    live = jnp.logical_or(c > 0, lax.broadcasted_iota(jnp.int32, gl_ref.shape[1:], 0) >= N_DUMMY)
    ksl = lambda x, h: x[:, h * hk:(h + 1) * hk]
    vsl = lambda x, h: x[:, h * hv:(h + 1) * hv]
    seqs = range(nb)
    cells = [(b, h) for b in seqs for h in range(nh)]

    gl = [jnp.where(live, gl_ref[b], 0.0) for b in seqs]
    bc = [_dot_sel_left(tril, gl[b]) for b in seqs]
    b_last = [bc[b][CHUNK - 1:CHUNK, :] for b in seqs]
    e_last = [jnp.exp(b_last[b]) for b in seqs]
    q_t = [(q_ref[b] * scale * jnp.exp(bc[b])).astype(BF16) for b in seqs]
    k_t = [(k_ref[b] * jnp.exp(-bc[b])).astype(BF16) for b in seqs]
    k_h = [(k_ref[b] * jnp.exp(b_last[b] - bc[b])).astype(BF16) for b in seqs]
    v = {(b, h): vsl(v_ref[b], h).astype(BF16) for b, h in cells}
    att = {(b, h): jnp.where(causal, _dot_nt(ksl(q_t[b], h), ksl(k_t[b], h)), 0.0).astype(BF16) for b, h in cells}
    st = {(b, h): st_ref[b, h] for b, h in cells}
    o = {(b, h): _dot_nt(ksl(q_t[b], h), st[b, h]) + jnp.dot(att[b, h], v[b, h], preferred_element_type=F32)
         for b, h in cells}
    for b, h in cells:
        st_ref[b, h] = st[b, h] * ksl(e_last[b], h) + _dot_tn(v[b, h], ksl(k_h[b], h))
    for b, h in cells:
        on = o[b, h] * lax.rsqrt(jnp.mean(o[b, h] * o[b, h], axis=-1, keepdims=True) + GLA_HEAD_EPS)
        gate = vsl(gate_ref[b], h).astype(F32)
        z_ref[b, :, h * hv:(h + 1) * hv] = (on * vsl(gn_ref[...], h) * (gate * _sigmoid(gate))).astype(z_ref.dtype)


def _gla_chunk(p, glog, gn_w, batch, d):
    tp = p.shape[0] // batch
    nc = tp // CHUNK
    dk = d // 2
    hk = dk // GLA_HEADS
    hv = d // GLA_HEADS
    p3 = p.reshape(batch, tp, p.shape[1])
    nb = GLA_SEQ_GROUP if batch % GLA_SEQ_GROUP == 0 else 1
    z = pl.pallas_call(
        functools.partial(_gla_chunk_kernel, scale=hk ** -0.5),
        grid=(batch // nb, nc),
        in_specs=[
            pl.BlockSpec((nb, CHUNK, dk), lambda g, c: (g, c, 0)),
            pl.BlockSpec((nb, CHUNK, dk), lambda g, c: (g, c, 1)),
            pl.BlockSpec((nb, CHUNK, d), lambda g, c: (g, c, 1)),
            pl.BlockSpec((nb, CHUNK, d), lambda g, c: (g, c, 2)),
            pl.BlockSpec((nb, CHUNK, dk), lambda g, c: (g, c, 0)),
            pl.BlockSpec((1, d), lambda g, c: (0, 0)),
        ],
        out_specs=pl.BlockSpec((nb, CHUNK, d), lambda g, c: (g, c, 0)),
        out_shape=jax.ShapeDtypeStruct((batch, tp, d), BF16),
        scratch_shapes=[pltpu.VMEM((nb, GLA_HEADS, hv, hk), F32)],
        compiler_params=_cparams("parallel", "arbitrary"),
        name="gla_chunk",
    )(p3, p3, p3, p3, glog.reshape(batch, tp, dk), gn_w)
    return z.reshape(batch * tp, d)


def _rwkv_layer(h, batch, j, v_first, norm_g, rw_mix, rw_w_rkv, rw_w0, rw_w1, rw_w2, rw_a0, rw_a1, rw_a2,
                rw_v0, rw_v1, rw_v2, rw_g1, rw_g2, rw_k_k, rw_k_a, rw_r_k, rw_ln_w, rw_ln_b, rw_w_o,
                sides=()):
    m, d = h.shape
    npair = d // LANES
    row = lambda t: t.reshape(1, d)
    pairs = lambda t: t.reshape(npair, 1, LANES)
    h3 = h.reshape(batch, m // batch, d)
    g = row(norm_g)
    rkv, cast = _rkv(h3, g, rw_mix[j], rw_w_rkv, 0, sides)
    if rw_w_o is None:
        rw_w_o = cast[0]
    branches = [(1, "tanh", "decay"), (4, "none", "sigmoid"), (5, "sigmoid", "none")]
    weights = [_pad_rank(rw_w1[j], rw_w2[j]) + (row(rw_w0[j]),),
               _pad_rank(rw_a1[j], rw_a2[j]) + (row(rw_a0[j]),),
               _pad_rank(rw_g1[j], rw_g2[j]) + (jnp.zeros((1, d), F32),)]
    if j > 0:
        branches.append((3, "none", "sigmoid"))
        weights.append(_pad_rank(rw_v1[j - 1], rw_v2[j - 1]) + (row(rw_v0[j - 1]),))
    outs = _lora(h3, g, rw_mix[j], tuple(branches), weights)
    lw, ag, gg = outs[:3]
    vres = (v_first, outs[3]) if j > 0 else None
    z = _rw_scan(rkv, lw, ag, gg, vres, pairs(rw_k_k[j]), pairs(rw_k_a[j]), pairs(rw_r_k[j]),
                 pairs(rw_ln_w[j]), pairs(rw_ln_b[j]), batch)
    h = _proj_residual(z, rw_w_o, 0, h, pair_major=True)
    return h, rkv, cast


def _gla_layer(h, batch, j, norm_g, gla_w_in_bf, gla_w_a2, gla_b_a, gla_gn_w, gla_w_o):
    m, d = h.shape
    dk = d // 2
    n_main = 2 * dk + 2 * d
    rank = gla_w_in_bf.shape[1] - n_main
    g = norm_g.reshape(1, d)
    wz = jnp.pad(gla_w_in_bf[0, n_main:, :].T, ((0, 0), (0, LANES - rank)))
    wa = jnp.pad(gla_w_a2[j], ((0, LANES - rank), (0, 0)))
    p, glog = _gla_in(h, g, gla_w_in_bf, 0, n_main, wz, wa, gla_b_a[j].reshape(1, dk))
    z = _gla_chunk(p, glog, gla_gn_w[j].reshape(1, d), batch, d)
    return _proj_residual(z, gla_w_o, 0, h, pair_major=False)


def kernel(x, meta, norm_mix, norm_mlp, norm_f, mlp_w1, mlp_w2, rw_mix, rw_w_rkv, rw_w0, rw_w1, rw_w2, rw_a0, rw_a1, rw_a2, rw_v0, rw_v1, rw_v2, rw_g1, rw_g2, rw_k_k, rw_k_a, rw_r_k, rw_ln_w, rw_ln_b, rw_w_o, gla_w_in, gla_w_a2, gla_b_a, gla_gn_w, gla_w_o):
    batch, seq, d = x.shape
    depth = norm_mix.shape[0]
    assert seq % CHUNK == 0 and d % (2 * LANES) == 0 and meta.shape[0] == N_META
    tp = LEAD + seq
    lead = jnp.concatenate([jnp.zeros((N_DUMMY, d), x.dtype), meta.astype(x.dtype)], axis=0)
    h = jnp.concatenate([jnp.broadcast_to(lead[None], (batch, LEAD, d)), x], axis=1).reshape(batch * tp, d)
    gf = norm_f.reshape(1, d)
    rw_w_rkv2 = rw_w_rkv.reshape(rw_w_rkv.shape[0], 3 * d, d)
    gla_w_in_t = jnp.transpose(gla_w_in, (0, 2, 1))

    def mixer_weights(i):
        j = i // 2
        mixer = [(rw_w_rkv2, j), (rw_w_o, j)] if i % 2 == 0 else [(gla_w_in_t, j), (gla_w_o, j)]
        return mixer + [(mlp_w1, i), (mlp_w2, i)]

    w_first, first_layer = mixer_weights(0)[0]
    cast = [w_first[first_layer:first_layer + 1].astype(BF16), None, None, None]
    v_first = None
    for i in range(depth):
        j = i // 2
        w_a, w_b, w1_bf, w2_bf = cast
        if i % 2 == 0:
            h, rkv, early = _rwkv_layer(h, batch, j, v_first, norm_mix[i], rw_mix, w_a.reshape(1, 3, d, d), rw_w0,
                                        rw_w1, rw_w2, rw_a0, rw_a1, rw_a2, rw_v0, rw_v1, rw_v2, rw_g1, rw_g2,
                                        rw_k_k, rw_k_a, rw_r_k.reshape(rw_r_k.shape[0], d), rw_ln_w, rw_ln_b, w_b,
                                        sides=mixer_weights(0)[1:] if i == 0 else ())
            if i == 0:
                _, w1_bf, w2_bf = early
            if j == 0:
                v_first = rkv
        else:
            h = _gla_layer(h, batch, j, norm_mix[i], w_a, gla_w_a2, gla_b_a, gla_gn_w, w_b)
        g_mlp = norm_mlp[i].reshape(1, d)
        if i == depth - 1:
            return _mlp_final(h.reshape(batch, tp, d), g_mlp, w1_bf, w2_bf, 0, gf, *MLP_FINAL_TILE)
        h, cast = _mlp(h, g_mlp, w1_bf, w2_bf, 0, gf, *MLP_TILE, sides=mixer_weights(i + 1))
```

```python
import functools
import math

import jax
import jax.numpy as jnp
from jax import lax
from jax.experimental import pallas as pl
from jax.experimental.pallas import tpu as pltpu

F32 = jnp.float32
BF16 = jnp.bfloat16

N_META = 16
CHUNK = 64
LEAD = CHUNK
N_DUMMY = LEAD - N_META
NORM_EPS = 1e-6

RW_HEAD = 64
LANES = 128
RW_GN_EPS = 64e-5
RW_DECAY_SCALE = 0.6065306597126334
PAIR_GROUP = 16

GLA_HEADS = 4
GLA_TAU = 16.0
GLA_HEAD_EPS = 1e-5
GLA_SEQ_GROUP = 4

V7X_VMEM_BYTES = 64 * 1024 * 1024
VMEM_REQUEST_CAP = V7X_VMEM_BYTES - 8 * 1024 * 1024

MLP_TILE = (832, 1024)
MLP_FINAL_TILE = (512, 1024)
RKV_ROWS = 208
LORA_ROWS = 416
PROJ_ROWS = 640
GLA_IN_TILE = (640, 2048)


def _nbytes(shape, dtype):
    return math.prod(shape) * jnp.dtype(dtype).itemsize


def _cparams(sem, moving, fixed=(), live=0):
    need = 2 * sum(moving) + sum(fixed) + live
    return pltpu.CompilerParams(dimension_semantics=sem, vmem_limit_bytes=min(need, VMEM_REQUEST_CAP))


def _dot(a, b):
    return jnp.dot(a.astype(BF16), b.astype(BF16), preferred_element_type=F32)


def _dot_nt(a, b):
    return lax.dot_general(a.astype(BF16), b.astype(BF16), (((1,), (1,)), ((), ())),
                           preferred_element_type=F32)


def _dot_tn(a, b):
    return lax.dot_general(a.astype(BF16), b.astype(BF16), (((0,), (0,)), ((), ())),
                           preferred_element_type=F32)


def _split2(x):
    hi = x.astype(BF16)
    return hi, (x - hi.astype(F32)).astype(BF16)


def _dot_sel_left(sel, x):
    hi, lo = _split2(x)
    return jnp.dot(sel, hi, preferred_element_type=F32) + jnp.dot(sel, lo, preferred_element_type=F32)


def _rms(x, g, eps):
    return x * lax.rsqrt(jnp.mean(x * x, axis=-1, keepdims=True) + eps) * g


def _sigmoid(x):
    return 1.0 / (1.0 + jnp.exp(-x))


def _pick_tile(n, target):
    best = None
    for t in range(16, min(n, target) + 1, 16):
        if n % t == 0:
            best = t
    assert best is not None, (n, target)
    return best


def _side_cast_specs(sides, n_steps, step_of):
    in_specs, out_specs, out_shapes, moving = [], [], [], []
    for arr, layer in sides:
        _, rows, cols = arr.shape
        blk = next(b for b in range(16, rows + 1, 16) if rows % b == 0 and rows // b <= n_steps)
        last = rows // blk - 1
        in_specs.append(pl.BlockSpec(
            (None, blk, cols), lambda *ids, layer=layer, last=last: (layer, jnp.minimum(step_of(*ids), last), 0)))
        out_specs.append(pl.BlockSpec(
            (None, blk, cols), lambda *ids, last=last: (0, jnp.minimum(step_of(*ids), last), 0)))
        out_shapes.append(jax.ShapeDtypeStruct((1, rows, cols), BF16))
        moving += [_nbytes((blk, cols), F32), _nbytes((blk, cols), BF16)]
    return in_specs, out_specs, out_shapes, moving


def _mlp_kernel(*refs, final, n_side):
    h_ref, g_ref, w1_ref, w2_ref, gf_ref = refs[:5]
    side_in = refs[5:5 + n_side]
    o_ref = refs[5 + n_side]
    side_out = refs[6 + n_side:6 + 2 * n_side]
    xn_ref = refs[6 + 2 * n_side]
    ff_axis = 2 if final else 1
    if final:
        o_ref = o_ref.at[0]
    j = pl.program_id(ff_axis)

    @pl.when(j == 0)
    def _():
        x = h_ref[...]
        xn_ref[...] = _rms(x, g_ref[...], NORM_EPS).astype(BF16)
        o_ref[...] = x

    hid = jnp.dot(xn_ref[...], w1_ref[...], preferred_element_type=F32)
    hid = jnp.maximum(hid, 0.0)
    hid = hid * hid
    o_ref[...] += jnp.dot(hid.astype(BF16), w2_ref[...], preferred_element_type=F32)
    for src, dst in zip(side_in, side_out):
        dst[...] = src[...].astype(BF16)

    if final:
        @pl.when(j == pl.num_programs(ff_axis) - 1)
        def _():
            o_ref[...] = _rms(o_ref[...], gf_ref[...], NORM_EPS)


def _mlp(h, g, w1, w2, gf, tm_target, tf_target, sides=()):
    m, d = h.shape
    ff = w1.shape[2]
    tm = _pick_tile(m, tm_target)
    tf = _pick_tile(ff, tf_target)
    nj = ff // tf
    side_in, side_out, side_shapes, side_bytes = _side_cast_specs(
        sides, (m // tm) * nj, lambda i, j: i * nj + j)
    vmem = dict(moving=[2 * _nbytes((tm, d), F32), 2 * _nbytes((d, tf), BF16)] + side_bytes,
                fixed=[_nbytes((tm, d), BF16)], live=_nbytes((tm, tf), F32) + _nbytes((tm, tf), BF16))
    outs = pl.pallas_call(
        functools.partial(_mlp_kernel, final=False, n_side=len(sides)),
        grid=(m // tm, nj),
        in_specs=[
            pl.BlockSpec((tm, d), lambda i, j: (i, 0)),
            pl.BlockSpec((1, d), lambda i, j: (0, 0)),
            pl.BlockSpec((None, d, tf), lambda i, j: (0, 0, j)),
            pl.BlockSpec((None, tf, d), lambda i, j: (0, j, 0)),
            pl.BlockSpec((1, d), lambda i, j: (0, 0)),
        ] + side_in,
        out_specs=[pl.BlockSpec((tm, d), lambda i, j: (i, 0))] + side_out,
        out_shape=[jax.ShapeDtypeStruct((m, d), F32)] + side_shapes,
        scratch_shapes=[pltpu.VMEM((tm, d), BF16)],
        compiler_params=_cparams(("arbitrary", "arbitrary"), **vmem),
        name="mlp",
    )(h, g, w1, w2, gf, *[arr for arr, _ in sides])
    return outs[0], outs[1:]


def _mlp_final(h3, g, w1, w2, gf, tm_target, tf_target):
    b, tp, d = h3.shape
    seq = tp - LEAD
    ff = w1.shape[2]
    tm = _pick_tile(seq, tm_target)
    tf = _pick_tile(ff, tf_target)
    vmem = dict(moving=[2 * _nbytes((tm, d), F32), 2 * _nbytes((d, tf), BF16)],
                fixed=[_nbytes((tm, d), BF16)], live=_nbytes((tm, tf), F32) + _nbytes((tm, tf), BF16))
    return pl.pallas_call(
        functools.partial(_mlp_kernel, final=True, n_side=0),
        grid=(b, seq // tm, ff // tf),
        in_specs=[
            pl.BlockSpec((pl.Element(tm), pl.Element(d)),
                         lambda bi, i, j: (pl.multiple_of(bi * tp + LEAD + i * tm, 16), 0)),
            pl.BlockSpec((1, d), lambda bi, i, j: (0, 0)),
            pl.BlockSpec((None, d, tf), lambda bi, i, j: (0, 0, j)),
            pl.BlockSpec((None, tf, d), lambda bi, i, j: (0, j, 0)),
            pl.BlockSpec((1, d), lambda bi, i, j: (0, 0)),
        ],
        out_specs=pl.BlockSpec((1, tm, d), lambda bi, i, j: (bi, i, 0)),
        out_shape=jax.ShapeDtypeStruct((b, seq, d), F32),
        scratch_shapes=[pltpu.VMEM((tm, d), BF16)],
        compiler_params=_cparams(("parallel", "parallel", "arbitrary"), **vmem),
        name="mlp_final",
    )(h3.reshape(b * tp, d), g, w1, w2, gf)


def _shifted_norm(h_ref, hp_ref, g_ref):
    g = g_ref[...]
    hn = _rms(h_ref[0], g, NORM_EPS)
    pn = _rms(hp_ref[0], g, NORM_EPS)[7:8]
    pn = jnp.where(pl.program_id(1) == 0, 0.0, pn)
    row = lax.broadcasted_iota(jnp.int32, hn.shape, 0)
    prev = jnp.where(row == 0, pn, pltpu.roll(hn, 1, axis=0))
    return hn, prev - hn


def _to_pairs(o_ref, lead, val):
    for p in range(val.shape[1] // LANES):
        o_ref[lead + (p,)] = val[:, p * LANES:(p + 1) * LANES].astype(o_ref.dtype)


def _shift_specs(tm, d):
    nb = tm // 8
    return [
        pl.BlockSpec((1, tm, d), lambda bi, i: (bi, i, 0)),
        pl.BlockSpec((1, 8, d), lambda bi, i: (bi, jnp.maximum(i * nb - 1, 0), 0)),
        pl.BlockSpec((1, d), lambda bi, i: (0, 0)),
        pl.BlockSpec((6, d), lambda bi, i: (0, 0)),
    ]


RKV_MIX_ROWS = (0, 2, 3)


def _rkv_kernel(*refs, n_side):
    h_ref, hp_ref, g_ref, mix_ref, w_ref = refs[:5]
    side_in = refs[5:5 + n_side]
    o_ref = refs[5 + n_side]
    side_out = refs[6 + n_side:]
    hn, xx = _shifted_norm(h_ref, hp_ref, g_ref)
    for s, mrow in enumerate(RKV_MIX_ROWS):
        x = (hn + xx * mix_ref[mrow:mrow + 1, :]).astype(BF16)
        _to_pairs(o_ref, (s,), jnp.dot(x, w_ref[s], preferred_element_type=F32))
    for src, dst in zip(side_in, side_out):
        dst[...] = src[...].astype(BF16)


def _rkv(h3, g, mix, w, sides=()):
    b, tp, d = h3.shape
    tm = _pick_tile(tp, RKV_ROWS)
    nt = tp // tm
    npair = d // LANES
    side_in, side_out, side_shapes, side_bytes = _side_cast_specs(sides, b * nt, lambda bi, i: bi * nt + i)
    vmem = dict(moving=[_nbytes((tm + 8, d), F32), 3 * _nbytes((tm, d), F32)] + side_bytes,
                fixed=[_nbytes((3, d, d), BF16)], live=4 * _nbytes((tm, d), F32))
    outs = pl.pallas_call(
        functools.partial(_rkv_kernel, n_side=len(sides)),
        grid=(b, nt),
        in_specs=_shift_specs(tm, d) + [
            pl.BlockSpec((None, 3, d, d), lambda bi, i: (0, 0, 0, 0), pipeline_mode=pl.Buffered(1)),
        ] + side_in,
        out_specs=[pl.BlockSpec((3, npair, tm, LANES), lambda bi, i: (0, 0, bi * nt + i, 0))] + side_out,
        out_shape=[jax.ShapeDtypeStruct((3, npair, b * tp, LANES), F32)] + side_shapes,
        compiler_params=_cparams(("arbitrary", "arbitrary"), **vmem),
        name="rw_rkv",
    )(h3, h3, g, mix, w, *[arr for arr, _ in sides])
    return outs[0], outs[1:]


def _lora_kernel(*refs, branches):
    h_ref, hp_ref, g_ref, mix_ref = refs[:4]
    nb = len(branches)
    w_refs = refs[4:4 + 3 * nb]
    o_refs = refs[4 + 3 * nb:]
    hn, xx = _shifted_norm(h_ref, hp_ref, g_ref)
    for bi, (mrow, mid, out) in enumerate(branches):
        w1_ref, w2_ref, b_ref = w_refs[3 * bi:3 * bi + 3]
        x = (hn + xx * mix_ref[mrow:mrow + 1, :]).astype(BF16)
        z = jnp.dot(x, w1_ref[...], preferred_element_type=F32)
        if mid == "tanh":
            z = jnp.tanh(z)
        elif mid == "sigmoid":
            z = _sigmoid(z)
        y = jnp.dot(z.astype(BF16), w2_ref[...], preferred_element_type=F32)
        if out == "decay":
            y = -RW_DECAY_SCALE * _sigmoid(y + b_ref[...])
        elif out == "sigmoid":
            y = _sigmoid(y + b_ref[...])
        _to_pairs(o_refs[bi], (), y)


def _lora(h3, g, mix, branches, weights):
    b, tp, d = h3.shape
    tm = _pick_tile(tp, LORA_ROWS)
    nt = tp // tm
    npair = d // LANES
    w_specs, w_args = [], []
    for w1, w2, bias in weights:
        r = w1.shape[1]
        w_specs += [pl.BlockSpec((d, r), lambda bi, i: (0, 0)),
                    pl.BlockSpec((r, d), lambda bi, i: (0, 0)),
                    pl.BlockSpec((1, d), lambda bi, i: (0, 0))]
        w_args += [w1, w2, bias]
    out_spec = pl.BlockSpec((npair, tm, LANES), lambda bi, i: (0, bi * nt + i, 0))
    out_shape = jax.ShapeDtypeStruct((npair, b * tp, LANES), F32)
    vmem = dict(moving=[_nbytes((tm + 8, d), F32)] + [_nbytes((tm, d), F32)] * len(branches),
                fixed=[_nbytes(w.shape, w.dtype) for w in w_args], live=4 * _nbytes((tm, d), F32))
    return pl.pallas_call(
        functools.partial(_lora_kernel, branches=branches),
        grid=(b, nt),
        in_specs=_shift_specs(tm, d) + w_specs,
        out_specs=[out_spec] * len(branches),
        out_shape=[out_shape] * len(branches),
        compiler_params=_cparams(("parallel", "parallel"), **vmem),
        name="rw_lora",
    )(h3, h3, g, mix, *w_args)


def _pad_rank(w1, w2):
    r = w1.shape[1]
    rp = -(-r // LANES) * LANES
    return (jnp.pad(w1, ((0, 0), (0, rp - r))).astype(BF16),
            jnp.pad(w2, ((0, rp - r), (0, 0))).astype(BF16))


def _stack_heads(x, lane_lo):
    return jnp.concatenate([jnp.where(lane_lo, x, 0.0), jnp.where(lane_lo, 0.0, x)], axis=0)


def _rw_scan_kernel(*refs, has_vres):
    if has_vres:
        (rkv_ref, lw_ref, ag_ref, gg_ref, vf_ref, vg_ref,
         kk_ref, ka_ref, rk_ref, lnw_ref, lnb_ref, z_ref, st_ref) = refs
    else:
        (rkv_ref, lw_ref, ag_ref, gg_ref,
         kk_ref, ka_ref, rk_ref, lnw_ref, lnb_ref, z_ref, st_ref) = refs
    c = pl.program_id(1)
    npair = z_ref.shape[0]
    group = min(PAIR_GROUP, npair)
    assert npair % group == 0
    two_l = 2 * CHUNK

    @pl.when(c == 0)
    def _():
        st_ref[...] = jnp.zeros_like(st_ref)

    row = lax.broadcasted_iota(jnp.int32, (two_l, two_l), 0)
    col = lax.broadcasted_iota(jnp.int32, (two_l, two_l), 1)
    t_half = lax.broadcasted_iota(jnp.int32, (CHUNK, two_l), 0)
    s_half = lax.broadcasted_iota(jnp.int32, (CHUNK, two_l), 1) % CHUNK
    head_half = lax.broadcasted_iota(jnp.int32, (CHUNK, two_l), 1) // CHUNK
    strict_h = [jnp.logical_and(t_half > s_half, head_half == hd) for hd in range(2)]
    incl_h = [jnp.logical_and(t_half >= s_half, head_half == hd) for hd in range(2)]
    same_head = (row // CHUNK) == (col // CHUNK)
    tril = jnp.where(lax.broadcasted_iota(jnp.int32, (CHUNK, CHUNK), 0)
                     >= lax.broadcasted_iota(jnp.int32, (CHUNK, CHUNK), 1), 1.0, 0.0).astype(BF16)
    diag = row == col
    lane_lo = lax.broadcasted_iota(jnp.int32, (CHUNK, LANES), 1) < RW_HEAD
    live = jnp.logical_or(c > 0, lax.broadcasted_iota(jnp.int32, (CHUNK, LANES), 0) >= N_DUMMY)

    def each(fn, *lists):
        return [fn(*xs) for xs in zip(*lists)]

    ones_bd = jnp.where(same_head, 1.0, 0.0).astype(BF16)

    def head_sum_lanes(x):
        lo = jnp.sum(jnp.where(lane_lo, x, 0.0), axis=-1, keepdims=True)
        hi = jnp.sum(jnp.where(lane_lo, 0.0, x), axis=-1, keepdims=True)
        return jnp.where(lane_lo, lo, hi)

    def head_sum_matmul(x):
        hi, lo = _split2(x)
        both = jnp.dot(jnp.concatenate([hi, lo], axis=0), ones_bd, preferred_element_type=F32)
        return both[:x.shape[0]] + both[x.shape[0]:]

    def swap_halves(x):
        return pltpu.roll(x, RW_HEAD, axis=1)

    def group_body(gi, carry):
        ps = [gi * group + i for i in range(group)]
        r = [rkv_ref[0, p] for p in ps]
        k0 = [rkv_ref[1, p] for p in ps]
        v = [rkv_ref[2, p] for p in ps]
        lw = [lw_ref[p] for p in ps]
        ag = [ag_ref[p] for p in ps]
        if has_vres:
            v = [vi + (vf_ref[0, p] - vi) * vg_ref[p] for vi, p in zip(v, ps)]
        kk = [ki * kk_ref[p] for ki, p in zip(k0, ps)]
        k = [ki * (1.0 + (ai - 1.0) * ka_ref[p]) for ki, ai, p in zip(k0, ag, ps)]
        kk = each(lambda kki: kki / jnp.maximum(jnp.sqrt(head_sum_lanes(kki * kki)), 1e-12), kk)
        bonus = [head_sum_lanes(ri * ki * rk_ref[p]) * vi for ri, ki, vi, p in zip(r, k, v, ps)]

        cs = each(lambda lwi: _dot_sel_left(tril, lwi), lw)
        e_pos = each(jnp.exp, cs)
        e_neg = each(lambda ci: jnp.exp(-ci), cs)
        w_all = each(lambda ei: ei[CHUNK - 1:CHUNK, :], e_pos)
        a_t = each(lambda kki, ci, lwi: -kki * jnp.exp(ci - lwi), kk, cs, lw)
        r_t = each(lambda ri, ei: ri * ei, r, e_pos)
        a_s = each(lambda ati: _stack_heads(ati, lane_lo), a_t)
        r_s = each(lambda rti: _stack_heads(rti, lane_lo), r_t)
        b_t = each(lambda kki, ai, ei: kki * ai * ei, kk, ag, e_neg)
        k_t = each(lambda ki, ei: ki * ei, k, e_neg)
        v_s = each(lambda vi: _stack_heads(vi, lane_lo).astype(BF16), v)
        b_s = each(lambda bti: _stack_heads(bti, lane_lo), b_t)
        k_s = each(lambda kti: _stack_heads(kti, lane_lo), k_t)
        bh_t = each(lambda bsi, wi: (bsi * wi).T.astype(BF16), b_s, w_all)
        kh_t = each(lambda ksi, wi: (ksi * wi).T.astype(BF16), k_s, w_all)

        gram = each(lambda ati, rti, bsi, ksi: _dot_nt(
            jnp.concatenate([ati, rti], axis=0), jnp.concatenate([bsi, ksi], axis=0)), a_t, r_t, b_s, k_s)

        def block(gm, keep, rows, cols):
            blk = gm[rows * CHUNK:(rows + 1) * CHUNK, cols * two_l:(cols + 1) * two_l]
            return jnp.concatenate([jnp.where(keep[0], blk, 0.0), jnp.where(keep[1], blk, 0.0)],
                                   axis=0).astype(BF16)

        pw = each(lambda gm: block(gm, strict_h, 0, 0), gram)
        a_ak = each(lambda gm: block(gm, strict_h, 0, 1), gram)
        a_rb = each(lambda gm: block(gm, incl_h, 1, 0), gram)
        a_rk = each(lambda gm: block(gm, incl_h, 1, 1), gram)

        xv = each(lambda aki, ari, khi, vsi: jnp.dot(jnp.concatenate([aki, ari, khi], axis=0), vsi,
                                                     preferred_element_type=F32), a_ak, a_rk, kh_t, v_s)

        z = each(lambda asi, xi: asi + swap_halves(xi[:two_l]), a_s, xv)
        n_sq = CHUNK.bit_length() - 1
        for s in range(n_sq):
            if s + 1 < n_sq:
                prod = each(lambda pi, zi: jnp.dot(pi, jnp.concatenate([zi.astype(BF16), pi], axis=1),
                                                   preferred_element_type=F32), pw, z)
                z = each(lambda zi, pr: zi + pr[:, :LANES], z, prod)
                pw = each(lambda pr: pr[:, LANES:].astype(BF16), prod)
            else:
                z = each(lambda zi, pi: zi + jnp.dot(pi, zi.astype(BF16), preferred_element_type=F32), z, pw)
        zb = each(lambda zi: zi.astype(BF16), z)

        rb = each(lambda ai, bhi, zi: jnp.dot(jnp.concatenate([ai, bhi], axis=0), zi,
                                              preferred_element_type=F32), a_rb, bh_t, zb)
        r_hat = each(lambda rsi, rbi: rsi + jnp.where(same_head, rbi[:two_l], 0.0), r_s, rb)
        y_hat = each(lambda rbi, xi: swap_halves(jnp.where(same_head, 0.0, rbi[:two_l])) + xi[two_l:2 * two_l],
                     rb, xv)
        m_mat = each(lambda wi, rbi: jnp.where(diag, wi, 0.0) + jnp.where(same_head, rbi[two_l:], 0.0),
                     w_all, rb)
        n_mat = each(lambda rbi, xi: swap_halves(jnp.where(same_head, 0.0, rbi[two_l:])) + xi[2 * two_l:],
                     rb, xv)

        hb = [st_ref[p].astype(BF16) for p in ps]
        ys = each(lambda rh, mm, hi: jnp.dot(jnp.concatenate([rh, mm], axis=0).astype(BF16), hi,
                                             preferred_element_type=F32), r_hat, m_mat, hb)
        for p, ysi, nm in zip(ps, ys, n_mat):
            st_ref[p] = ysi[two_l:] + nm
        y2 = each(lambda ysi, yh: ysi[:two_l] + yh, ys, y_hat)
        y = each(lambda yi: yi[:CHUNK] + yi[CHUNK:], y2)

        dy = each(lambda yi: yi - head_sum_matmul(yi) * (1.0 / RW_HEAD), y)
        var = each(lambda di: head_sum_matmul(di * di) * (1.0 / RW_HEAD), dy)
        for p, di, vi, bi in zip(ps, dy, var, bonus):
            yn = di * lax.rsqrt(vi + RW_GN_EPS) * lnw_ref[p] + lnb_ref[p]
            z_ref[p] = jnp.where(live, (yn + bi) * gg_ref[p], 0.0).astype(z_ref.dtype)
        return carry

    lax.fori_loop(0, npair // group, group_body, 0)


def _rw_scan(rkv, lw, ag, gg, vres, kk, ka, rk, lnw, lnb, batch):
    _, npair, m, _ = rkv.shape
    nc = m // batch // CHUNK
    row_map3 = lambda bi, c: (0, bi * nc + c, 0)
    row_map4 = lambda bi, c: (0, 0, bi * nc + c, 0)
    par = pl.BlockSpec((npair, 1, LANES), lambda bi, c: (0, 0, 0))
    tile = pl.BlockSpec((npair, CHUNK, LANES), row_map3)
    in_specs = [pl.BlockSpec((3, npair, CHUNK, LANES), row_map4), tile, tile, tile]
    args = [rkv, lw, ag, gg]
    if vres is not None:
        vfirst, vgate = vres
        in_specs += [pl.BlockSpec((1, npair, CHUNK, LANES), lambda bi, c: (2, 0, bi * nc + c, 0)), tile]
        args += [vfirst, vgate]
    in_specs += [par] * 5
    args += [kk, ka, rk, lnw, lnb]
    tile_bytes = _nbytes((npair, CHUNK, LANES), F32)
    n_tiles = 3 + 3 + (2 if vres is not None else 0) + 1
    state_bytes = _nbytes((npair, 2 * CHUNK, LANES), F32)
    vmem = dict(moving=[n_tiles * tile_bytes], fixed=[state_bytes], live=8 * state_bytes)
    return pl.pallas_call(
        functools.partial(_rw_scan_kernel, has_vres=vres is not None),
        grid=(batch, nc),
        in_specs=in_specs,
        out_specs=tile,
        out_shape=jax.ShapeDtypeStruct((npair, m, LANES), BF16),
        scratch_shapes=[pltpu.VMEM((npair, 2 * CHUNK, LANES), F32)],
        compiler_params=_cparams(("parallel", "arbitrary"), **vmem),
        name="rw_scan",
    )(*args)


def _proj_kernel(x_ref, w_ref, h_ref, o_ref, *, pair_major):
    if pair_major:
        x = jnp.concatenate([x_ref[p] for p in range(x_ref.shape[0])], axis=1)
    else:
        x = x_ref[...]
    o_ref[...] = h_ref[...] + jnp.dot(x, w_ref[...], preferred_element_type=F32)


def _proj_residual(x, w, h, pair_major):
    m, d = h.shape
    kdim = w.shape[1]
    tm = _pick_tile(m, PROJ_ROWS)
    vmem = dict(moving=[_nbytes((tm, kdim), BF16), 2 * _nbytes((tm, d), F32)], fixed=[_nbytes((kdim, d), BF16)],
                live=_nbytes((tm, d), F32))
    if pair_major:
        x_spec = pl.BlockSpec((x.shape[0], tm, LANES), lambda i: (0, i, 0))
    else:
        x_spec = pl.BlockSpec((tm, kdim), lambda i: (i, 0))
    return pl.pallas_call(
        functools.partial(_proj_kernel, pair_major=pair_major),
        grid=(m // tm,),
        in_specs=[x_spec,
                  pl.BlockSpec((None, kdim, d), lambda i: (0, 0, 0)),
                  pl.BlockSpec((tm, d), lambda i: (i, 0))],
        out_specs=pl.BlockSpec((tm, d), lambda i: (i, 0)),
        out_shape=jax.ShapeDtypeStruct((m, d), F32),
        compiler_params=_cparams(("parallel",), **vmem),
        name="proj_residual",
    )(x, w, h)


def _gla_in_kernel(h_ref, g_ref, w_ref, wz_ref, wa_ref, ba_ref, p_ref, gl_ref, xn_ref):
    @pl.when(pl.program_id(1) == 0)
    def _():
        xn = _rms(h_ref[...], g_ref[...], NORM_EPS).astype(BF16)
        xn_ref[...] = xn
        za = jnp.dot(xn, wz_ref[...], preferred_element_type=F32)
        u = _dot(za, wa_ref[...]) + ba_ref[...]
        gl_ref[...] = (jnp.minimum(u, 0.0) - jnp.log1p(jnp.exp(-jnp.abs(u)))) * (1.0 / GLA_TAU)

    p_ref[...] = _dot_nt(xn_ref[...], w_ref[...]).astype(p_ref.dtype)


def _gla_in(h, g, w, n, wz, wa, ba):
    m, d = h.shape
    dk = wa.shape[1]
    tm = _pick_tile(m, GLA_IN_TILE[0])
    tn = _pick_tile(n, GLA_IN_TILE[1])
    vmem = dict(moving=[_nbytes((tm, d), F32), _nbytes((tn, d), BF16), _nbytes((tm, tn), BF16),
                        _nbytes((tm, dk), F32)],
                fixed=[_nbytes((tm, d), BF16), _nbytes((d, LANES), BF16), _nbytes((LANES, dk), F32)],
                live=_nbytes((tm, tn), F32) + 2 * _nbytes((tm, dk), F32))
    return pl.pallas_call(
        _gla_in_kernel,
        grid=(m // tm, n // tn),
        in_specs=[
            pl.BlockSpec((tm, d), lambda i, j: (i, 0)),
            pl.BlockSpec((1, d), lambda i, j: (0, 0)),
            pl.BlockSpec((None, tn, d), lambda i, j: (0, j, 0)),
            pl.BlockSpec((d, LANES), lambda i, j: (0, 0)),
            pl.BlockSpec((LANES, dk), lambda i, j: (0, 0)),
            pl.BlockSpec((1, dk), lambda i, j: (0, 0)),
        ],
        out_specs=[pl.BlockSpec((tm, tn), lambda i, j: (i, j)),
                   pl.BlockSpec((tm, dk), lambda i, j: (i, 0))],
        out_shape=[jax.ShapeDtypeStruct((m, n), BF16), jax.ShapeDtypeStruct((m, dk), F32)],
        scratch_shapes=[pltpu.VMEM((tm, d), BF16)],
        compiler_params=_cparams(("parallel", "arbitrary"), **vmem),
        name="gla_in",
    )(h, g, w, wz, wa, ba)


def _gla_chunk_kernel(q_ref, k_ref, v_ref, gate_ref, gl_ref, gn_ref, z_ref, st_ref, *, scale):
    c = pl.program_id(1)
    nb, nh, hv, hk = st_ref.shape

    @pl.when(c == 0)
    def _():
        st_ref[...] = jnp.zeros_like(st_ref)

    row = lax.broadcasted_iota(jnp.int32, (CHUNK, CHUNK), 0)
    col = lax.broadcasted_iota(jnp.int32, (CHUNK, CHUNK), 1)
    causal = row >= col
    tril = jnp.where(causal, 1.0, 0.0).astype(BF16)
    live = jnp.logical_or(c > 0, lax.broadcasted_iota(jnp.int32, gl_ref.shape[1:], 0) >= N_DUMMY)
    ksl = lambda x, h: x[:, h * hk:(h + 1) * hk]
    vsl = lambda x, h: x[:, h * hv:(h + 1) * hv]
    seqs = range(nb)
    cells = [(b, h) for b in seqs for h in range(nh)]

    gl = [jnp.where(live, gl_ref[b], 0.0) for b in seqs]
    bc = [_dot_sel_left(tril, gl[b]) for b in seqs]
    b_last = [bc[b][CHUNK - 1:CHUNK, :] for b in seqs]
    e_last = [jnp.exp(b_last[b]) for b in seqs]
    q_t = [(q_ref[b] * scale * jnp.exp(bc[b])).astype(BF16) for b in seqs]
    k_t = [(k_ref[b] * jnp.exp(-bc[b])).astype(BF16) for b in seqs]
    k_h = [(k_ref[b] * jnp.exp(b_last[b] - bc[b])).astype(BF16) for b in seqs]
    v = {(b, h): vsl(v_ref[b], h).astype(BF16) for b, h in cells}
    att = {(b, h): jnp.where(causal, _dot_nt(ksl(q_t[b], h), ksl(k_t[b], h)), 0.0).astype(BF16) for b, h in cells}
    st = {(b, h): st_ref[b, h] for b, h in cells}
    o = {(b, h): _dot_nt(ksl(q_t[b], h), st[b, h]) + jnp.dot(att[b, h], v[b, h], preferred_element_type=F32)
         for b, h in cells}
    for b, h in cells:
        st_ref[b, h] = st[b, h] * ksl(e_last[b], h) + _dot_tn(v[b, h], ksl(k_h[b], h))
    for b, h in cells:
        on = o[b, h] * lax.rsqrt(jnp.mean(o[b, h] * o[b, h], axis=-1, keepdims=True) + GLA_HEAD_EPS)
        gate = vsl(gate_ref[b], h).astype(F32)
        z_ref[b, :, h * hv:(h + 1) * hv] = (on * vsl(gn_ref[...], h) * (gate * _sigmoid(gate))).astype(z_ref.dtype)


def _gla_chunk(p, glog, gn_w, batch, d):
    tp = p.shape[0] // batch
    nc = tp // CHUNK
    dk = d // 2
    hk = dk // GLA_HEADS
    hv = d // GLA_HEADS
    p3 = p.reshape(batch, tp, p.shape[1])
    nb = GLA_SEQ_GROUP if batch % GLA_SEQ_GROUP == 0 else 1
    state_bytes = _nbytes((nb, GLA_HEADS, hv, hk), F32)
    vmem = dict(moving=[2 * _nbytes((nb, CHUNK, dk), BF16), 3 * _nbytes((nb, CHUNK, d), BF16),
                        _nbytes((nb, CHUNK, dk), F32)], fixed=[state_bytes], live=state_bytes)
    z = pl.pallas_call(
        functools.partial(_gla_chunk_kernel, scale=hk ** -0.5),
        grid=(batch // nb, nc),
        in_specs=[
            pl.BlockSpec((nb, CHUNK, dk), lambda g, c: (g, c, 0)),
            pl.BlockSpec((nb, CHUNK, dk), lambda g, c: (g, c, 1)),
            pl.BlockSpec((nb, CHUNK, d), lambda g, c: (g, c, 1)),
            pl.BlockSpec((nb, CHUNK, d), lambda g, c: (g, c, 2)),
            pl.BlockSpec((nb, CHUNK, dk), lambda g, c: (g, c, 0)),
            pl.BlockSpec((1, d), lambda g, c: (0, 0)),
        ],
        out_specs=pl.BlockSpec((nb, CHUNK, d), lambda g, c: (g, c, 0)),
        out_shape=jax.ShapeDtypeStruct((batch, tp, d), BF16),
        scratch_shapes=[pltpu.VMEM((nb, GLA_HEADS, hv, hk), F32)],
        compiler_params=_cparams(("parallel", "arbitrary"), **vmem),
        name="gla_chunk",
    )(p3, p3, p3, p3, glog.reshape(batch, tp, dk), gn_w)
    return z.reshape(batch * tp, d)


def _rwkv_layer(h, batch, j, v_first, norm_g, rw_mix, rw_w_rkv, rw_w0, rw_w1, rw_w2, rw_a0, rw_a1, rw_a2,
                rw_v0, rw_v1, rw_v2, rw_g1, rw_g2, rw_k_k, rw_k_a, rw_r_k, rw_ln_w, rw_ln_b, rw_w_o,
                sides=()):
    m, d = h.shape
    npair = d // LANES
    row = lambda t: t.reshape(1, d)
    pairs = lambda t: t.reshape(npair, 1, LANES)
    h3 = h.reshape(batch, m // batch, d)
    g = row(norm_g)
    rkv, cast = _rkv(h3, g, rw_mix[j], rw_w_rkv, sides)
    if rw_w_o is None:
        rw_w_o = cast[0]
    branches = [(1, "tanh", "decay"), (4, "none", "sigmoid"), (5, "sigmoid", "none")]
    weights = [_pad_rank(rw_w1[j], rw_w2[j]) + (row(rw_w0[j]),),
               _pad_rank(rw_a1[j], rw_a2[j]) + (row(rw_a0[j]),),
               _pad_rank(rw_g1[j], rw_g2[j]) + (jnp.zeros((1, d), F32),)]
    if j > 0:
        branches.append((3, "none", "sigmoid"))
        weights.append(_pad_rank(rw_v1[j - 1], rw_v2[j - 1]) + (row(rw_v0[j - 1]),))
    outs = _lora(h3, g, rw_mix[j], tuple(branches), weights)
    lw, ag, gg = outs[:3]
    vres = (v_first, outs[3]) if j > 0 else None
    z = _rw_scan(rkv, lw, ag, gg, vres, pairs(rw_k_k[j]), pairs(rw_k_a[j]), pairs(rw_r_k[j]),
                 pairs(rw_ln_w[j]), pairs(rw_ln_b[j]), batch)
    h = _proj_residual(z, rw_w_o, h, pair_major=True)
    return h, rkv, cast


def _gla_layer(h, batch, j, norm_g, gla_w_in_bf, gla_w_a2, gla_b_a, gla_gn_w, gla_w_o):
    m, d = h.shape
    dk = d // 2
    n_main = 2 * dk + 2 * d
    rank = gla_w_in_bf.shape[1] - n_main
    g = norm_g.reshape(1, d)
    wz = jnp.pad(gla_w_in_bf[0, n_main:, :].T, ((0, 0), (0, LANES - rank)))
    wa = jnp.pad(gla_w_a2[j], ((0, LANES - rank), (0, 0)))
    p, glog = _gla_in(h, g, gla_w_in_bf, n_main, wz, wa, gla_b_a[j].reshape(1, dk))
    z = _gla_chunk(p, glog, gla_gn_w[j].reshape(1, d), batch, d)
    return _proj_residual(z, gla_w_o, h, pair_major=False)


def kernel(x, meta, norm_mix, norm_mlp, norm_f, mlp_w1, mlp_w2, rw_mix, rw_w_rkv, rw_w0, rw_w1, rw_w2, rw_a0, rw_a1, rw_a2, rw_v0, rw_v1, rw_v2, rw_g1, rw_g2, rw_k_k, rw_k_a, rw_r_k, rw_ln_w, rw_ln_b, rw_w_o, gla_w_in, gla_w_a2, gla_b_a, gla_gn_w, gla_w_o):
    batch, seq, d = x.shape
    depth = norm_mix.shape[0]
    assert seq % CHUNK == 0 and d % (2 * LANES) == 0 and meta.shape[0] == N_META
    tp = LEAD + seq
    lead = jnp.concatenate([jnp.zeros((N_DUMMY, d), x.dtype), meta.astype(x.dtype)], axis=0)
    h = jnp.concatenate([jnp.broadcast_to(lead[None], (batch, LEAD, d)), x], axis=1).reshape(batch * tp, d)
    gf = norm_f.reshape(1, d)
    rw_w_rkv2 = rw_w_rkv.reshape(rw_w_rkv.shape[0], 3 * d, d)
    gla_w_in_t = jnp.transpose(gla_w_in, (0, 2, 1))

    def mixer_weights(i):
        j = i // 2
        mixer = [(rw_w_rkv2, j), (rw_w_o, j)] if i % 2 == 0 else [(gla_w_in_t, j), (gla_w_o, j)]
        return mixer + [(mlp_w1, i), (mlp_w2, i)]

    w_first, first_layer = mixer_weights(0)[0]
    cast = [w_first[first_layer:first_layer + 1].astype(BF16), None, None, None]
    v_first = None
    for i in range(depth):
        j = i // 2
        w_a, w_b, w1_bf, w2_bf = cast
        if i % 2 == 0:
            h, rkv, early = _rwkv_layer(h, batch, j, v_first, norm_mix[i], rw_mix, w_a.reshape(1, 3, d, d), rw_w0,
                                        rw_w1, rw_w2, rw_a0, rw_a1, rw_a2, rw_v0, rw_v1, rw_v2, rw_g1, rw_g2,
                                        rw_k_k, rw_k_a, rw_r_k.reshape(rw_r_k.shape[0], d), rw_ln_w, rw_ln_b, w_b,
                                        sides=mixer_weights(0)[1:] if i == 0 else ())
            if i == 0:
                _, w1_bf, w2_bf = early
            if j == 0:
                v_first = rkv
        else:
            h = _gla_layer(h, batch, j, norm_mix[i], w_a, gla_w_a2, gla_b_a, gla_gn_w, w_b)
        g_mlp = norm_mlp[i].reshape(1, d)
        if i == depth - 1:
            return _mlp_final(h.reshape(batch, tp, d), g_mlp, w1_bf, w2_bf, gf, *MLP_FINAL_TILE)
        h, cast = _mlp(h, g_mlp, w1_bf, w2_bf, gf, *MLP_TILE, sides=mixer_weights(i + 1))
```

```python
import functools
import math

import jax
import jax.numpy as jnp
from jax import lax
from jax.experimental import pallas as pl
from jax.experimental.pallas import tpu as pltpu

F32 = jnp.float32
BF16 = jnp.bfloat16

N_META = 16
CHUNK = 64
LEAD = CHUNK
N_DUMMY = LEAD - N_META
NORM_EPS = 1e-6

RW_HEAD = 64
LANES = 128
RW_GN_EPS = 64e-5
RW_DECAY_SCALE = 0.6065306597126334
PAIR_GROUP = 16

GLA_HEADS = 4
GLA_TAU = 16.0
GLA_HEAD_EPS = 1e-5
GLA_SEQ_GROUP = 4

V7X_VMEM_BYTES = 64 * 1024 * 1024
VMEM_REQUEST_CAP = V7X_VMEM_BYTES - 8 * 1024 * 1024

MLP_TILE = (832, 1024)
MLP_FINAL_TILE = (512, 1024)
RKV_ROWS = 208
LORA_ROWS = 416
PROJ_ROWS = 640
GLA_IN_TILE = (640, 2048)


def _nbytes(shape, dtype):
    return math.prod(shape) * jnp.dtype(dtype).itemsize


def _cparams(sem, moving, fixed=(), live=0):
    need = 2 * sum(moving) + sum(fixed) + live
    return pltpu.CompilerParams(dimension_semantics=sem, vmem_limit_bytes=min(need, VMEM_REQUEST_CAP))


def _dot(a, b):
    return jnp.dot(a.astype(BF16), b.astype(BF16), preferred_element_type=F32)


def _dot_nt(a, b):
    return lax.dot_general(a.astype(BF16), b.astype(BF16), (((1,), (1,)), ((), ())),
                           preferred_element_type=F32)


def _dot_tn(a, b):
    return lax.dot_general(a.astype(BF16), b.astype(BF16), (((0,), (0,)), ((), ())),
                           preferred_element_type=F32)


def _split2(x):
    hi = x.astype(BF16)
    return hi, (x - hi.astype(F32)).astype(BF16)


def _dot_sel_left(sel, x):
    hi, lo = _split2(x)
    return jnp.dot(sel, hi, preferred_element_type=F32) + jnp.dot(sel, lo, preferred_element_type=F32)


def _rms(x, g, eps):
    return x * lax.rsqrt(jnp.mean(x * x, axis=-1, keepdims=True) + eps) * g


def _sigmoid(x):
    return 1.0 / (1.0 + jnp.exp(-x))


def _pick_tile(n, target):
    best = None
    for t in range(16, min(n, target) + 1, 16):
        if n % t == 0:
            best = t
    assert best is not None, (n, target)
    return best


def _side_cast_specs(sides, n_steps, step_of):
    in_specs, out_specs, out_shapes, moving = [], [], [], []
    for arr, layer in sides:
        _, rows, cols = arr.shape
        blk = next(b for b in range(16, rows + 1, 16) if rows % b == 0 and rows // b <= n_steps)
        last = rows // blk - 1
        in_specs.append(pl.BlockSpec(
            (None, blk, cols), lambda *ids, layer=layer, last=last: (layer, jnp.minimum(step_of(*ids), last), 0)))
        out_specs.append(pl.BlockSpec(
            (None, blk, cols), lambda *ids, last=last: (0, jnp.minimum(step_of(*ids), last), 0)))
        out_shapes.append(jax.ShapeDtypeStruct((1, rows, cols), BF16))
        moving += [_nbytes((blk, cols), F32), _nbytes((blk, cols), BF16)]
    return in_specs, out_specs, out_shapes, moving


def _mlp_kernel(*refs, final, n_side):
    h_ref, g_ref, w1_ref, w2_ref, gf_ref = refs[:5]
    side_in = refs[5:5 + n_side]
    o_ref = refs[5 + n_side]
    side_out = refs[6 + n_side:6 + 2 * n_side]
    xn_ref = refs[6 + 2 * n_side]
    ff_axis = 2 if final else 1
    if final:
        o_ref = o_ref.at[0]
    j = pl.program_id(ff_axis)

    @pl.when(j == 0)
    def _():
        x = h_ref[...]
        xn_ref[...] = _rms(x, g_ref[...], NORM_EPS).astype(BF16)
        o_ref[...] = x

    hid = jnp.dot(xn_ref[...], w1_ref[...], preferred_element_type=F32)
    hid = jnp.maximum(hid, 0.0)
    hid = hid * hid
    o_ref[...] += jnp.dot(hid.astype(BF16), w2_ref[...], preferred_element_type=F32)
    for src, dst in zip(side_in, side_out):
        dst[...] = src[...].astype(BF16)

    if final:
        @pl.when(j == pl.num_programs(ff_axis) - 1)
        def _():
            o_ref[...] = _rms(o_ref[...], gf_ref[...], NORM_EPS)


def _mlp(h, g, w1, w2, gf, tm_target, tf_target, sides=()):
    m, d = h.shape
    ff = w1.shape[2]
    tm = _pick_tile(m, tm_target)
    tf = _pick_tile(ff, tf_target)
    nj = ff // tf
    side_in, side_out, side_shapes, side_bytes = _side_cast_specs(
        sides, (m // tm) * nj, lambda i, j: i * nj + j)
    vmem = dict(moving=[2 * _nbytes((tm, d), F32), 2 * _nbytes((d, tf), BF16)] + side_bytes,
                fixed=[_nbytes((tm, d), BF16)], live=_nbytes((tm, tf), F32) + _nbytes((tm, tf), BF16))
    outs = pl.pallas_call(
        functools.partial(_mlp_kernel, final=False, n_side=len(sides)),
        grid=(m // tm, nj),
        in_specs=[
            pl.BlockSpec((tm, d), lambda i, j: (i, 0)),
            pl.BlockSpec((1, d), lambda i, j: (0, 0)),
            pl.BlockSpec((None, d, tf), lambda i, j: (0, 0, j)),
            pl.BlockSpec((None, tf, d), lambda i, j: (0, j, 0)),
            pl.BlockSpec((1, d), lambda i, j: (0, 0)),
        ] + side_in,
        out_specs=[pl.BlockSpec((tm, d), lambda i, j: (i, 0))] + side_out,
        out_shape=[jax.ShapeDtypeStruct((m, d), F32)] + side_shapes,
        scratch_shapes=[pltpu.VMEM((tm, d), BF16)],
        compiler_params=_cparams(("arbitrary", "arbitrary"), **vmem),
        name="mlp",
    )(h, g, w1, w2, gf, *[arr for arr, _ in sides])
    return outs[0], outs[1:]


def _mlp_final(h3, g, w1, w2, gf, tm_target, tf_target):
    b, tp, d = h3.shape
    seq = tp - LEAD
    ff = w1.shape[2]
    tm = _pick_tile(seq, tm_target)
    tf = _pick_tile(ff, tf_target)
    vmem = dict(moving=[2 * _nbytes((tm, d), F32), 2 * _nbytes((d, tf), BF16)],
                fixed=[_nbytes((tm, d), BF16)], live=_nbytes((tm, tf), F32) + _nbytes((tm, tf), BF16))
    return pl.pallas_call(
        functools.partial(_mlp_kernel, final=True, n_side=0),
        grid=(b, seq // tm, ff // tf),
        in_specs=[
            pl.BlockSpec((pl.Element(tm), pl.Element(d)),
                         lambda bi, i, j: (pl.multiple_of(bi * tp + LEAD + i * tm, 16), 0)),
            pl.BlockSpec((1, d), lambda bi, i, j: (0, 0)),
            pl.BlockSpec((None, d, tf), lambda bi, i, j: (0, 0, j)),
            pl.BlockSpec((None, tf, d), lambda bi, i, j: (0, j, 0)),
            pl.BlockSpec((1, d), lambda bi, i, j: (0, 0)),
        ],
        out_specs=pl.BlockSpec((1, tm, d), lambda bi, i, j: (bi, i, 0)),
        out_shape=jax.ShapeDtypeStruct((b, seq, d), F32),
        scratch_shapes=[pltpu.VMEM((tm, d), BF16)],
        compiler_params=_cparams(("parallel", "parallel", "arbitrary"), **vmem),
        name="mlp_final",
    )(h3.reshape(b * tp, d), g, w1, w2, gf)


def _shifted_norm(h_ref, hp_ref, g_ref):
    g = g_ref[...]
    hn = _rms(h_ref[0], g, NORM_EPS)
    pn = _rms(hp_ref[0], g, NORM_EPS)[7:8]
    pn = jnp.where(pl.program_id(1) == 0, 0.0, pn)
    row = lax.broadcasted_iota(jnp.int32, hn.shape, 0)
    prev = jnp.where(row == 0, pn, pltpu.roll(hn, 1, axis=0))
    return hn, prev - hn


def _to_pairs(o_ref, lead, val):
    for p in range(val.shape[1] // LANES):
        o_ref[lead + (p,)] = val[:, p * LANES:(p + 1) * LANES].astype(o_ref.dtype)


def _shift_specs(tm, d):
    nb = tm // 8
    return [
        pl.BlockSpec((1, tm, d), lambda bi, i: (bi, i, 0)),
        pl.BlockSpec((1, 8, d), lambda bi, i: (bi, jnp.maximum(i * nb - 1, 0), 0)),
        pl.BlockSpec((1, d), lambda bi, i: (0, 0)),
        pl.BlockSpec((6, d), lambda bi, i: (0, 0)),
    ]


RKV_MIX_ROWS = (0, 2, 3)


def _rkv_kernel(*refs, n_side):
    h_ref, hp_ref, g_ref, mix_ref, w_ref = refs[:5]
    side_in = refs[5:5 + n_side]
    o_ref = refs[5 + n_side]
    side_out = refs[6 + n_side:]
    hn, xx = _shifted_norm(h_ref, hp_ref, g_ref)
    for s, mrow in enumerate(RKV_MIX_ROWS):
        x = (hn + xx * mix_ref[mrow:mrow + 1, :]).astype(BF16)
        _to_pairs(o_ref, (s,), jnp.dot(x, w_ref[s], preferred_element_type=F32))
    for src, dst in zip(side_in, side_out):
        dst[...] = src[...].astype(BF16)


def _rkv(h3, g, mix, w, sides=()):
    b, tp, d = h3.shape
    tm = _pick_tile(tp, RKV_ROWS)
    nt = tp // tm
    npair = d // LANES
    side_in, side_out, side_shapes, side_bytes = _side_cast_specs(sides, b * nt, lambda bi, i: bi * nt + i)
    vmem = dict(moving=[_nbytes((tm + 8, d), F32), 3 * _nbytes((tm, d), F32)] + side_bytes,
                fixed=[_nbytes((3, d, d), BF16)], live=4 * _nbytes((tm, d), F32))
    outs = pl.pallas_call(
        functools.partial(_rkv_kernel, n_side=len(sides)),
        grid=(b, nt),
        in_specs=_shift_specs(tm, d) + [
            pl.BlockSpec((None, 3, d, d), lambda bi, i: (0, 0, 0, 0), pipeline_mode=pl.Buffered(1)),
        ] + side_in,
        out_specs=[pl.BlockSpec((3, npair, tm, LANES), lambda bi, i: (0, 0, bi * nt + i, 0))] + side_out,
        out_shape=[jax.ShapeDtypeStruct((3, npair, b * tp, LANES), F32)] + side_shapes,
        compiler_params=_cparams(("arbitrary", "arbitrary"), **vmem),
        name="rw_rkv",
    )(h3, h3, g, mix, w, *[arr for arr, _ in sides])
    return outs[0], outs[1:]


def _lora_kernel(*refs, branches):
    h_ref, hp_ref, g_ref, mix_ref = refs[:4]
    nb = len(branches)
    w_refs = refs[4:4 + 3 * nb]
    o_refs = refs[4 + 3 * nb:]
    hn, xx = _shifted_norm(h_ref, hp_ref, g_ref)
    for bi, (mrow, mid, out) in enumerate(branches):
        w1_ref, w2_ref, b_ref = w_refs[3 * bi:3 * bi + 3]
        x = (hn + xx * mix_ref[mrow:mrow + 1, :]).astype(BF16)
        z = jnp.dot(x, w1_ref[...], preferred_element_type=F32)
        if mid == "tanh":
            z = jnp.tanh(z)
        elif mid == "sigmoid":
            z = _sigmoid(z)
        y = jnp.dot(z.astype(BF16), w2_ref[...], preferred_element_type=F32)
        if out == "decay":
            y = -RW_DECAY_SCALE * _sigmoid(y + b_ref[...])
        elif out == "sigmoid":
            y = _sigmoid(y + b_ref[...])
        _to_pairs(o_refs[bi], (), y)


def _lora(h3, g, mix, branches, weights):
    b, tp, d = h3.shape
    tm = _pick_tile(tp, LORA_ROWS)
    nt = tp // tm
    npair = d // LANES
    w_specs, w_args = [], []
    for w1, w2, bias in weights:
        r = w1.shape[1]
        w_specs += [pl.BlockSpec((d, r), lambda bi, i: (0, 0)),
                    pl.BlockSpec((r, d), lambda bi, i: (0, 0)),
                    pl.BlockSpec((1, d), lambda bi, i: (0, 0))]
        w_args += [w1, w2, bias]
    out_spec = pl.BlockSpec((npair, tm, LANES), lambda bi, i: (0, bi * nt + i, 0))
    out_shape = jax.ShapeDtypeStruct((npair, b * tp, LANES), F32)
    vmem = dict(moving=[_nbytes((tm + 8, d), F32)] + [_nbytes((tm, d), F32)] * len(branches),
                fixed=[_nbytes(w.shape, w.dtype) for w in w_args], live=4 * _nbytes((tm, d), F32))
    return pl.pallas_call(
        functools.partial(_lora_kernel, branches=branches),
        grid=(b, nt),
        in_specs=_shift_specs(tm, d) + w_specs,
        out_specs=[out_spec] * len(branches),
        out_shape=[out_shape] * len(branches),
        compiler_params=_cparams(("parallel", "parallel"), **vmem),
        name="rw_lora",
    )(h3, h3, g, mix, *w_args)


def _pad_rank(w1, w2):
    r = w1.shape[1]
    rp = -(-r // LANES) * LANES
    return (jnp.pad(w1, ((0, 0), (0, rp - r))).astype(BF16),
            jnp.pad(w2, ((0, rp - r), (0, 0))).astype(BF16))


def _stack_heads(x, lane_lo):
    return jnp.concatenate([jnp.where(lane_lo, x, 0.0), jnp.where(lane_lo, 0.0, x)], axis=0)


def _rw_scan_kernel(*refs, has_vres):
    if has_vres:
        (rkv_ref, lw_ref, ag_ref, gg_ref, vf_ref, vg_ref,
         kk_ref, ka_ref, rk_ref, lnw_ref, lnb_ref, z_ref, st_ref) = refs
    else:
        (rkv_ref, lw_ref, ag_ref, gg_ref,
         kk_ref, ka_ref, rk_ref, lnw_ref, lnb_ref, z_ref, st_ref) = refs
    c = pl.program_id(1)
    npair = z_ref.shape[0]
    group = min(PAIR_GROUP, npair)
    assert npair % group == 0
    two_l = 2 * CHUNK

    @pl.when(c == 0)
    def _():
        st_ref[...] = jnp.zeros_like(st_ref)

    row = lax.broadcasted_iota(jnp.int32, (two_l, two_l), 0)
    col = lax.broadcasted_iota(jnp.int32, (two_l, two_l), 1)
    t_half = lax.broadcasted_iota(jnp.int32, (CHUNK, two_l), 0)
    s_half = lax.broadcasted_iota(jnp.int32, (CHUNK, two_l), 1) % CHUNK
    strict_t = t_half > s_half
    incl_t = t_half >= s_half
    same_head = (row // CHUNK) == (col // CHUNK)
    tril = jnp.where(lax.broadcasted_iota(jnp.int32, (CHUNK, CHUNK), 0)
                     >= lax.broadcasted_iota(jnp.int32, (CHUNK, CHUNK), 1), 1.0, 0.0).astype(BF16)
    diag = row == col
    lane_lo = lax.broadcasted_iota(jnp.int32, (CHUNK, LANES), 1) < RW_HEAD
    live = jnp.logical_or(c > 0, lax.broadcasted_iota(jnp.int32, (CHUNK, LANES), 0) >= N_DUMMY)

    def each(fn, *lists):
        return [fn(*xs) for xs in zip(*lists)]

    ones_bd = jnp.where(same_head, 1.0, 0.0).astype(BF16)

    def head_sum_lanes(x):
        lo = jnp.sum(jnp.where(lane_lo, x, 0.0), axis=-1, keepdims=True)
        hi = jnp.sum(jnp.where(lane_lo, 0.0, x), axis=-1, keepdims=True)
        return jnp.where(lane_lo, lo, hi)

    def head_sum_matmul(x):
        hi, lo = _split2(x)
        both = jnp.dot(jnp.concatenate([hi, lo], axis=0), ones_bd, preferred_element_type=F32)
        return both[:x.shape[0]] + both[x.shape[0]:]

    def swap_halves(x):
        return pltpu.roll(x, RW_HEAD, axis=1)

    def group_body(gi, carry):
        ps = [gi * group + i for i in range(group)]
        r = [rkv_ref[0, p] for p in ps]
        k0 = [rkv_ref[1, p] for p in ps]
        v = [rkv_ref[2, p] for p in ps]
        lw = [lw_ref[p] for p in ps]
        ag = [ag_ref[p] for p in ps]
        if has_vres:
            v = [vi + (vf_ref[0, p] - vi) * vg_ref[p] for vi, p in zip(v, ps)]
        kk = [ki * kk_ref[p] for ki, p in zip(k0, ps)]
        k = [ki * (1.0 + (ai - 1.0) * ka_ref[p]) for ki, ai, p in zip(k0, ag, ps)]
        kk = each(lambda kki: kki / jnp.maximum(jnp.sqrt(head_sum_lanes(kki * kki)), 1e-12), kk)
        bonus = [head_sum_lanes(ri * ki * rk_ref[p]) * vi for ri, ki, vi, p in zip(r, k, v, ps)]

        cs = each(lambda lwi: _dot_sel_left(tril, lwi), lw)
        e_pos = each(jnp.exp, cs)
        e_neg = each(lambda ci: jnp.exp(-ci), cs)
        w_all = each(lambda ei: ei[CHUNK - 1:CHUNK, :], e_pos)
        a_t = each(lambda kki, ci, lwi: -kki * jnp.exp(ci - lwi), kk, cs, lw)
        r_t = each(lambda ri, ei: ri * ei, r, e_pos)
        b_t = each(lambda kki, ai, ei: kki * ai * ei, kk, ag, e_neg)
        k_t = each(lambda ki, ei: ki * ei, k, e_neg)
        v_s = each(lambda vi: _stack_heads(vi, lane_lo).astype(BF16), v)
        b_s = each(lambda bti: _stack_heads(bti, lane_lo), b_t)
        k_s = each(lambda kti: _stack_heads(kti, lane_lo), k_t)
        bh_t = each(lambda bsi, wi: (bsi * wi).T.astype(BF16), b_s, w_all)
        kh_t = each(lambda ksi, wi: (ksi * wi).T.astype(BF16), k_s, w_all)

        gram = each(lambda ati, rti, bsi, ksi: _dot_nt(
            jnp.concatenate([ati, rti], axis=0), jnp.concatenate([bsi, ksi], axis=0)), a_t, r_t, b_s, k_s)

        block = lambda gm, keep, rows, cols: jnp.where(
            keep, gm[rows * CHUNK:(rows + 1) * CHUNK, cols * two_l:(cols + 1) * two_l], 0.0)
        p_c = each(lambda gm: block(gm, strict_t, 0, 0), gram)
        a_ak = each(lambda gm: block(gm, strict_t, 0, 1).astype(BF16), gram)
        a_rb = each(lambda gm: _stack_heads(block(gm, incl_t, 1, 0), lane_lo).astype(BF16), gram)
        a_rk = each(lambda gm: block(gm, incl_t, 1, 1).astype(BF16), gram)

        xv = each(lambda aki, ari, khi, vsi: jnp.dot(jnp.concatenate([aki, ari, khi], axis=0), vsi,
                                                     preferred_element_type=F32), a_ak, a_rk, kh_t, v_s)

        def z_init(ati, xi):
            su = swap_halves(xi[:CHUNK])
            return jnp.concatenate([jnp.where(lane_lo, ati, su), jnp.where(lane_lo, su, ati)], axis=0)

        z = each(z_init, a_t, xv)
        pw = each(lambda pci: _stack_heads(pci, lane_lo).astype(BF16), p_c)
        n_sq = CHUNK.bit_length() - 1
        for s in range(n_sq):
            z = each(lambda zi, pi: zi + jnp.dot(pi, zi.astype(BF16), preferred_element_type=F32), z, pw)
            if s + 1 < n_sq:
                p_c = each(lambda pci, pi: jnp.dot(pci.astype(BF16), pi, preferred_element_type=F32), p_c, pw)
                pw = each(lambda pci: _stack_heads(pci, lane_lo).astype(BF16), p_c)
        zb = each(lambda zi: zi.astype(BF16), z)

        rb = each(lambda ai, bhi, zi: jnp.dot(jnp.concatenate([ai, bhi], axis=0), zi,
                                              preferred_element_type=F32), a_rb, bh_t, zb)
        r_hat = each(lambda rti, rbi: rti + jnp.where(lane_lo, rbi[:CHUNK], rbi[CHUNK:two_l]), r_t, rb)
        y_hat = each(lambda rbi, xi: swap_halves(jnp.where(lane_lo, rbi[CHUNK:two_l], rbi[:CHUNK]))
                     + xi[CHUNK:two_l], rb, xv)
        m_mat = each(lambda wi, rbi: jnp.where(diag, wi, 0.0) + jnp.where(same_head, rbi[two_l:], 0.0),
                     w_all, rb)
        n_mat = each(lambda rbi, xi: swap_halves(jnp.where(same_head, 0.0, rbi[two_l:])) + xi[two_l:],
                     rb, xv)

        hb = [st_ref[p].astype(BF16) for p in ps]
        ys = each(lambda rh, mm, hi: jnp.dot(jnp.concatenate([rh, mm], axis=0).astype(BF16), hi,
                                             preferred_element_type=F32), r_hat, m_mat, hb)
        for p, ysi, nm in zip(ps, ys, n_mat):
            st_ref[p] = ysi[CHUNK:] + nm
        y = each(lambda ysi, yh: ysi[:CHUNK] + yh, ys, y_hat)

        dy = each(lambda yi: yi - head_sum_matmul(yi) * (1.0 / RW_HEAD), y)
        var = each(lambda di: head_sum_matmul(di * di) * (1.0 / RW_HEAD), dy)
        for p, di, vi, bi in zip(ps, dy, var, bonus):
            yn = di * lax.rsqrt(vi + RW_GN_EPS) * lnw_ref[p] + lnb_ref[p]
            z_ref[p] = jnp.where(live, (yn + bi) * gg_ref[p], 0.0).astype(z_ref.dtype)
        return carry

    lax.fori_loop(0, npair // group, group_body, 0)


def _rw_scan(rkv, lw, ag, gg, vres, kk, ka, rk, lnw, lnb, batch):
    _, npair, m, _ = rkv.shape
    nc = m // batch // CHUNK
    row_map3 = lambda bi, c: (0, bi * nc + c, 0)
    row_map4 = lambda bi, c: (0, 0, bi * nc + c, 0)
    par = pl.BlockSpec((npair, 1, LANES), lambda bi, c: (0, 0, 0))
    tile = pl.BlockSpec((npair, CHUNK, LANES), row_map3)
    in_specs = [pl.BlockSpec((3, npair, CHUNK, LANES), row_map4), tile, tile, tile]
    args = [rkv, lw, ag, gg]
    if vres is not None:
        vfirst, vgate = vres
        in_specs += [pl.BlockSpec((1, npair, CHUNK, LANES), lambda bi, c: (2, 0, bi * nc + c, 0)), tile]
        args += [vfirst, vgate]
    in_specs += [par] * 5
    args += [kk, ka, rk, lnw, lnb]
    tile_bytes = _nbytes((npair, CHUNK, LANES), F32)
    n_tiles = 3 + 3 + (2 if vres is not None else 0) + 1
    state_bytes = _nbytes((npair, 2 * CHUNK, LANES), F32)
    vmem = dict(moving=[n_tiles * tile_bytes], fixed=[state_bytes], live=8 * state_bytes)
    return pl.pallas_call(
        functools.partial(_rw_scan_kernel, has_vres=vres is not None),
        grid=(batch, nc),
        in_specs=in_specs,
        out_specs=tile,
        out_shape=jax.ShapeDtypeStruct((npair, m, LANES), BF16),
        scratch_shapes=[pltpu.VMEM((npair, 2 * CHUNK, LANES), F32)],
        compiler_params=_cparams(("parallel", "arbitrary"), **vmem),
        name="rw_scan",
    )(*args)


def _proj_kernel(x_ref, w_ref, h_ref, o_ref, *, pair_major):
    if pair_major:
        x = jnp.concatenate([x_ref[p] for p in range(x_ref.shape[0])], axis=1)
    else:
        x = x_ref[...]
    o_ref[...] = h_ref[...] + jnp.dot(x, w_ref[...], preferred_element_type=F32)


def _proj_residual(x, w, h, pair_major):
    m, d = h.shape
    kdim = w.shape[1]
    tm = _pick_tile(m, PROJ_ROWS)
    vmem = dict(moving=[_nbytes((tm, kdim), BF16), 2 * _nbytes((tm, d), F32)], fixed=[_nbytes((kdim, d), BF16)],
                live=_nbytes((tm, d), F32))
    if pair_major:
        x_spec = pl.BlockSpec((x.shape[0], tm, LANES), lambda i: (0, i, 0))
    else:
        x_spec = pl.BlockSpec((tm, kdim), lambda i: (i, 0))
    return pl.pallas_call(
        functools.partial(_proj_kernel, pair_major=pair_major),
        grid=(m // tm,),
        in_specs=[x_spec,
                  pl.BlockSpec((None, kdim, d), lambda i: (0, 0, 0)),
                  pl.BlockSpec((tm, d), lambda i: (i, 0))],
        out_specs=pl.BlockSpec((tm, d), lambda i: (i, 0)),
        out_shape=jax.ShapeDtypeStruct((m, d), F32),
        compiler_params=_cparams(("parallel",), **vmem),
        name="proj_residual",
    )(x, w, h)


def _gla_in_kernel(h_ref, g_ref, w_ref, wz_ref, wa_ref, ba_ref, p_ref, gl_ref, xn_ref):
    @pl.when(pl.program_id(1) == 0)
    def _():
        xn = _rms(h_ref[...], g_ref[...], NORM_EPS).astype(BF16)
        xn_ref[...] = xn
        za = jnp.dot(xn, wz_ref[...], preferred_element_type=F32)
        u = _dot(za, wa_ref[...]) + ba_ref[...]
        gl_ref[...] = (jnp.minimum(u, 0.0) - jnp.log1p(jnp.exp(-jnp.abs(u)))) * (1.0 / GLA_TAU)

    p_ref[...] = _dot_nt(xn_ref[...], w_ref[...]).astype(p_ref.dtype)


def _gla_in(h, g, w, n, wz, wa, ba):
    m, d = h.shape
    dk = wa.shape[1]
    tm = _pick_tile(m, GLA_IN_TILE[0])
    tn = _pick_tile(n, GLA_IN_TILE[1])
    vmem = dict(moving=[_nbytes((tm, d), F32), _nbytes((tn, d), BF16), _nbytes((tm, tn), BF16),
                        _nbytes((tm, dk), F32)],
                fixed=[_nbytes((tm, d), BF16), _nbytes((d, LANES), BF16), _nbytes((LANES, dk), F32)],
                live=_nbytes((tm, tn), F32) + 2 * _nbytes((tm, dk), F32))
    return pl.pallas_call(
        _gla_in_kernel,
        grid=(m // tm, n // tn),
        in_specs=[
            pl.BlockSpec((tm, d), lambda i, j: (i, 0)),
            pl.BlockSpec((1, d), lambda i, j: (0, 0)),
            pl.BlockSpec((None, tn, d), lambda i, j: (0, j, 0)),
            pl.BlockSpec((d, LANES), lambda i, j: (0, 0)),
            pl.BlockSpec((LANES, dk), lambda i, j: (0, 0)),
            pl.BlockSpec((1, dk), lambda i, j: (0, 0)),
        ],
        out_specs=[pl.BlockSpec((tm, tn), lambda i, j: (i, j)),
                   pl.BlockSpec((tm, dk), lambda i, j: (i, 0))],
        out_shape=[jax.ShapeDtypeStruct((m, n), BF16), jax.ShapeDtypeStruct((m, dk), F32)],
        scratch_shapes=[pltpu.VMEM((tm, d), BF16)],
        compiler_params=_cparams(("parallel", "arbitrary"), **vmem),
        name="gla_in",
    )(h, g, w, wz, wa, ba)


def _gla_chunk_kernel(q_ref, k_ref, v_ref, gate_ref, gl_ref, gn_ref, z_ref, st_ref, *, scale):
    c = pl.program_id(1)
    nb, nh, hv, hk = st_ref.shape

    @pl.when(c == 0)
    def _():
        st_ref[...] = jnp.zeros_like(st_ref)

    row = lax.broadcasted_iota(jnp.int32, (CHUNK, CHUNK), 0)
    col = lax.broadcasted_iota(jnp.int32, (CHUNK, CHUNK), 1)
    causal = row >= col
    tril = jnp.where(causal, 1.0, 0.0).astype(BF16)
    live = jnp.logical_or(c > 0, lax.broadcasted_iota(jnp.int32, gl_ref.shape[1:], 0) >= N_DUMMY)
    ksl = lambda x, h: x[:, h * hk:(h + 1) * hk]
    vsl = lambda x, h: x[:, h * hv:(h + 1) * hv]
    seqs = range(nb)
    cells = [(b, h) for b in seqs for h in range(nh)]

    gl = [jnp.where(live, gl_ref[b], 0.0) for b in seqs]
    bc = [_dot_sel_left(tril, gl[b]) for b in seqs]
    b_last = [bc[b][CHUNK - 1:CHUNK, :] for b in seqs]
    e_last = [jnp.exp(b_last[b]) for b in seqs]
    q_t = [(q_ref[b] * scale * jnp.exp(bc[b])).astype(BF16) for b in seqs]
    k_t = [(k_ref[b] * jnp.exp(-bc[b])).astype(BF16) for b in seqs]
    k_h = [(k_ref[b] * jnp.exp(b_last[b] - bc[b])).astype(BF16) for b in seqs]
    v = {(b, h): vsl(v_ref[b], h).astype(BF16) for b, h in cells}
    att = {(b, h): jnp.where(causal, _dot_nt(ksl(q_t[b], h), ksl(k_t[b], h)), 0.0).astype(BF16) for b, h in cells}
    st = {(b, h): st_ref[b, h] for b, h in cells}
    o = {(b, h): _dot_nt(ksl(q_t[b], h), st[b, h]) + jnp.dot(att[b, h], v[b, h], preferred_element_type=F32)
         for b, h in cells}
    for b, h in cells:
        st_ref[b, h] = st[b, h] * ksl(e_last[b], h) + _dot_tn(v[b, h], ksl(k_h[b], h))
    for b, h in cells:
        on = o[b, h] * lax.rsqrt(jnp.mean(o[b, h] * o[b, h], axis=-1, keepdims=True) + GLA_HEAD_EPS)
        gate = vsl(gate_ref[b], h).astype(F32)
        z_ref[b, :, h * hv:(h + 1) * hv] = (on * vsl(gn_ref[...], h) * (gate * _sigmoid(gate))).astype(z_ref.dtype)


def _gla_chunk(p, glog, gn_w, batch, d):
    tp = p.shape[0] // batch
    nc = tp // CHUNK
    dk = d // 2
    hk = dk // GLA_HEADS
    hv = d // GLA_HEADS
    p3 = p.reshape(batch, tp, p.shape[1])
    nb = GLA_SEQ_GROUP if batch % GLA_SEQ_GROUP == 0 else 1
    state_bytes = _nbytes((nb, GLA_HEADS, hv, hk), F32)
    vmem = dict(moving=[2 * _nbytes((nb, CHUNK, dk), BF16), 3 * _nbytes((nb, CHUNK, d), BF16),
                        _nbytes((nb, CHUNK, dk), F32)], fixed=[state_bytes], live=state_bytes)
    z = pl.pallas_call(
        functools.partial(_gla_chunk_kernel, scale=hk ** -0.5),
        grid=(batch // nb, nc),
        in_specs=[
            pl.BlockSpec((nb, CHUNK, dk), lambda g, c: (g, c, 0)),
            pl.BlockSpec((nb, CHUNK, dk), lambda g, c: (g, c, 1)),
            pl.BlockSpec((nb, CHUNK, d), lambda g, c: (g, c, 1)),
            pl.BlockSpec((nb, CHUNK, d), lambda g, c: (g, c, 2)),
            pl.BlockSpec((nb, CHUNK, dk), lambda g, c: (g, c, 0)),
            pl.BlockSpec((1, d), lambda g, c: (0, 0)),
        ],
        out_specs=pl.BlockSpec((nb, CHUNK, d), lambda g, c: (g, c, 0)),
        out_shape=jax.ShapeDtypeStruct((batch, tp, d), BF16),
        scratch_shapes=[pltpu.VMEM((nb, GLA_HEADS, hv, hk), F32)],
        compiler_params=_cparams(("parallel", "arbitrary"), **vmem),
        name="gla_chunk",
    )(p3, p3, p3, p3, glog.reshape(batch, tp, dk), gn_w)
    return z.reshape(batch * tp, d)


def _rwkv_layer(h, batch, j, v_first, norm_g, rw_mix, rw_w_rkv, rw_w0, rw_w1, rw_w2, rw_a0, rw_a1, rw_a2,
                rw_v0, rw_v1, rw_v2, rw_g1, rw_g2, rw_k_k, rw_k_a, rw_r_k, rw_ln_w, rw_ln_b, rw_w_o,
                sides=()):
    m, d = h.shape
    npair = d // LANES
    row = lambda t: t.reshape(1, d)
    pairs = lambda t: t.reshape(npair, 1, LANES)
    h3 = h.reshape(batch, m // batch, d)
    g = row(norm_g)
    rkv, cast = _rkv(h3, g, rw_mix[j], rw_w_rkv, sides)
    if rw_w_o is None:
        rw_w_o = cast[0]
    branches = [(1, "tanh", "decay"), (4, "none", "sigmoid"), (5, "sigmoid", "none")]
    weights = [_pad_rank(rw_w1[j], rw_w2[j]) + (row(rw_w0[j]),),
               _pad_rank(rw_a1[j], rw_a2[j]) + (row(rw_a0[j]),),
               _pad_rank(rw_g1[j], rw_g2[j]) + (jnp.zeros((1, d), F32),)]
    if j > 0:
        branches.append((3, "none", "sigmoid"))
        weights.append(_pad_rank(rw_v1[j - 1], rw_v2[j - 1]) + (row(rw_v0[j - 1]),))
    outs = _lora(h3, g, rw_mix[j], tuple(branches), weights)
    lw, ag, gg = outs[:3]
    vres = (v_first, outs[3]) if j > 0 else None
    z = _rw_scan(rkv, lw, ag, gg, vres, pairs(rw_k_k[j]), pairs(rw_k_a[j]), pairs(rw_r_k[j]),
                 pairs(rw_ln_w[j]), pairs(rw_ln_b[j]), batch)
    h = _proj_residual(z, rw_w_o, h, pair_major=True)
    return h, rkv, cast


def _gla_layer(h, batch, j, norm_g, gla_w_in_bf, gla_w_a2, gla_b_a, gla_gn_w, gla_w_o):
    m, d = h.shape
    dk = d // 2
    n_main = 2 * dk + 2 * d
    rank = gla_w_in_bf.shape[1] - n_main
    g = norm_g.reshape(1, d)
    wz = jnp.pad(gla_w_in_bf[0, n_main:, :].T, ((0, 0), (0, LANES - rank)))
    wa = jnp.pad(gla_w_a2[j], ((0, LANES - rank), (0, 0)))
    p, glog = _gla_in(h, g, gla_w_in_bf, n_main, wz, wa, gla_b_a[j].reshape(1, dk))
    z = _gla_chunk(p, glog, gla_gn_w[j].reshape(1, d), batch, d)
    return _proj_residual(z, gla_w_o, h, pair_major=False)


def kernel(x, meta, norm_mix, norm_mlp, norm_f, mlp_w1, mlp_w2, rw_mix, rw_w_rkv, rw_w0, rw_w1, rw_w2, rw_a0, rw_a1, rw_a2, rw_v0, rw_v1, rw_v2, rw_g1, rw_g2, rw_k_k, rw_k_a, rw_r_k, rw_ln_w, rw_ln_b, rw_w_o, gla_w_in, gla_w_a2, gla_b_a, gla_gn_w, gla_w_o):
    batch, seq, d = x.shape
    depth = norm_mix.shape[0]
    assert seq % CHUNK == 0 and d % (2 * LANES) == 0 and meta.shape[0] == N_META
    tp = LEAD + seq
    lead = jnp.concatenate([jnp.zeros((N_DUMMY, d), x.dtype), meta.astype(x.dtype)], axis=0)
    h = jnp.concatenate([jnp.broadcast_to(lead[None], (batch, LEAD, d)), x], axis=1).reshape(batch * tp, d)
    gf = norm_f.reshape(1, d)
    rw_w_rkv2 = rw_w_rkv.reshape(rw_w_rkv.shape[0], 3 * d, d)
    gla_w_in_t = jnp.transpose(gla_w_in, (0, 2, 1))

    def mixer_weights(i):
        j = i // 2
        mixer = [(rw_w_rkv2, j), (rw_w_o, j)] if i % 2 == 0 else [(gla_w_in_t, j), (gla_w_o, j)]
        return mixer + [(mlp_w1, i), (mlp_w2, i)]

    w_first, first_layer = mixer_weights(0)[0]
    cast = [w_first[first_layer:first_layer + 1].astype(BF16), None, None, None]
    v_first = None
    for i in range(depth):
        j = i // 2
        w_a, w_b, w1_bf, w2_bf = cast
        if i % 2 == 0:
            h, rkv, early = _rwkv_layer(h, batch, j, v_first, norm_mix[i], rw_mix, w_a.reshape(1, 3, d, d), rw_w0,
                                        rw_w1, rw_w2, rw_a0, rw_a1, rw_a2, rw_v0, rw_v1, rw_v2, rw_g1, rw_g2,
                                        rw_k_k, rw_k_a, rw_r_k.reshape(rw_r_k.shape[0], d), rw_ln_w, rw_ln_b, w_b,
                                        sides=mixer_weights(0)[1:] if i == 0 else ())
            if i == 0:
                _, w1_bf, w2_bf = early
            if j == 0:
                v_first = rkv
        else:
            h = _gla_layer(h, batch, j, norm_mix[i], w_a, gla_w_a2, gla_b_a, gla_gn_w, w_b)
        g_mlp = norm_mlp[i].reshape(1, d)
        if i == depth - 1:
            return _mlp_final(h.reshape(batch, tp, d), g_mlp, w1_bf, w2_bf, gf, *MLP_FINAL_TILE)
        h, cast = _mlp(h, g_mlp, w1_bf, w2_bf, gf, *MLP_TILE, sides=mixer_weights(i + 1))
```

```python
import functools
import math

import jax
import jax.numpy as jnp
from jax import lax
from jax.experimental import pallas as pl
from jax.experimental.pallas import tpu as pltpu

F32 = jnp.float32
BF16 = jnp.bfloat16

N_META = 16
CHUNK = 64
LEAD = CHUNK
N_DUMMY = LEAD - N_META
NORM_EPS = 1e-6

RW_HEAD = 64
LANES = 128
RW_GN_EPS = 64e-5
RW_DECAY_SCALE = 0.6065306597126334
PAIR_GROUP = 16
INV_BASE_BLOCK = 4

GLA_HEADS = 4
GLA_TAU = 16.0
GLA_HEAD_EPS = 1e-5
GLA_SEQ_GROUP = 4

V7X_VMEM_BYTES = 64 * 1024 * 1024
VMEM_REQUEST_CAP = V7X_VMEM_BYTES - 8 * 1024 * 1024

MLP_TILE = (832, 1024)
MLP_FINAL_TILE = (512, 1024)
RKV_ROWS = 208
LORA_ROWS = 416
PROJ_ROWS = 640
GLA_IN_TILE = (640, 2048)


def _nbytes(shape, dtype):
    return math.prod(shape) * jnp.dtype(dtype).itemsize


def _cparams(sem, moving, fixed=(), live=0):
    need = 2 * sum(moving) + sum(fixed) + live
    return pltpu.CompilerParams(dimension_semantics=sem, vmem_limit_bytes=min(need, VMEM_REQUEST_CAP))


def _dot(a, b):
    return jnp.dot(a.astype(BF16), b.astype(BF16), preferred_element_type=F32)


def _dot_nt(a, b):
    return lax.dot_general(a.astype(BF16), b.astype(BF16), (((1,), (1,)), ((), ())),
                           preferred_element_type=F32)


def _dot_tn(a, b):
    return lax.dot_general(a.astype(BF16), b.astype(BF16), (((0,), (0,)), ((), ())),
                           preferred_element_type=F32)


def _split2(x):
    hi = x.astype(BF16)
    return hi, (x - hi.astype(F32)).astype(BF16)


def _dot_sel_left(sel, x):
    hi, lo = _split2(x)
    return jnp.dot(sel, hi, preferred_element_type=F32) + jnp.dot(sel, lo, preferred_element_type=F32)


def _rms(x, g, eps):
    return x * lax.rsqrt(jnp.mean(x * x, axis=-1, keepdims=True) + eps) * g


def _sigmoid(x):
    return 1.0 / (1.0 + jnp.exp(-x))


def _pick_tile(n, target):
    best = None
    for t in range(16, min(n, target) + 1, 16):
        if n % t == 0:
            best = t
    assert best is not None, (n, target)
    return best


def _side_cast_specs(sides, n_steps, step_of):
    in_specs, out_specs, out_shapes, moving = [], [], [], []
    for arr, layer in sides:
        _, rows, cols = arr.shape
        blk = next(b for b in range(16, rows + 1, 16) if rows % b == 0 and rows // b <= n_steps)
        last = rows // blk - 1
        in_specs.append(pl.BlockSpec(
            (None, blk, cols), lambda *ids, layer=layer, last=last: (layer, jnp.minimum(step_of(*ids), last), 0)))
        out_specs.append(pl.BlockSpec(
            (None, blk, cols), lambda *ids, last=last: (0, jnp.minimum(step_of(*ids), last), 0)))
        out_shapes.append(jax.ShapeDtypeStruct((1, rows, cols), BF16))
        moving += [_nbytes((blk, cols), F32), _nbytes((blk, cols), BF16)]
    return in_specs, out_specs, out_shapes, moving


def _mlp_kernel(*refs, final, n_side):
    h_ref, g_ref, w1_ref, w2_ref, gf_ref = refs[:5]
    side_in = refs[5:5 + n_side]
    o_ref = refs[5 + n_side]
    side_out = refs[6 + n_side:6 + 2 * n_side]
    xn_ref = refs[6 + 2 * n_side]
    ff_axis = 2 if final else 1
    if final:
        o_ref = o_ref.at[0]
    j = pl.program_id(ff_axis)

    @pl.when(j == 0)
    def _():
        x = h_ref[...]
        xn_ref[...] = _rms(x, g_ref[...], NORM_EPS).astype(BF16)
        o_ref[...] = x

    hid = jnp.dot(xn_ref[...], w1_ref[...], preferred_element_type=F32)
    hid = jnp.maximum(hid, 0.0)
    hid = hid * hid
    o_ref[...] += jnp.dot(hid.astype(BF16), w2_ref[...], preferred_element_type=F32)
    for src, dst in zip(side_in, side_out):
        dst[...] = src[...].astype(BF16)

    if final:
        @pl.when(j == pl.num_programs(ff_axis) - 1)
        def _():
            o_ref[...] = _rms(o_ref[...], gf_ref[...], NORM_EPS)


def _mlp(h, g, w1, w2, gf, tm_target, tf_target, sides=()):
    m, d = h.shape
    ff = w1.shape[2]
    tm = _pick_tile(m, tm_target)
    tf = _pick_tile(ff, tf_target)
    nj = ff // tf
    side_in, side_out, side_shapes, side_bytes = _side_cast_specs(
        sides, (m // tm) * nj, lambda i, j: i * nj + j)
    vmem = dict(moving=[2 * _nbytes((tm, d), F32), 2 * _nbytes((d, tf), BF16)] + side_bytes,
                fixed=[_nbytes((tm, d), BF16)], live=_nbytes((tm, tf), F32) + _nbytes((tm, tf), BF16))
    outs = pl.pallas_call(
        functools.partial(_mlp_kernel, final=False, n_side=len(sides)),
        grid=(m // tm, nj),
        in_specs=[
            pl.BlockSpec((tm, d), lambda i, j: (i, 0)),
            pl.BlockSpec((1, d), lambda i, j: (0, 0)),
            pl.BlockSpec((None, d, tf), lambda i, j: (0, 0, j)),
            pl.BlockSpec((None, tf, d), lambda i, j: (0, j, 0)),
            pl.BlockSpec((1, d), lambda i, j: (0, 0)),
        ] + side_in,
        out_specs=[pl.BlockSpec((tm, d), lambda i, j: (i, 0))] + side_out,
        out_shape=[jax.ShapeDtypeStruct((m, d), F32)] + side_shapes,
        scratch_shapes=[pltpu.VMEM((tm, d), BF16)],
        compiler_params=_cparams(("arbitrary", "arbitrary"), **vmem),
        name="mlp",
    )(h, g, w1, w2, gf, *[arr for arr, _ in sides])
    return outs[0], outs[1:]


def _mlp_final(h3, g, w1, w2, gf, tm_target, tf_target):
    b, tp, d = h3.shape
    seq = tp - LEAD
    ff = w1.shape[2]
    tm = _pick_tile(seq, tm_target)
    tf = _pick_tile(ff, tf_target)
    vmem = dict(moving=[2 * _nbytes((tm, d), F32), 2 * _nbytes((d, tf), BF16)],
                fixed=[_nbytes((tm, d), BF16)], live=_nbytes((tm, tf), F32) + _nbytes((tm, tf), BF16))
    return pl.pallas_call(
        functools.partial(_mlp_kernel, final=True, n_side=0),
        grid=(b, seq // tm, ff // tf),
        in_specs=[
            pl.BlockSpec((pl.Element(tm), pl.Element(d)),
                         lambda bi, i, j: (pl.multiple_of(bi * tp + LEAD + i * tm, 16), 0)),
            pl.BlockSpec((1, d), lambda bi, i, j: (0, 0)),
            pl.BlockSpec((None, d, tf), lambda bi, i, j: (0, 0, j)),
            pl.BlockSpec((None, tf, d), lambda bi, i, j: (0, j, 0)),
            pl.BlockSpec((1, d), lambda bi, i, j: (0, 0)),
        ],
        out_specs=pl.BlockSpec((1, tm, d), lambda bi, i, j: (bi, i, 0)),
        out_shape=jax.ShapeDtypeStruct((b, seq, d), F32),
        scratch_shapes=[pltpu.VMEM((tm, d), BF16)],
        compiler_params=_cparams(("parallel", "parallel", "arbitrary"), **vmem),
        name="mlp_final",
    )(h3.reshape(b * tp, d), g, w1, w2, gf)


def _shifted_norm(h_ref, hp_ref, g_ref):
    g = g_ref[...]
    hn = _rms(h_ref[0], g, NORM_EPS)
    pn = _rms(hp_ref[0], g, NORM_EPS)[7:8]
    pn = jnp.where(pl.program_id(1) == 0, 0.0, pn)
    row = lax.broadcasted_iota(jnp.int32, hn.shape, 0)
    prev = jnp.where(row == 0, pn, pltpu.roll(hn, 1, axis=0))
    return hn, prev - hn


def _to_pairs(o_ref, lead, val):
    for p in range(val.shape[1] // LANES):
        o_ref[lead + (p,)] = val[:, p * LANES:(p + 1) * LANES].astype(o_ref.dtype)


def _shift_specs(tm, d):
    nb = tm // 8
    return [
        pl.BlockSpec((1, tm, d), lambda bi, i: (bi, i, 0)),
        pl.BlockSpec((1, 8, d), lambda bi, i: (bi, jnp.maximum(i * nb - 1, 0), 0)),
        pl.BlockSpec((1, d), lambda bi, i: (0, 0)),
        pl.BlockSpec((6, d), lambda bi, i: (0, 0)),
    ]


RKV_MIX_ROWS = (0, 2, 3)


def _rkv_kernel(*refs, n_side):
    h_ref, hp_ref, g_ref, mix_ref, w_ref = refs[:5]
    side_in = refs[5:5 + n_side]
    o_ref = refs[5 + n_side]
    side_out = refs[6 + n_side:]
    hn, xx = _shifted_norm(h_ref, hp_ref, g_ref)
    for s, mrow in enumerate(RKV_MIX_ROWS):
        x = (hn + xx * mix_ref[mrow:mrow + 1, :]).astype(BF16)
        _to_pairs(o_ref, (s,), jnp.dot(x, w_ref[s], preferred_element_type=F32))
    for src, dst in zip(side_in, side_out):
        dst[...] = src[...].astype(BF16)


def _rkv(h3, g, mix, w, sides=()):
    b, tp, d = h3.shape
    tm = _pick_tile(tp, RKV_ROWS)
    nt = tp // tm
    npair = d // LANES
    side_in, side_out, side_shapes, side_bytes = _side_cast_specs(sides, b * nt, lambda bi, i: bi * nt + i)
    vmem = dict(moving=[_nbytes((tm + 8, d), F32), 3 * _nbytes((tm, d), F32)] + side_bytes,
                fixed=[_nbytes((3, d, d), BF16)], live=4 * _nbytes((tm, d), F32))
    outs = pl.pallas_call(
        functools.partial(_rkv_kernel, n_side=len(sides)),
        grid=(b, nt),
        in_specs=_shift_specs(tm, d) + [
            pl.BlockSpec((None, 3, d, d), lambda bi, i: (0, 0, 0, 0), pipeline_mode=pl.Buffered(1)),
        ] + side_in,
        out_specs=[pl.BlockSpec((3, npair, tm, LANES), lambda bi, i: (0, 0, bi * nt + i, 0))] + side_out,
        out_shape=[jax.ShapeDtypeStruct((3, npair, b * tp, LANES), F32)] + side_shapes,
        compiler_params=_cparams(("arbitrary", "arbitrary"), **vmem),
        name="rw_rkv",
    )(h3, h3, g, mix, w, *[arr for arr, _ in sides])
    return outs[0], outs[1:]


def _lora_kernel(*refs, branches):
    h_ref, hp_ref, g_ref, mix_ref = refs[:4]
    nb = len(branches)
    w_refs = refs[4:4 + 3 * nb]
    o_refs = refs[4 + 3 * nb:]
    hn, xx = _shifted_norm(h_ref, hp_ref, g_ref)
    for bi, (mrow, mid, out) in enumerate(branches):
        w1_ref, w2_ref, b_ref = w_refs[3 * bi:3 * bi + 3]
        x = (hn + xx * mix_ref[mrow:mrow + 1, :]).astype(BF16)
        z = jnp.dot(x, w1_ref[...], preferred_element_type=F32)
        if mid == "tanh":
            z = jnp.tanh(z)
        elif mid == "sigmoid":
            z = _sigmoid(z)
        y = jnp.dot(z.astype(BF16), w2_ref[...], preferred_element_type=F32)
        if out == "decay":
            y = -RW_DECAY_SCALE * _sigmoid(y + b_ref[...])
        elif out == "sigmoid":
            y = _sigmoid(y + b_ref[...])
        _to_pairs(o_refs[bi], (), y)


def _lora(h3, g, mix, branches, weights):
    b, tp, d = h3.shape
    tm = _pick_tile(tp, LORA_ROWS)
    nt = tp // tm
    npair = d // LANES
    w_specs, w_args = [], []
    for w1, w2, bias in weights:
        r = w1.shape[1]
        w_specs += [pl.BlockSpec((d, r), lambda bi, i: (0, 0)),
                    pl.BlockSpec((r, d), lambda bi, i: (0, 0)),
                    pl.BlockSpec((1, d), lambda bi, i: (0, 0))]
        w_args += [w1, w2, bias]
    out_spec = pl.BlockSpec((npair, tm, LANES), lambda bi, i: (0, bi * nt + i, 0))
    out_shape = jax.ShapeDtypeStruct((npair, b * tp, LANES), F32)
    vmem = dict(moving=[_nbytes((tm + 8, d), F32)] + [_nbytes((tm, d), F32)] * len(branches),
                fixed=[_nbytes(w.shape, w.dtype) for w in w_args], live=4 * _nbytes((tm, d), F32))
    return pl.pallas_call(
        functools.partial(_lora_kernel, branches=branches),
        grid=(b, nt),
        in_specs=_shift_specs(tm, d) + w_specs,
        out_specs=[out_spec] * len(branches),
        out_shape=[out_shape] * len(branches),
        compiler_params=_cparams(("parallel", "parallel"), **vmem),
        name="rw_lora",
    )(h3, h3, g, mix, *w_args)


def _pad_rank(w1, w2):
    r = w1.shape[1]
    rp = -(-r // LANES) * LANES
    return (jnp.pad(w1, ((0, 0), (0, rp - r))).astype(BF16),
            jnp.pad(w2, ((0, rp - r), (0, 0))).astype(BF16))


def _stack_heads(x, lane_lo):
    return jnp.concatenate([jnp.where(lane_lo, x, 0.0), jnp.where(lane_lo, 0.0, x)], axis=0)


def _rw_scan_kernel(*refs, has_vres):
    if has_vres:
        (rkv_ref, lw_ref, ag_ref, gg_ref, vf_ref, vg_ref,
         kk_ref, ka_ref, rk_ref, lnw_ref, lnb_ref, z_ref, st_ref) = refs
    else:
        (rkv_ref, lw_ref, ag_ref, gg_ref,
         kk_ref, ka_ref, rk_ref, lnw_ref, lnb_ref, z_ref, st_ref) = refs
    c = pl.program_id(1)
    npair = z_ref.shape[0]
    group = min(PAIR_GROUP, npair)
    assert npair % group == 0
    two_l = 2 * CHUNK

    @pl.when(c == 0)
    def _():
        st_ref[...] = jnp.zeros_like(st_ref)

    row = lax.broadcasted_iota(jnp.int32, (two_l, two_l), 0)
    col = lax.broadcasted_iota(jnp.int32, (two_l, two_l), 1)
    t_half = lax.broadcasted_iota(jnp.int32, (CHUNK, two_l), 0)
    s_half = lax.broadcasted_iota(jnp.int32, (CHUNK, two_l), 1) % CHUNK
    strict_t = t_half > s_half
    incl_t = t_half >= s_half
    same_head = (row // CHUNK) == (col // CHUNK)
    tril = jnp.where(lax.broadcasted_iota(jnp.int32, (CHUNK, CHUNK), 0)
                     >= lax.broadcasted_iota(jnp.int32, (CHUNK, CHUNK), 1), 1.0, 0.0).astype(BF16)
    diag = row == col
    same_block = lambda n: (row // n) == (col // n)
    base_mask = same_block(INV_BASE_BLOCK)
    merge_masks = [jnp.logical_and(same_block(2 * n), jnp.logical_not(same_block(n)))
                   for n in (INV_BASE_BLOCK << i for i in range((CHUNK // INV_BASE_BLOCK).bit_length() - 1))]
    lane_lo = lax.broadcasted_iota(jnp.int32, (CHUNK, LANES), 1) < RW_HEAD
    live = jnp.logical_or(c > 0, lax.broadcasted_iota(jnp.int32, (CHUNK, LANES), 0) >= N_DUMMY)

    def each(fn, *lists):
        return [fn(*xs) for xs in zip(*lists)]

    ones_bd = jnp.where(same_head, 1.0, 0.0).astype(BF16)

    def head_sum_lanes(x):
        lo = jnp.sum(jnp.where(lane_lo, x, 0.0), axis=-1, keepdims=True)
        hi = jnp.sum(jnp.where(lane_lo, 0.0, x), axis=-1, keepdims=True)
        return jnp.where(lane_lo, lo, hi)

    def head_sum_matmul(x):
        hi, lo = _split2(x)
        both = jnp.dot(jnp.concatenate([hi, lo], axis=0), ones_bd, preferred_element_type=F32)
        return both[:x.shape[0]] + both[x.shape[0]:]

    def swap_halves(x):
        return pltpu.roll(x, RW_HEAD, axis=1)

    def group_body(gi, carry):
        ps = [gi * group + i for i in range(group)]
        r = [rkv_ref[0, p] for p in ps]
        k0 = [rkv_ref[1, p] for p in ps]
        v = [rkv_ref[2, p] for p in ps]
        lw = [lw_ref[p] for p in ps]
        ag = [ag_ref[p] for p in ps]
        if has_vres:
            v = [vi + (vf_ref[0, p] - vi) * vg_ref[p] for vi, p in zip(v, ps)]
        kk = [ki * kk_ref[p] for ki, p in zip(k0, ps)]
        k = [ki * (1.0 + (ai - 1.0) * ka_ref[p]) for ki, ai, p in zip(k0, ag, ps)]
        kk = each(lambda kki: kki / jnp.maximum(jnp.sqrt(head_sum_lanes(kki * kki)), 1e-12), kk)
        bonus = [head_sum_lanes(ri * ki * rk_ref[p]) * vi for ri, ki, vi, p in zip(r, k, v, ps)]

        cs = each(lambda lwi: _dot_sel_left(tril, lwi), lw)
        e_pos = each(jnp.exp, cs)
        e_neg = each(lambda ci: jnp.exp(-ci), cs)
        w_all = each(lambda ei: ei[CHUNK - 1:CHUNK, :], e_pos)
        a_t = each(lambda kki, ci, lwi: -kki * jnp.exp(ci - lwi), kk, cs, lw)
        r_t = each(lambda ri, ei: ri * ei, r, e_pos)
        b_t = each(lambda kki, ai, ei: kki * ai * ei, kk, ag, e_neg)
        k_t = each(lambda ki, ei: ki * ei, k, e_neg)
        v_s = each(lambda vi: _stack_heads(vi, lane_lo).astype(BF16), v)
        b_s = each(lambda bti: _stack_heads(bti, lane_lo), b_t)
        k_s = each(lambda kti: _stack_heads(kti, lane_lo), k_t)
        bh_t = each(lambda bsi, wi: (bsi * wi).T.astype(BF16), b_s, w_all)
        kh_t = each(lambda ksi, wi: (ksi * wi).T.astype(BF16), k_s, w_all)

        gram = each(lambda ati, rti, bsi, ksi: _dot_nt(
            jnp.concatenate([ati, rti], axis=0), jnp.concatenate([bsi, ksi], axis=0)), a_t, r_t, b_s, k_s)

        block = lambda gm, keep, rows, cols: jnp.where(
            keep, gm[rows * CHUNK:(rows + 1) * CHUNK, cols * two_l:(cols + 1) * two_l], 0.0)
        p_c = each(lambda gm: block(gm, strict_t, 0, 0), gram)
        a_ak = each(lambda gm: block(gm, strict_t, 0, 1).astype(BF16), gram)
        a_rb = each(lambda gm: _stack_heads(block(gm, incl_t, 1, 0), lane_lo).astype(BF16), gram)
        a_rk = each(lambda gm: block(gm, incl_t, 1, 1).astype(BF16), gram)

        xv = each(lambda aki, ari, khi, vsi: jnp.dot(jnp.concatenate([aki, ari, khi], axis=0), vsi,
                                                     preferred_element_type=F32), a_ak, a_rk, kh_t, v_s)

        def z_init(ati, xi):
            su = swap_halves(xi[:CHUNK])
            return jnp.concatenate([jnp.where(lane_lo, ati, su), jnp.where(lane_lo, su, ati)], axis=0)

        z0 = each(z_init, a_t, xv)
        p_bd = each(lambda pci: _stack_heads(pci, lane_lo), p_c)
        d_bf = each(lambda pi: jnp.where(base_mask, pi, 0.0).astype(BF16), p_bd)
        d_sq = each(lambda di: jnp.dot(di, di, preferred_element_type=F32), d_bf)
        t_inv = each(lambda di, qi: jnp.dot(jnp.where(diag, 1.0, di.astype(F32)).astype(BF16),
                                            jnp.where(diag, 1.0, qi).astype(BF16),
                                            preferred_element_type=F32), d_bf, d_sq)
        for merge_mask in merge_masks:
            below = each(lambda pi: jnp.where(merge_mask, pi, 0.0).astype(BF16), p_bd)
            t_bf = each(lambda ti: ti.astype(BF16), t_inv)
            lt = each(lambda li, ti: jnp.dot(li, ti, preferred_element_type=F32).astype(BF16), below, t_bf)
            t_inv = each(lambda ti, tbi, lti: ti + jnp.dot(tbi, lti, preferred_element_type=F32), t_inv, t_bf, lt)
        zb = each(lambda ti, zi: jnp.dot(ti.astype(BF16), zi.astype(BF16), preferred_element_type=F32).astype(BF16),
                  t_inv, z0)

        rb = each(lambda ai, bhi, zi: jnp.dot(jnp.concatenate([ai, bhi], axis=0), zi,
                                              preferred_element_type=F32), a_rb, bh_t, zb)
        r_hat = each(lambda rti, rbi: rti + jnp.where(lane_lo, rbi[:CHUNK], rbi[CHUNK:two_l]), r_t, rb)
        y_hat = each(lambda rbi, xi: swap_halves(jnp.where(lane_lo, rbi[CHUNK:two_l], rbi[:CHUNK]))
                     + xi[CHUNK:two_l], rb, xv)
        m_mat = each(lambda wi, rbi: jnp.where(diag, wi, 0.0) + jnp.where(same_head, rbi[two_l:], 0.0),
                     w_all, rb)
        n_mat = each(lambda rbi, xi: swap_halves(jnp.where(same_head, 0.0, rbi[two_l:])) + xi[two_l:],
                     rb, xv)

        hb = [st_ref[p].astype(BF16) for p in ps]
        ys = each(lambda rh, mm, hi: jnp.dot(jnp.concatenate([rh, mm], axis=0).astype(BF16), hi,
                                             preferred_element_type=F32), r_hat, m_mat, hb)
        for p, ysi, nm in zip(ps, ys, n_mat):
            st_ref[p] = ysi[CHUNK:] + nm
        y = each(lambda ysi, yh: ysi[:CHUNK] + yh, ys, y_hat)

        dy = each(lambda yi: yi - head_sum_matmul(yi) * (1.0 / RW_HEAD), y)
        var = each(lambda di: head_sum_matmul(di * di) * (1.0 / RW_HEAD), dy)
        for p, di, vi, bi in zip(ps, dy, var, bonus):
            yn = di * lax.rsqrt(vi + RW_GN_EPS) * lnw_ref[p] + lnb_ref[p]
            z_ref[p] = jnp.where(live, (yn + bi) * gg_ref[p], 0.0).astype(z_ref.dtype)
        return carry

    lax.fori_loop(0, npair // group, group_body, 0)


def _rw_scan(rkv, lw, ag, gg, vres, kk, ka, rk, lnw, lnb, batch):
    _, npair, m, _ = rkv.shape
    nc = m // batch // CHUNK
    row_map3 = lambda bi, c: (0, bi * nc + c, 0)
    row_map4 = lambda bi, c: (0, 0, bi * nc + c, 0)
    par = pl.BlockSpec((npair, 1, LANES), lambda bi, c: (0, 0, 0))
    tile = pl.BlockSpec((npair, CHUNK, LANES), row_map3)
    in_specs = [pl.BlockSpec((3, npair, CHUNK, LANES), row_map4), tile, tile, tile]
    args = [rkv, lw, ag, gg]
    if vres is not None:
        vfirst, vgate = vres
        in_specs += [pl.BlockSpec((1, npair, CHUNK, LANES), lambda bi, c: (2, 0, bi * nc + c, 0)), tile]
        args += [vfirst, vgate]
    in_specs += [par] * 5
    args += [kk, ka, rk, lnw, lnb]
    tile_bytes = _nbytes((npair, CHUNK, LANES), F32)
    n_tiles = 3 + 3 + (2 if vres is not None else 0) + 1
    state_bytes = _nbytes((npair, 2 * CHUNK, LANES), F32)
    vmem = dict(moving=[n_tiles * tile_bytes], fixed=[state_bytes], live=8 * state_bytes)
    return pl.pallas_call(
        functools.partial(_rw_scan_kernel, has_vres=vres is not None),
        grid=(batch, nc),
        in_specs=in_specs,
        out_specs=tile,
        out_shape=jax.ShapeDtypeStruct((npair, m, LANES), BF16),
        scratch_shapes=[pltpu.VMEM((npair, 2 * CHUNK, LANES), F32)],
        compiler_params=_cparams(("parallel", "arbitrary"), **vmem),
        name="rw_scan",
    )(*args)


def _proj_kernel(x_ref, w_ref, h_ref, o_ref, *, pair_major):
    if pair_major:
        x = jnp.concatenate([x_ref[p] for p in range(x_ref.shape[0])], axis=1)
    else:
        x = x_ref[...]
    o_ref[...] = h_ref[...] + jnp.dot(x, w_ref[...], preferred_element_type=F32)


def _proj_residual(x, w, h, pair_major):
    m, d = h.shape
    kdim = w.shape[1]
    tm = _pick_tile(m, PROJ_ROWS)
    vmem = dict(moving=[_nbytes((tm, kdim), BF16), 2 * _nbytes((tm, d), F32)], fixed=[_nbytes((kdim, d), BF16)],
                live=_nbytes((tm, d), F32))
    if pair_major:
        x_spec = pl.BlockSpec((x.shape[0], tm, LANES), lambda i: (0, i, 0))
    else:
        x_spec = pl.BlockSpec((tm, kdim), lambda i: (i, 0))
    return pl.pallas_call(
        functools.partial(_proj_kernel, pair_major=pair_major),
        grid=(m // tm,),
        in_specs=[x_spec,
                  pl.BlockSpec((None, kdim, d), lambda i: (0, 0, 0)),
                  pl.BlockSpec((tm, d), lambda i: (i, 0))],
        out_specs=pl.BlockSpec((tm, d), lambda i: (i, 0)),
        out_shape=jax.ShapeDtypeStruct((m, d), F32),
        compiler_params=_cparams(("parallel",), **vmem),
        name="proj_residual",
    )(x, w, h)


def _gla_in_kernel(h_ref, g_ref, w_ref, wz_ref, wa_ref, ba_ref, p_ref, gl_ref, xn_ref):
    @pl.when(pl.program_id(1) == 0)
    def _():
        xn = _rms(h_ref[...], g_ref[...], NORM_EPS).astype(BF16)
        xn_ref[...] = xn
        za = jnp.dot(xn, wz_ref[...], preferred_element_type=F32)
        u = _dot(za, wa_ref[...]) + ba_ref[...]
        gl_ref[...] = (jnp.minimum(u, 0.0) - jnp.log1p(jnp.exp(-jnp.abs(u)))) * (1.0 / GLA_TAU)

    p_ref[...] = _dot_nt(xn_ref[...], w_ref[...]).astype(p_ref.dtype)


def _gla_in(h, g, w, n, wz, wa, ba):
    m, d = h.shape
    dk = wa.shape[1]
    tm = _pick_tile(m, GLA_IN_TILE[0])
    tn = _pick_tile(n, GLA_IN_TILE[1])
    vmem = dict(moving=[_nbytes((tm, d), F32), _nbytes((tn, d), BF16), _nbytes((tm, tn), BF16),
                        _nbytes((tm, dk), F32)],
                fixed=[_nbytes((tm, d), BF16), _nbytes((d, LANES), BF16), _nbytes((LANES, dk), F32)],
                live=_nbytes((tm, tn), F32) + 2 * _nbytes((tm, dk), F32))
    return pl.pallas_call(
        _gla_in_kernel,
        grid=(m // tm, n // tn),
        in_specs=[
            pl.BlockSpec((tm, d), lambda i, j: (i, 0)),
            pl.BlockSpec((1, d), lambda i, j: (0, 0)),
            pl.BlockSpec((None, tn, d), lambda i, j: (0, j, 0)),
            pl.BlockSpec((d, LANES), lambda i, j: (0, 0)),
            pl.BlockSpec((LANES, dk), lambda i, j: (0, 0)),
            pl.BlockSpec((1, dk), lambda i, j: (0, 0)),
        ],
        out_specs=[pl.BlockSpec((tm, tn), lambda i, j: (i, j)),
                   pl.BlockSpec((tm, dk), lambda i, j: (i, 0))],
        out_shape=[jax.ShapeDtypeStruct((m, n), BF16), jax.ShapeDtypeStruct((m, dk), F32)],
        scratch_shapes=[pltpu.VMEM((tm, d), BF16)],
        compiler_params=_cparams(("parallel", "arbitrary"), **vmem),
        name="gla_in",
    )(h, g, w, wz, wa, ba)


def _gla_chunk_kernel(q_ref, k_ref, v_ref, gate_ref, gl_ref, gn_ref, z_ref, st_ref, *, scale):
    c = pl.program_id(1)
    nb, nh, hv, hk = st_ref.shape

    @pl.when(c == 0)
    def _():
        st_ref[...] = jnp.zeros_like(st_ref)

    row = lax.broadcasted_iota(jnp.int32, (CHUNK, CHUNK), 0)
    col = lax.broadcasted_iota(jnp.int32, (CHUNK, CHUNK), 1)
    causal = row >= col
    tril = jnp.where(causal, 1.0, 0.0).astype(BF16)
    live = jnp.logical_or(c > 0, lax.broadcasted_iota(jnp.int32, gl_ref.shape[1:], 0) >= N_DUMMY)
    ksl = lambda x, h: x[:, h * hk:(h + 1) * hk]
    vsl = lambda x, h: x[:, h * hv:(h + 1) * hv]
    seqs = range(nb)
    cells = [(b, h) for b in seqs for h in range(nh)]

    gl = [jnp.where(live, gl_ref[b], 0.0) for b in seqs]
    bc = [_dot_sel_left(tril, gl[b]) for b in seqs]
    b_last = [bc[b][CHUNK - 1:CHUNK, :] for b in seqs]
    e_last = [jnp.exp(b_last[b]) for b in seqs]
    q_t = [(q_ref[b] * scale * jnp.exp(bc[b])).astype(BF16) for b in seqs]
    k_t = [(k_ref[b] * jnp.exp(-bc[b])).astype(BF16) for b in seqs]
    k_h = [(k_ref[b] * jnp.exp(b_last[b] - bc[b])).astype(BF16) for b in seqs]
    v = {(b, h): vsl(v_ref[b], h).astype(BF16) for b, h in cells}
    att = {(b, h): jnp.where(causal, _dot_nt(ksl(q_t[b], h), ksl(k_t[b], h)), 0.0).astype(BF16) for b, h in cells}
    st = {(b, h): st_ref[b, h] for b, h in cells}
    o = {(b, h): _dot_nt(ksl(q_t[b], h), st[b, h]) + jnp.dot(att[b, h], v[b, h], preferred_element_type=F32)
         for b, h in cells}
    for b, h in cells:
        st_ref[b, h] = st[b, h] * ksl(e_last[b], h) + _dot_tn(v[b, h], ksl(k_h[b], h))
    for b, h in cells:
        on = o[b, h] * lax.rsqrt(jnp.mean(o[b, h] * o[b, h], axis=-1, keepdims=True) + GLA_HEAD_EPS)
        gate = vsl(gate_ref[b], h).astype(F32)
        z_ref[b, :, h * hv:(h + 1) * hv] = (on * vsl(gn_ref[...], h) * (gate * _sigmoid(gate))).astype(z_ref.dtype)


def _gla_chunk(p, glog, gn_w, batch, d):
    tp = p.shape[0] // batch
    nc = tp // CHUNK
    dk = d // 2
    hk = dk // GLA_HEADS
    hv = d // GLA_HEADS
    p3 = p.reshape(batch, tp, p.shape[1])
    nb = GLA_SEQ_GROUP if batch % GLA_SEQ_GROUP == 0 else 1
    state_bytes = _nbytes((nb, GLA_HEADS, hv, hk), F32)
    vmem = dict(moving=[2 * _nbytes((nb, CHUNK, dk), BF16), 3 * _nbytes((nb, CHUNK, d), BF16),
                        _nbytes((nb, CHUNK, dk), F32)], fixed=[state_bytes], live=state_bytes)
    z = pl.pallas_call(
        functools.partial(_gla_chunk_kernel, scale=hk ** -0.5),
        grid=(batch // nb, nc),
        in_specs=[
            pl.BlockSpec((nb, CHUNK, dk), lambda g, c: (g, c, 0)),
            pl.BlockSpec((nb, CHUNK, dk), lambda g, c: (g, c, 1)),
            pl.BlockSpec((nb, CHUNK, d), lambda g, c: (g, c, 1)),
            pl.BlockSpec((nb, CHUNK, d), lambda g, c: (g, c, 2)),
            pl.BlockSpec((nb, CHUNK, dk), lambda g, c: (g, c, 0)),
            pl.BlockSpec((1, d), lambda g, c: (0, 0)),
        ],
        out_specs=pl.BlockSpec((nb, CHUNK, d), lambda g, c: (g, c, 0)),
        out_shape=jax.ShapeDtypeStruct((batch, tp, d), BF16),
        scratch_shapes=[pltpu.VMEM((nb, GLA_HEADS, hv, hk), F32)],
        compiler_params=_cparams(("parallel", "arbitrary"), **vmem),
        name="gla_chunk",
    )(p3, p3, p3, p3, glog.reshape(batch, tp, dk), gn_w)
    return z.reshape(batch * tp, d)


def _rwkv_layer(h, batch, j, v_first, norm_g, rw_mix, rw_w_rkv, rw_w0, rw_w1, rw_w2, rw_a0, rw_a1, rw_a2,
                rw_v0, rw_v1, rw_v2, rw_g1, rw_g2, rw_k_k, rw_k_a, rw_r_k, rw_ln_w, rw_ln_b, rw_w_o,
                sides=()):
    m, d = h.shape
    npair = d // LANES
    row = lambda t: t.reshape(1, d)
    pairs = lambda t: t.reshape(npair, 1, LANES)
    h3 = h.reshape(batch, m // batch, d)
    g = row(norm_g)
    rkv, cast = _rkv(h3, g, rw_mix[j], rw_w_rkv, sides)
    if rw_w_o is None:
        rw_w_o = cast[0]
    branches = [(1, "tanh", "decay"), (4, "none", "sigmoid"), (5, "sigmoid", "none")]
    weights = [_pad_rank(rw_w1[j], rw_w2[j]) + (row(rw_w0[j]),),
               _pad_rank(rw_a1[j], rw_a2[j]) + (row(rw_a0[j]),),
               _pad_rank(rw_g1[j], rw_g2[j]) + (jnp.zeros((1, d), F32),)]
    if j > 0:
        branches.append((3, "none", "sigmoid"))
        weights.append(_pad_rank(rw_v1[j - 1], rw_v2[j - 1]) + (row(rw_v0[j - 1]),))
    outs = _lora(h3, g, rw_mix[j], tuple(branches), weights)
    lw, ag, gg = outs[:3]
    vres = (v_first, outs[3]) if j > 0 else None
    z = _rw_scan(rkv, lw, ag, gg, vres, pairs(rw_k_k[j]), pairs(rw_k_a[j]), pairs(rw_r_k[j]),
                 pairs(rw_ln_w[j]), pairs(rw_ln_b[j]), batch)
    h = _proj_residual(z, rw_w_o, h, pair_major=True)
    return h, rkv, cast


def _gla_layer(h, batch, j, norm_g, gla_w_in_bf, gla_w_a2, gla_b_a, gla_gn_w, gla_w_o):
    m, d = h.shape
    dk = d // 2
    n_main = 2 * dk + 2 * d
    rank = gla_w_in_bf.shape[1] - n_main
    g = norm_g.reshape(1, d)
    wz = jnp.pad(gla_w_in_bf[0, n_main:, :].T, ((0, 0), (0, LANES - rank)))
    wa = jnp.pad(gla_w_a2[j], ((0, LANES - rank), (0, 0)))
    p, glog = _gla_in(h, g, gla_w_in_bf, n_main, wz, wa, gla_b_a[j].reshape(1, dk))
    z = _gla_chunk(p, glog, gla_gn_w[j].reshape(1, d), batch, d)
    return _proj_residual(z, gla_w_o, h, pair_major=False)


def kernel(x, meta, norm_mix, norm_mlp, norm_f, mlp_w1, mlp_w2, rw_mix, rw_w_rkv, rw_w0, rw_w1, rw_w2, rw_a0, rw_a1, rw_a2, rw_v0, rw_v1, rw_v2, rw_g1, rw_g2, rw_k_k, rw_k_a, rw_r_k, rw_ln_w, rw_ln_b, rw_w_o, gla_w_in, gla_w_a2, gla_b_a, gla_gn_w, gla_w_o):
    batch, seq, d = x.shape
    depth = norm_mix.shape[0]
    assert seq % CHUNK == 0 and d % (2 * LANES) == 0 and meta.shape[0] == N_META
    tp = LEAD + seq
    lead = jnp.concatenate([jnp.zeros((N_DUMMY, d), x.dtype), meta.astype(x.dtype)], axis=0)
    h = jnp.concatenate([jnp.broadcast_to(lead[None], (batch, LEAD, d)), x], axis=1).reshape(batch * tp, d)
    gf = norm_f.reshape(1, d)
    rw_w_rkv2 = rw_w_rkv.reshape(rw_w_rkv.shape[0], 3 * d, d)
    gla_w_in_t = jnp.transpose(gla_w_in, (0, 2, 1))

    def mixer_weights(i):
        j = i // 2
        mixer = [(rw_w_rkv2, j), (rw_w_o, j)] if i % 2 == 0 else [(gla_w_in_t, j), (gla_w_o, j)]
        return mixer + [(mlp_w1, i), (mlp_w2, i)]

    w_first, first_layer = mixer_weights(0)[0]
    cast = [w_first[first_layer:first_layer + 1].astype(BF16), None, None, None]
    v_first = None
    for i in range(depth):
        j = i // 2
        w_a, w_b, w1_bf, w2_bf = cast
        if i % 2 == 0:
            h, rkv, early = _rwkv_layer(h, batch, j, v_first, norm_mix[i], rw_mix, w_a.reshape(1, 3, d, d), rw_w0,
                                        rw_w1, rw_w2, rw_a0, rw_a1, rw_a2, rw_v0, rw_v1, rw_v2, rw_g1, rw_g2,
                                        rw_k_k, rw_k_a, rw_r_k.reshape(rw_r_k.shape[0], d), rw_ln_w, rw_ln_b, w_b,
                                        sides=mixer_weights(0)[1:] if i == 0 else ())
            if i == 0:
                _, w1_bf, w2_bf = early
            if j == 0:
                v_first = rkv
        else:
            h = _gla_layer(h, batch, j, norm_mix[i], w_a, gla_w_a2, gla_b_a, gla_gn_w, w_b)
        g_mlp = norm_mlp[i].reshape(1, d)
        if i == depth - 1:
            return _mlp_final(h.reshape(batch, tp, d), g_mlp, w1_bf, w2_bf, gf, *MLP_FINAL_TILE)
        h, cast = _mlp(h, g_mlp, w1_bf, w2_bf, gf, *MLP_TILE, sides=mixer_weights(i + 1))
```

```python
import functools
import math

import jax
import jax.numpy as jnp
from jax import lax
from jax.experimental import pallas as pl
from jax.experimental.pallas import tpu as pltpu

F32 = jnp.float32
BF16 = jnp.bfloat16

N_META = 16
CHUNK = 64
LEAD = CHUNK
N_DUMMY = LEAD - N_META
NORM_EPS = 1e-6

RW_HEAD = 64
LANES = 128
RW_GN_EPS = 64e-5
RW_DECAY_SCALE = 0.6065306597126334
PAIR_GROUP = 16
INV_BASE_BLOCK = 4

GLA_HEADS = 4
GLA_TAU = 16.0
GLA_HEAD_EPS = 1e-5
GLA_SEQ_GROUP = 4

V7X_VMEM_BYTES = 64 * 1024 * 1024
VMEM_REQUEST_CAP = V7X_VMEM_BYTES - 8 * 1024 * 1024

MLP_TILE = (832, 1024)
MLP_FINAL_TILE = (512, 1024)
RKV_ROWS = 208
LORA_ROWS = 416
PROJ_ROWS = 640
GLA_IN_TILE = (640, 2048)


def _nbytes(shape, dtype):
    return math.prod(shape) * jnp.dtype(dtype).itemsize


def _cparams(sem, moving, fixed=(), live=0):
    need = 2 * sum(moving) + sum(fixed) + live
    return pltpu.CompilerParams(dimension_semantics=sem, vmem_limit_bytes=min(need, VMEM_REQUEST_CAP))


def _dot(a, b):
    return jnp.dot(a.astype(BF16), b.astype(BF16), preferred_element_type=F32)


def _dot_nt(a, b):
    return lax.dot_general(a.astype(BF16), b.astype(BF16), (((1,), (1,)), ((), ())),
                           preferred_element_type=F32)


def _dot_tn(a, b):
    return lax.dot_general(a.astype(BF16), b.astype(BF16), (((0,), (0,)), ((), ())),
                           preferred_element_type=F32)


def _split2(x):
    hi = x.astype(BF16)
    return hi, (x - hi.astype(F32)).astype(BF16)


def _dot_sel_left(sel, x):
    hi, lo = _split2(x)
    return jnp.dot(sel, hi, preferred_element_type=F32) + jnp.dot(sel, lo, preferred_element_type=F32)


def _rms(x, g, eps):
    return x * lax.rsqrt(jnp.mean(x * x, axis=-1, keepdims=True) + eps) * g


def _sigmoid(x):
    return 1.0 / (1.0 + jnp.exp(-x))


def _pick_tile(n, target):
    best = None
    for t in range(16, min(n, target) + 1, 16):
        if n % t == 0:
            best = t
    assert best is not None, (n, target)
    return best


def _side_cast_specs(sides, n_steps, step_of):
    in_specs, out_specs, out_shapes, moving = [], [], [], []
    for arr, layer in sides:
        _, rows, cols = arr.shape
        blk = next(b for b in range(16, rows + 1, 16) if rows % b == 0 and rows // b <= n_steps)
        last = rows // blk - 1
        in_specs.append(pl.BlockSpec(
            (None, blk, cols), lambda *ids, layer=layer, last=last: (layer, jnp.minimum(step_of(*ids), last), 0)))
        out_specs.append(pl.BlockSpec(
            (None, blk, cols), lambda *ids, last=last: (0, jnp.minimum(step_of(*ids), last), 0)))
        out_shapes.append(jax.ShapeDtypeStruct((1, rows, cols), BF16))
        moving += [_nbytes((blk, cols), F32), _nbytes((blk, cols), BF16)]
    return in_specs, out_specs, out_shapes, moving


def _mlp_kernel(*refs, final, n_side):
    h_ref, g_ref, w1_ref, w2_ref, gf_ref = refs[:5]
    side_in = refs[5:5 + n_side]
    o_ref = refs[5 + n_side]
    side_out = refs[6 + n_side:6 + 2 * n_side]
    xn_ref = refs[6 + 2 * n_side]
    ff_axis = 2 if final else 1
    if final:
        o_ref = o_ref.at[0]
    j = pl.program_id(ff_axis)

    @pl.when(j == 0)
    def _():
        x = h_ref[...]
        xn_ref[...] = _rms(x, g_ref[...], NORM_EPS).astype(BF16)
        o_ref[...] = x

    hid = jnp.dot(xn_ref[...], w1_ref[...], preferred_element_type=F32)
    hid = jnp.maximum(hid, 0.0)
    hid = hid * hid
    o_ref[...] += jnp.dot(hid.astype(BF16), w2_ref[...], preferred_element_type=F32)
    for src, dst in zip(side_in, side_out):
        dst[...] = src[...].astype(BF16)

    if final:
        @pl.when(j == pl.num_programs(ff_axis) - 1)
        def _():
            o_ref[...] = _rms(o_ref[...], gf_ref[...], NORM_EPS)


def _mlp(h, g, w1, w2, gf, tm_target, tf_target, sides=()):
    m, d = h.shape
    ff = w1.shape[2]
    tm = _pick_tile(m, tm_target)
    tf = _pick_tile(ff, tf_target)
    nj = ff // tf
    side_in, side_out, side_shapes, side_bytes = _side_cast_specs(
        sides, (m // tm) * nj, lambda i, j: i * nj + j)
    vmem = dict(moving=[2 * _nbytes((tm, d), F32), 2 * _nbytes((d, tf), BF16)] + side_bytes,
                fixed=[_nbytes((tm, d), BF16)], live=_nbytes((tm, tf), F32) + _nbytes((tm, tf), BF16))
    outs = pl.pallas_call(
        functools.partial(_mlp_kernel, final=False, n_side=len(sides)),
        grid=(m // tm, nj),
        in_specs=[
            pl.BlockSpec((tm, d), lambda i, j: (i, 0)),
            pl.BlockSpec((1, d), lambda i, j: (0, 0)),
            pl.BlockSpec((None, d, tf), lambda i, j: (0, 0, j)),
            pl.BlockSpec((None, tf, d), lambda i, j: (0, j, 0)),
            pl.BlockSpec((1, d), lambda i, j: (0, 0)),
        ] + side_in,
        out_specs=[pl.BlockSpec((tm, d), lambda i, j: (i, 0))] + side_out,
        out_shape=[jax.ShapeDtypeStruct((m, d), F32)] + side_shapes,
        scratch_shapes=[pltpu.VMEM((tm, d), BF16)],
        compiler_params=_cparams(("arbitrary", "arbitrary"), **vmem),
        name="mlp",
    )(h, g, w1, w2, gf, *[arr for arr, _ in sides])
    return outs[0], outs[1:]


def _mlp_final(h3, g, w1, w2, gf, tm_target, tf_target):
    b, tp, d = h3.shape
    seq = tp - LEAD
    ff = w1.shape[2]
    tm = _pick_tile(seq, tm_target)
    tf = _pick_tile(ff, tf_target)
    vmem = dict(moving=[2 * _nbytes((tm, d), F32), 2 * _nbytes((d, tf), BF16)],
                fixed=[_nbytes((tm, d), BF16)], live=_nbytes((tm, tf), F32) + _nbytes((tm, tf), BF16))
    return pl.pallas_call(
        functools.partial(_mlp_kernel, final=True, n_side=0),
        grid=(b, seq // tm, ff // tf),
        in_specs=[
            pl.BlockSpec((pl.Element(tm), pl.Element(d)),
                         lambda bi, i, j: (pl.multiple_of(bi * tp + LEAD + i * tm, 16), 0)),
            pl.BlockSpec((1, d), lambda bi, i, j: (0, 0)),
            pl.BlockSpec((None, d, tf), lambda bi, i, j: (0, 0, j)),
            pl.BlockSpec((None, tf, d), lambda bi, i, j: (0, j, 0)),
            pl.BlockSpec((1, d), lambda bi, i, j: (0, 0)),
        ],
        out_specs=pl.BlockSpec((1, tm, d), lambda bi, i, j: (bi, i, 0)),
        out_shape=jax.ShapeDtypeStruct((b, seq, d), F32),
        scratch_shapes=[pltpu.VMEM((tm, d), BF16)],
        compiler_params=_cparams(("parallel", "parallel", "arbitrary"), **vmem),
        name="mlp_final",
    )(h3.reshape(b * tp, d), g, w1, w2, gf)


def _shifted_norm(h_ref, hp_ref, g_ref):
    g = g_ref[...]
    hn = _rms(h_ref[0], g, NORM_EPS)
    pn = _rms(hp_ref[0], g, NORM_EPS)[7:8]
    pn = jnp.where(pl.program_id(1) == 0, 0.0, pn)
    row = lax.broadcasted_iota(jnp.int32, hn.shape, 0)
    prev = jnp.where(row == 0, pn, pltpu.roll(hn, 1, axis=0))
    return hn, prev - hn


def _to_pairs(o_ref, lead, val):
    for p in range(val.shape[1] // LANES):
        o_ref[lead + (p,)] = val[:, p * LANES:(p + 1) * LANES].astype(o_ref.dtype)


def _shift_specs(tm, d):
    nb = tm // 8
    return [
        pl.BlockSpec((1, tm, d), lambda bi, i: (bi, i, 0)),
        pl.BlockSpec((1, 8, d), lambda bi, i: (bi, jnp.maximum(i * nb - 1, 0), 0)),
        pl.BlockSpec((1, d), lambda bi, i: (0, 0)),
        pl.BlockSpec((6, d), lambda bi, i: (0, 0)),
    ]


RKV_MIX_ROWS = (0, 2, 3)


def _rkv_kernel(*refs, n_side):
    h_ref, hp_ref, g_ref, mix_ref, w_ref = refs[:5]
    side_in = refs[5:5 + n_side]
    o_ref = refs[5 + n_side]
    side_out = refs[6 + n_side:]
    hn, xx = _shifted_norm(h_ref, hp_ref, g_ref)
    for s, mrow in enumerate(RKV_MIX_ROWS):
        x = (hn + xx * mix_ref[mrow:mrow + 1, :]).astype(BF16)
        _to_pairs(o_ref, (s,), jnp.dot(x, w_ref[s], preferred_element_type=F32))
    for src, dst in zip(side_in, side_out):
        dst[...] = src[...].astype(BF16)


def _rkv(h3, g, mix, w, sides=()):
    b, tp, d = h3.shape
    tm = _pick_tile(tp, RKV_ROWS)
    nt = tp // tm
    npair = d // LANES
    side_in, side_out, side_shapes, side_bytes = _side_cast_specs(sides, b * nt, lambda bi, i: bi * nt + i)
    vmem = dict(moving=[_nbytes((tm + 8, d), F32), 3 * _nbytes((tm, d), F32)] + side_bytes,
                fixed=[_nbytes((3, d, d), BF16)], live=4 * _nbytes((tm, d), F32))
    outs = pl.pallas_call(
        functools.partial(_rkv_kernel, n_side=len(sides)),
        grid=(b, nt),
        in_specs=_shift_specs(tm, d) + [
            pl.BlockSpec((None, 3, d, d), lambda bi, i: (0, 0, 0, 0), pipeline_mode=pl.Buffered(1)),
        ] + side_in,
        out_specs=[pl.BlockSpec((3, npair, tm, LANES), lambda bi, i: (0, 0, bi * nt + i, 0))] + side_out,
        out_shape=[jax.ShapeDtypeStruct((3, npair, b * tp, LANES), F32)] + side_shapes,
        compiler_params=_cparams(("arbitrary", "arbitrary"), **vmem),
        name="rw_rkv",
    )(h3, h3, g, mix, w, *[arr for arr, _ in sides])
    return outs[0], outs[1:]


def _lora_kernel(*refs, branches):
    h_ref, hp_ref, g_ref, mix_ref = refs[:4]
    nb = len(branches)
    w_refs = refs[4:4 + 3 * nb]
    o_refs = refs[4 + 3 * nb:]
    hn, xx = _shifted_norm(h_ref, hp_ref, g_ref)
    for bi, (mrow, mid, out) in enumerate(branches):
        w1_ref, w2_ref, b_ref = w_refs[3 * bi:3 * bi + 3]
        x = (hn + xx * mix_ref[mrow:mrow + 1, :]).astype(BF16)
        z = jnp.dot(x, w1_ref[...], preferred_element_type=F32)
        if mid == "tanh":
            z = jnp.tanh(z)
        elif mid == "sigmoid":
            z = _sigmoid(z)
        y = jnp.dot(z.astype(BF16), w2_ref[...], preferred_element_type=F32)
        if out == "decay":
            y = -RW_DECAY_SCALE * _sigmoid(y + b_ref[...])
        elif out == "sigmoid":
            y = _sigmoid(y + b_ref[...])
        _to_pairs(o_refs[bi], (), y)


def _lora(h3, g, mix, branches, weights):
    b, tp, d = h3.shape
    tm = _pick_tile(tp, LORA_ROWS)
    nt = tp // tm
    npair = d // LANES
    w_specs, w_args = [], []
    for w1, w2, bias in weights:
        r = w1.shape[1]
        w_specs += [pl.BlockSpec((d, r), lambda bi, i: (0, 0)),
                    pl.BlockSpec((r, d), lambda bi, i: (0, 0)),
                    pl.BlockSpec((1, d), lambda bi, i: (0, 0))]
        w_args += [w1, w2, bias]
    out_spec = pl.BlockSpec((npair, tm, LANES), lambda bi, i: (0, bi * nt + i, 0))
    out_shape = jax.ShapeDtypeStruct((npair, b * tp, LANES), F32)
    vmem = dict(moving=[_nbytes((tm + 8, d), F32)] + [_nbytes((tm, d), F32)] * len(branches),
                fixed=[_nbytes(w.shape, w.dtype) for w in w_args], live=4 * _nbytes((tm, d), F32))
    return pl.pallas_call(
        functools.partial(_lora_kernel, branches=branches),
        grid=(b, nt),
        in_specs=_shift_specs(tm, d) + w_specs,
        out_specs=[out_spec] * len(branches),
        out_shape=[out_shape] * len(branches),
        compiler_params=_cparams(("parallel", "parallel"), **vmem),
        name="rw_lora",
    )(h3, h3, g, mix, *w_args)


def _pad_rank(w1, w2):
    r = w1.shape[1]
    rp = -(-r // LANES) * LANES
    return (jnp.pad(w1, ((0, 0), (0, rp - r))).astype(BF16),
            jnp.pad(w2, ((0, rp - r), (0, 0))).astype(BF16))


def _stack_heads(x, lane_lo):
    return jnp.concatenate([jnp.where(lane_lo, x, 0.0), jnp.where(lane_lo, 0.0, x)], axis=0)


def _rw_scan_kernel(*refs, has_vres):
    if has_vres:
        (rkv_ref, lw_ref, ag_ref, gg_ref, vf_ref, vg_ref,
         kk_ref, ka_ref, rk_ref, lnw_ref, lnb_ref, z_ref, st_ref) = refs
    else:
        (rkv_ref, lw_ref, ag_ref, gg_ref,
         kk_ref, ka_ref, rk_ref, lnw_ref, lnb_ref, z_ref, st_ref) = refs
    c = pl.program_id(1)
    npair = z_ref.shape[0]
    group = min(PAIR_GROUP, npair)
    assert npair % group == 0
    two_l = 2 * CHUNK

    @pl.when(c == 0)
    def _():
        st_ref[...] = jnp.zeros_like(st_ref)

    row = lax.broadcasted_iota(jnp.int32, (two_l, two_l), 0)
    col = lax.broadcasted_iota(jnp.int32, (two_l, two_l), 1)
    t_half = lax.broadcasted_iota(jnp.int32, (CHUNK, two_l), 0)
    s_half = lax.broadcasted_iota(jnp.int32, (CHUNK, two_l), 1) % CHUNK
    strict_t = t_half > s_half
    incl_t = t_half >= s_half
    same_head = (row // CHUNK) == (col // CHUNK)
    tril = jnp.where(lax.broadcasted_iota(jnp.int32, (CHUNK, CHUNK), 0)
                     >= lax.broadcasted_iota(jnp.int32, (CHUNK, CHUNK), 1), 1.0, 0.0).astype(BF16)
    diag = row == col
    same_block = lambda n: (row // n) == (col // n)
    base_mask = same_block(INV_BASE_BLOCK)
    merge_masks = [jnp.logical_and(same_block(2 * n), jnp.logical_not(same_block(n)))
                   for n in (INV_BASE_BLOCK << i for i in range((CHUNK // INV_BASE_BLOCK).bit_length() - 1))]
    lane_lo = lax.broadcasted_iota(jnp.int32, (CHUNK, LANES), 1) < RW_HEAD
    live = jnp.logical_or(c > 0, lax.broadcasted_iota(jnp.int32, (CHUNK, LANES), 0) >= N_DUMMY)

    def each(fn, *lists):
        return [fn(*xs) for xs in zip(*lists)]

    ones_bd = jnp.where(same_head, 1.0, 0.0).astype(BF16)

    def head_sum_lanes(x):
        lo = jnp.sum(jnp.where(lane_lo, x, 0.0), axis=-1, keepdims=True)
        hi = jnp.sum(jnp.where(lane_lo, 0.0, x), axis=-1, keepdims=True)
        return jnp.where(lane_lo, lo, hi)

    def head_sum_matmul(x):
        hi, lo = _split2(x)
        both = jnp.dot(jnp.concatenate([hi, lo], axis=0), ones_bd, preferred_element_type=F32)
        return both[:x.shape[0]] + both[x.shape[0]:]

    def swap_halves(x):
        return pltpu.roll(x, RW_HEAD, axis=1)

    def group_body(gi, carry):
        ps = [gi * group + i for i in range(group)]
        r = [rkv_ref[0, p] for p in ps]
        k0 = [rkv_ref[1, p] for p in ps]
        v = [rkv_ref[2, p] for p in ps]
        lw = [lw_ref[p] for p in ps]
        ag = [ag_ref[p] for p in ps]
        if has_vres:
            v = [vi + (vf_ref[0, p] - vi) * vg_ref[p] for vi, p in zip(v, ps)]
        kk = [ki * kk_ref[p] for ki, p in zip(k0, ps)]
        k = [ki * (1.0 + (ai - 1.0) * ka_ref[p]) for ki, ai, p in zip(k0, ag, ps)]
        kk = each(lambda kki: kki / jnp.maximum(jnp.sqrt(head_sum_lanes(kki * kki)), 1e-12), kk)
        bonus = [head_sum_lanes(ri * ki * rk_ref[p]) * vi for ri, ki, vi, p in zip(r, k, v, ps)]

        cs = each(lambda lwi: _dot_sel_left(tril, lwi), lw)
        e_pos = each(jnp.exp, cs)
        e_neg = each(lambda ci: jnp.exp(-ci), cs)
        w_all = each(lambda ei: ei[CHUNK - 1:CHUNK, :], e_pos)
        a_t = each(lambda kki, ci, lwi: -kki * jnp.exp(ci - lwi), kk, cs, lw)
        r_t = each(lambda ri, ei: ri * ei, r, e_pos)
        b_t = each(lambda kki, ai, ei: kki * ai * ei, kk, ag, e_neg)
        k_t = each(lambda ki, ei: ki * ei, k, e_neg)
        v_s = each(lambda vi: _stack_heads(vi, lane_lo).astype(BF16), v)
        b_s = each(lambda bti: _stack_heads(bti, lane_lo), b_t)
        k_s = each(lambda kti: _stack_heads(kti, lane_lo), k_t)
        bh_t = each(lambda bsi, wi: (bsi * wi).T.astype(BF16), b_s, w_all)
        kh_t = each(lambda ksi, wi: (ksi * wi).T.astype(BF16), k_s, w_all)

        gram = each(lambda ati, rti, bsi, ksi: _dot_nt(
            jnp.concatenate([ati, rti], axis=0), jnp.concatenate([bsi, ksi], axis=0)), a_t, r_t, b_s, k_s)

        block = lambda gm, keep, rows, cols: jnp.where(
            keep, gm[rows * CHUNK:(rows + 1) * CHUNK, cols * two_l:(cols + 1) * two_l], 0.0)
        p_c = each(lambda gm: block(gm, strict_t, 0, 0), gram)
        a_ak = each(lambda gm: block(gm, strict_t, 0, 1).astype(BF16), gram)
        a_rb = each(lambda gm: _stack_heads(block(gm, incl_t, 1, 0), lane_lo).astype(BF16), gram)
        a_rk = each(lambda gm: block(gm, incl_t, 1, 1).astype(BF16), gram)

        xv = each(lambda aki, ari, khi, vsi: jnp.dot(jnp.concatenate([aki, ari, khi], axis=0), vsi,
                                                     preferred_element_type=F32), a_ak, a_rk, kh_t, v_s)

        def z_init(ati, xi):
            su = swap_halves(xi[:CHUNK])
            return jnp.concatenate([jnp.where(lane_lo, ati, su), jnp.where(lane_lo, su, ati)], axis=0)

        z0 = each(z_init, a_t, xv)
        p_bd = each(lambda pci: _stack_heads(pci, lane_lo), p_c)
        d_bf = each(lambda pi: jnp.where(base_mask, pi, 0.0).astype(BF16), p_bd)
        d_sq = each(lambda di: jnp.dot(di, di, preferred_element_type=F32), d_bf)
        t_inv = each(lambda di, qi: jnp.dot(jnp.where(diag, 1.0, di.astype(F32)).astype(BF16),
                                            jnp.where(diag, 1.0, qi).astype(BF16),
                                            preferred_element_type=F32), d_bf, d_sq)
        for merge_mask in merge_masks:
            below = each(lambda pi: jnp.where(merge_mask, pi, 0.0).astype(BF16), p_bd)
            t_bf = each(lambda ti: ti.astype(BF16), t_inv)
            lt = each(lambda li, ti: jnp.dot(li, ti, preferred_element_type=F32).astype(BF16), below, t_bf)
            t_inv = each(lambda ti, tbi, lti: ti + jnp.dot(tbi, lti, preferred_element_type=F32), t_inv, t_bf, lt)

        def apply_inverse(ti, zi):
            t_hi, t_lo = _split2(ti)
            z_hi, z_lo = _split2(zi)
            main = jnp.dot(t_hi, jnp.concatenate([z_hi, z_lo], axis=1), preferred_element_type=F32)
            return (main[:, :LANES] + main[:, LANES:]
                    + jnp.dot(t_lo, z_hi, preferred_element_type=F32)).astype(BF16)

        zb = each(apply_inverse, t_inv, z0)

        rb = each(lambda ai, bhi, zi: jnp.dot(jnp.concatenate([ai, bhi], axis=0), zi,
                                              preferred_element_type=F32), a_rb, bh_t, zb)
        r_hat = each(lambda rti, rbi: rti + jnp.where(lane_lo, rbi[:CHUNK], rbi[CHUNK:two_l]), r_t, rb)
        y_hat = each(lambda rbi, xi: swap_halves(jnp.where(lane_lo, rbi[CHUNK:two_l], rbi[:CHUNK]))
                     + xi[CHUNK:two_l], rb, xv)
        m_mat = each(lambda wi, rbi: jnp.where(diag, wi, 0.0) + jnp.where(same_head, rbi[two_l:], 0.0),
                     w_all, rb)
        n_mat = each(lambda rbi, xi: swap_halves(jnp.where(same_head, 0.0, rbi[two_l:])) + xi[two_l:],
                     rb, xv)

        hb = [st_ref[p].astype(BF16) for p in ps]
        ys = each(lambda rh, mm, hi: jnp.dot(jnp.concatenate([rh, mm], axis=0).astype(BF16), hi,
                                             preferred_element_type=F32), r_hat, m_mat, hb)
        for p, ysi, nm in zip(ps, ys, n_mat):
            st_ref[p] = ysi[CHUNK:] + nm
        y = each(lambda ysi, yh: ysi[:CHUNK] + yh, ys, y_hat)

        dy = each(lambda yi: yi - head_sum_matmul(yi) * (1.0 / RW_HEAD), y)
        var = each(lambda di: head_sum_matmul(di * di) * (1.0 / RW_HEAD), dy)
        for p, di, vi, bi in zip(ps, dy, var, bonus):
            yn = di * lax.rsqrt(vi + RW_GN_EPS) * lnw_ref[p] + lnb_ref[p]
            z_ref[p] = jnp.where(live, (yn + bi) * gg_ref[p], 0.0).astype(z_ref.dtype)
        return carry

    lax.fori_loop(0, npair // group, group_body, 0)


def _rw_scan(rkv, lw, ag, gg, vres, kk, ka, rk, lnw, lnb, batch):
    _, npair, m, _ = rkv.shape
    nc = m // batch // CHUNK
    row_map3 = lambda bi, c: (0, bi * nc + c, 0)
    row_map4 = lambda bi, c: (0, 0, bi * nc + c, 0)
    par = pl.BlockSpec((npair, 1, LANES), lambda bi, c: (0, 0, 0))
    tile = pl.BlockSpec((npair, CHUNK, LANES), row_map3)
    in_specs = [pl.BlockSpec((3, npair, CHUNK, LANES), row_map4), tile, tile, tile]
    args = [rkv, lw, ag, gg]
    if vres is not None:
        vfirst, vgate = vres
        in_specs += [pl.BlockSpec((1, npair, CHUNK, LANES), lambda bi, c: (2, 0, bi * nc + c, 0)), tile]
        args += [vfirst, vgate]
    in_specs += [par] * 5
    args += [kk, ka, rk, lnw, lnb]
    tile_bytes = _nbytes((npair, CHUNK, LANES), F32)
    n_tiles = 3 + 3 + (2 if vres is not None else 0) + 1
    state_bytes = _nbytes((npair, 2 * CHUNK, LANES), F32)
    vmem = dict(moving=[n_tiles * tile_bytes], fixed=[state_bytes], live=8 * state_bytes)
    return pl.pallas_call(
        functools.partial(_rw_scan_kernel, has_vres=vres is not None),
        grid=(batch, nc),
        in_specs=in_specs,
        out_specs=tile,
        out_shape=jax.ShapeDtypeStruct((npair, m, LANES), BF16),
        scratch_shapes=[pltpu.VMEM((npair, 2 * CHUNK, LANES), F32)],
        compiler_params=_cparams(("parallel", "arbitrary"), **vmem),
        name="rw_scan",
    )(*args)


def _proj_kernel(x_ref, w_ref, h_ref, o_ref, *, pair_major):
    if pair_major:
        x = jnp.concatenate([x_ref[p] for p in range(x_ref.shape[0])], axis=1)
    else:
        x = x_ref[...]
    o_ref[...] = h_ref[...] + jnp.dot(x, w_ref[...], preferred_element_type=F32)


def _proj_residual(x, w, h, pair_major):
    m, d = h.shape
    kdim = w.shape[1]
    tm = _pick_tile(m, PROJ_ROWS)
    vmem = dict(moving=[_nbytes((tm, kdim), BF16), 2 * _nbytes((tm, d), F32)], fixed=[_nbytes((kdim, d), BF16)],
                live=_nbytes((tm, d), F32))
    if pair_major:
        x_spec = pl.BlockSpec((x.shape[0], tm, LANES), lambda i: (0, i, 0))
    else:
        x_spec = pl.BlockSpec((tm, kdim), lambda i: (i, 0))
    return pl.pallas_call(
        functools.partial(_proj_kernel, pair_major=pair_major),
        grid=(m // tm,),
        in_specs=[x_spec,
                  pl.BlockSpec((None, kdim, d), lambda i: (0, 0, 0)),
                  pl.BlockSpec((tm, d), lambda i: (i, 0))],
        out_specs=pl.BlockSpec((tm, d), lambda i: (i, 0)),
        out_shape=jax.ShapeDtypeStruct((m, d), F32),
        compiler_params=_cparams(("parallel",), **vmem),
        name="proj_residual",
    )(x, w, h)


def _gla_in_kernel(h_ref, g_ref, w_ref, wz_ref, wa_ref, ba_ref, p_ref, gl_ref, xn_ref):
    @pl.when(pl.program_id(1) == 0)
    def _():
        xn = _rms(h_ref[...], g_ref[...], NORM_EPS).astype(BF16)
        xn_ref[...] = xn
        za = jnp.dot(xn, wz_ref[...], preferred_element_type=F32)
        u = _dot(za, wa_ref[...]) + ba_ref[...]
        gl_ref[...] = (jnp.minimum(u, 0.0) - jnp.log1p(jnp.exp(-jnp.abs(u)))) * (1.0 / GLA_TAU)

    p_ref[...] = _dot_nt(xn_ref[...], w_ref[...]).astype(p_ref.dtype)


def _gla_in(h, g, w, n, wz, wa, ba):
    m, d = h.shape
    dk = wa.shape[1]
    tm = _pick_tile(m, GLA_IN_TILE[0])
    tn = _pick_tile(n, GLA_IN_TILE[1])
    vmem = dict(moving=[_nbytes((tm, d), F32), _nbytes((tn, d), BF16), _nbytes((tm, tn), BF16),
                        _nbytes((tm, dk), F32)],
                fixed=[_nbytes((tm, d), BF16), _nbytes((d, LANES), BF16), _nbytes((LANES, dk), F32)],
                live=_nbytes((tm, tn), F32) + 2 * _nbytes((tm, dk), F32))
    return pl.pallas_call(
        _gla_in_kernel,
        grid=(m // tm, n // tn),
        in_specs=[
            pl.BlockSpec((tm, d), lambda i, j: (i, 0)),
            pl.BlockSpec((1, d), lambda i, j: (0, 0)),
            pl.BlockSpec((None, tn, d), lambda i, j: (0, j, 0)),
            pl.BlockSpec((d, LANES), lambda i, j: (0, 0)),
            pl.BlockSpec((LANES, dk), lambda i, j: (0, 0)),
            pl.BlockSpec((1, dk), lambda i, j: (0, 0)),
        ],
        out_specs=[pl.BlockSpec((tm, tn), lambda i, j: (i, j)),
                   pl.BlockSpec((tm, dk), lambda i, j: (i, 0))],
        out_shape=[jax.ShapeDtypeStruct((m, n), BF16), jax.ShapeDtypeStruct((m, dk), F32)],
        scratch_shapes=[pltpu.VMEM((tm, d), BF16)],
        compiler_params=_cparams(("parallel", "arbitrary"), **vmem),
        name="gla_in",
    )(h, g, w, wz, wa, ba)


def _gla_chunk_kernel(q_ref, k_ref, v_ref, gate_ref, gl_ref, gn_ref, z_ref, st_ref, *, scale):
    c = pl.program_id(1)
    nb, nh, hv, hk = st_ref.shape

    @pl.when(c == 0)
    def _():
        st_ref[...] = jnp.zeros_like(st_ref)

    row = lax.broadcasted_iota(jnp.int32, (CHUNK, CHUNK), 0)
    col = lax.broadcasted_iota(jnp.int32, (CHUNK, CHUNK), 1)
    causal = row >= col
    tril = jnp.where(causal, 1.0, 0.0).astype(BF16)
    live = jnp.logical_or(c > 0, lax.broadcasted_iota(jnp.int32, gl_ref.shape[1:], 0) >= N_DUMMY)
    ksl = lambda x, h: x[:, h * hk:(h + 1) * hk]
    vsl = lambda x, h: x[:, h * hv:(h + 1) * hv]
    seqs = range(nb)
    cells = [(b, h) for b in seqs for h in range(nh)]

    gl = [jnp.where(live, gl_ref[b], 0.0) for b in seqs]
    bc = [_dot_sel_left(tril, gl[b]) for b in seqs]
    b_last = [bc[b][CHUNK - 1:CHUNK, :] for b in seqs]
    e_last = [jnp.exp(b_last[b]) for b in seqs]
    q_t = [(q_ref[b] * scale * jnp.exp(bc[b])).astype(BF16) for b in seqs]
    k_t = [(k_ref[b] * jnp.exp(-bc[b])).astype(BF16) for b in seqs]
    k_h = [(k_ref[b] * jnp.exp(b_last[b] - bc[b])).astype(BF16) for b in seqs]
    v = {(b, h): vsl(v_ref[b], h).astype(BF16) for b, h in cells}
    att = {(b, h): jnp.where(causal, _dot_nt(ksl(q_t[b], h), ksl(k_t[b], h)), 0.0).astype(BF16) for b, h in cells}
    st = {(b, h): st_ref[b, h] for b, h in cells}
    o = {(b, h): _dot_nt(ksl(q_t[b], h), st[b, h]) + jnp.dot(att[b, h], v[b, h], preferred_element_type=F32)
         for b, h in cells}
    for b, h in cells:
        st_ref[b, h] = st[b, h] * ksl(e_last[b], h) + _dot_tn(v[b, h], ksl(k_h[b], h))
    for b, h in cells:
        on = o[b, h] * lax.rsqrt(jnp.mean(o[b, h] * o[b, h], axis=-1, keepdims=True) + GLA_HEAD_EPS)
        gate = vsl(gate_ref[b], h).astype(F32)
        z_ref[b, :, h * hv:(h + 1) * hv] = (on * vsl(gn_ref[...], h) * (gate * _sigmoid(gate))).astype(z_ref.dtype)


def _gla_chunk(p, glog, gn_w, batch, d):
    tp = p.shape[0] // batch
    nc = tp // CHUNK
    dk = d // 2
    hk = dk // GLA_HEADS
    hv = d // GLA_HEADS
    p3 = p.reshape(batch, tp, p.shape[1])
    nb = GLA_SEQ_GROUP if batch % GLA_SEQ_GROUP == 0 else 1
    state_bytes = _nbytes((nb, GLA_HEADS, hv, hk), F32)
    vmem = dict(moving=[2 * _nbytes((nb, CHUNK, dk), BF16), 3 * _nbytes((nb, CHUNK, d), BF16),
                        _nbytes((nb, CHUNK, dk), F32)], fixed=[state_bytes], live=state_bytes)
    z = pl.pallas_call(
        functools.partial(_gla_chunk_kernel, scale=hk ** -0.5),
        grid=(batch // nb, nc),
        in_specs=[
            pl.BlockSpec((nb, CHUNK, dk), lambda g, c: (g, c, 0)),
            pl.BlockSpec((nb, CHUNK, dk), lambda g, c: (g, c, 1)),
            pl.BlockSpec((nb, CHUNK, d), lambda g, c: (g, c, 1)),
            pl.BlockSpec((nb, CHUNK, d), lambda g, c: (g, c, 2)),
            pl.BlockSpec((nb, CHUNK, dk), lambda g, c: (g, c, 0)),
            pl.BlockSpec((1, d), lambda g, c: (0, 0)),
        ],
        out_specs=pl.BlockSpec((nb, CHUNK, d), lambda g, c: (g, c, 0)),
        out_shape=jax.ShapeDtypeStruct((batch, tp, d), BF16),
        scratch_shapes=[pltpu.VMEM((nb, GLA_HEADS, hv, hk), F32)],
        compiler_params=_cparams(("parallel", "arbitrary"), **vmem),
        name="gla_chunk",
    )(p3, p3, p3, p3, glog.reshape(batch, tp, dk), gn_w)
    return z.reshape(batch * tp, d)


def _rwkv_layer(h, batch, j, v_first, norm_g, rw_mix, rw_w_rkv, rw_w0, rw_w1, rw_w2, rw_a0, rw_a1, rw_a2,
                rw_v0, rw_v1, rw_v2, rw_g1, rw_g2, rw_k_k, rw_k_a, rw_r_k, rw_ln_w, rw_ln_b, rw_w_o,
                sides=()):
    m, d = h.shape
    npair = d // LANES
    row = lambda t: t.reshape(1, d)
    pairs = lambda t: t.reshape(npair, 1, LANES)
    h3 = h.reshape(batch, m // batch, d)
    g = row(norm_g)
    rkv, cast = _rkv(h3, g, rw_mix[j], rw_w_rkv, sides)
    if rw_w_o is None:
        rw_w_o = cast[0]
    branches = [(1, "tanh", "decay"), (4, "none", "sigmoid"), (5, "sigmoid", "none")]
    weights = [_pad_rank(rw_w1[j], rw_w2[j]) + (row(rw_w0[j]),),
               _pad_rank(rw_a1[j], rw_a2[j]) + (row(rw_a0[j]),),
               _pad_rank(rw_g1[j], rw_g2[j]) + (jnp.zeros((1, d), F32),)]
    if j > 0:
        branches.append((3, "none", "sigmoid"))
        weights.append(_pad_rank(rw_v1[j - 1], rw_v2[j - 1]) + (row(rw_v0[j - 1]),))
    outs = _lora(h3, g, rw_mix[j], tuple(branches), weights)
    lw, ag, gg = outs[:3]
    vres = (v_first, outs[3]) if j > 0 else None
    z = _rw_scan(rkv, lw, ag, gg, vres, pairs(rw_k_k[j]), pairs(rw_k_a[j]), pairs(rw_r_k[j]),
                 pairs(rw_ln_w[j]), pairs(rw_ln_b[j]), batch)
    h = _proj_residual(z, rw_w_o, h, pair_major=True)
    return h, rkv, cast


def _gla_layer(h, batch, j, norm_g, gla_w_in_bf, gla_w_a2, gla_b_a, gla_gn_w, gla_w_o):
    m, d = h.shape
    dk = d // 2
    n_main = 2 * dk + 2 * d
    rank = gla_w_in_bf.shape[1] - n_main
    g = norm_g.reshape(1, d)
    wz = jnp.pad(gla_w_in_bf[0, n_main:, :].T, ((0, 0), (0, LANES - rank)))
    wa = jnp.pad(gla_w_a2[j], ((0, LANES - rank), (0, 0)))
    p, glog = _gla_in(h, g, gla_w_in_bf, n_main, wz, wa, gla_b_a[j].reshape(1, dk))
    z = _gla_chunk(p, glog, gla_gn_w[j].reshape(1, d), batch, d)
    return _proj_residual(z, gla_w_o, h, pair_major=False)


def kernel(x, meta, norm_mix, norm_mlp, norm_f, mlp_w1, mlp_w2, rw_mix, rw_w_rkv, rw_w0, rw_w1, rw_w2, rw_a0, rw_a1, rw_a2, rw_v0, rw_v1, rw_v2, rw_g1, rw_g2, rw_k_k, rw_k_a, rw_r_k, rw_ln_w, rw_ln_b, rw_w_o, gla_w_in, gla_w_a2, gla_b_a, gla_gn_w, gla_w_o):
    batch, seq, d = x.shape
    depth = norm_mix.shape[0]
    assert seq % CHUNK == 0 and d % (2 * LANES) == 0 and meta.shape[0] == N_META
    tp = LEAD + seq
    lead = jnp.concatenate([jnp.zeros((N_DUMMY, d), x.dtype), meta.astype(x.dtype)], axis=0)
    h = jnp.concatenate([jnp.broadcast_to(lead[None], (batch, LEAD, d)), x], axis=1).reshape(batch * tp, d)
    gf = norm_f.reshape(1, d)
    rw_w_rkv2 = rw_w_rkv.reshape(rw_w_rkv.shape[0], 3 * d, d)
    gla_w_in_t = jnp.transpose(gla_w_in, (0, 2, 1))

    def mixer_weights(i):
        j = i // 2
        mixer = [(rw_w_rkv2, j), (rw_w_o, j)] if i % 2 == 0 else [(gla_w_in_t, j), (gla_w_o, j)]
        return mixer + [(mlp_w1, i), (mlp_w2, i)]

    w_first, first_layer = mixer_weights(0)[0]
    cast = [w_first[first_layer:first_layer + 1].astype(BF16), None, None, None]
    v_first = None
    for i in range(depth):
        j = i // 2
        w_a, w_b, w1_bf, w2_bf = cast
        if i % 2 == 0:
            h, rkv, early = _rwkv_layer(h, batch, j, v_first, norm_mix[i], rw_mix, w_a.reshape(1, 3, d, d), rw_w0,
                                        rw_w1, rw_w2, rw_a0, rw_a1, rw_a2, rw_v0, rw_v1, rw_v2, rw_g1, rw_g2,
                                        rw_k_k, rw_k_a, rw_r_k.reshape(rw_r_k.shape[0], d), rw_ln_w, rw_ln_b, w_b,
                                        sides=mixer_weights(0)[1:] if i == 0 else ())
            if i == 0:
                _, w1_bf, w2_bf = early
            if j == 0:
                v_first = rkv
        else:
            h = _gla_layer(h, batch, j, norm_mix[i], w_a, gla_w_a2, gla_b_a, gla_gn_w, w_b)
        g_mlp = norm_mlp[i].reshape(1, d)
        if i == depth - 1:
            return _mlp_final(h.reshape(batch, tp, d), g_mlp, w1_bf, w2_bf, gf, *MLP_FINAL_TILE)
        h, cast = _mlp(h, g_mlp, w1_bf, w2_bf, gf, *MLP_TILE, sides=mixer_weights(i + 1))
```

```python
import functools
import math

import jax
import jax.numpy as jnp
from jax import lax
from jax.experimental import pallas as pl
from jax.experimental.pallas import tpu as pltpu

F32 = jnp.float32
BF16 = jnp.bfloat16

N_META = 16
CHUNK = 64
LEAD = CHUNK
N_DUMMY = LEAD - N_META
NORM_EPS = 1e-6

RW_HEAD = 64
LANES = 128
RW_GN_EPS = 64e-5
RW_DECAY_SCALE = 0.6065306597126334
PAIR_GROUP = 16
INV_BASE_BLOCK = 4

GLA_HEADS = 4
GLA_TAU = 16.0
GLA_HEAD_EPS = 1e-5
GLA_SEQ_GROUP = 4

V7X_VMEM_BYTES = 64 * 1024 * 1024
VMEM_REQUEST_CAP = V7X_VMEM_BYTES - 8 * 1024 * 1024

MLP_TILE = (832, 1024)
MLP_FINAL_TILE = (512, 1024)
RKV_ROWS = 208
LORA_ROWS = 416
PROJ_ROWS = 640
GLA_IN_TILE = (640, 2048)


def _nbytes(shape, dtype):
    return math.prod(shape) * jnp.dtype(dtype).itemsize


def _cparams(sem, moving, fixed=(), live=0):
    need = 2 * sum(moving) + sum(fixed) + live
    return pltpu.CompilerParams(dimension_semantics=sem, vmem_limit_bytes=min(need, VMEM_REQUEST_CAP))


def _dot(a, b):
    return jnp.dot(a.astype(BF16), b.astype(BF16), preferred_element_type=F32)


def _dot_nt(a, b):
    return lax.dot_general(a.astype(BF16), b.astype(BF16), (((1,), (1,)), ((), ())),
                           preferred_element_type=F32)


def _dot_tn(a, b):
    return lax.dot_general(a.astype(BF16), b.astype(BF16), (((0,), (0,)), ((), ())),
                           preferred_element_type=F32)


def _split2(x):
    hi = x.astype(BF16)
    return hi, (x - hi.astype(F32)).astype(BF16)


def _dot_sel_left(sel, x):
    hi, lo = _split2(x)
    return jnp.dot(sel, hi, preferred_element_type=F32) + jnp.dot(sel, lo, preferred_element_type=F32)


def _rms(x, g, eps):
    return x * lax.rsqrt(jnp.mean(x * x, axis=-1, keepdims=True) + eps) * g


def _sigmoid(x):
    return 1.0 / (1.0 + jnp.exp(-x))


def _pick_tile(n, target):
    best = None
    for t in range(16, min(n, target) + 1, 16):
        if n % t == 0:
            best = t
    assert best is not None, (n, target)
    return best


def _side_cast_specs(sides, n_steps, step_of):
    in_specs, out_specs, out_shapes, moving = [], [], [], []
    for arr, layer in sides:
        _, rows, cols = arr.shape
        blk = next(b for b in range(16, rows + 1, 16) if rows % b == 0 and rows // b <= n_steps)
        last = rows // blk - 1
        in_specs.append(pl.BlockSpec(
            (None, blk, cols), lambda *ids, layer=layer, last=last: (layer, jnp.minimum(step_of(*ids), last), 0)))
        out_specs.append(pl.BlockSpec(
            (None, blk, cols), lambda *ids, last=last: (0, jnp.minimum(step_of(*ids), last), 0)))
        out_shapes.append(jax.ShapeDtypeStruct((1, rows, cols), BF16))
        moving += [_nbytes((blk, cols), F32), _nbytes((blk, cols), BF16)]
    return in_specs, out_specs, out_shapes, moving


def _mlp_kernel(*refs, final, n_side):
    h_ref, g_ref, w1_ref, w2_ref, gf_ref = refs[:5]
    side_in = refs[5:5 + n_side]
    o_ref = refs[5 + n_side]
    side_out = refs[6 + n_side:6 + 2 * n_side]
    xn_ref = refs[6 + 2 * n_side]
    ff_axis = 2 if final else 1
    if final:
        o_ref = o_ref.at[0]
    j = pl.program_id(ff_axis)

    @pl.when(j == 0)
    def _():
        x = h_ref[...]
        xn_ref[...] = _rms(x, g_ref[...], NORM_EPS).astype(BF16)
        o_ref[...] = x

    hid = jnp.dot(xn_ref[...], w1_ref[...], preferred_element_type=F32)
    hid = jnp.maximum(hid, 0.0)
    hid = hid * hid
    o_ref[...] += jnp.dot(hid.astype(BF16), w2_ref[...], preferred_element_type=F32)
    for src, dst in zip(side_in, side_out):
        dst[...] = src[...].astype(BF16)

    if final:
        @pl.when(j == pl.num_programs(ff_axis) - 1)
        def _():
            o_ref[...] = _rms(o_ref[...], gf_ref[...], NORM_EPS)


def _mlp(h, g, w1, w2, gf, tm_target, tf_target, sides=()):
    m, d = h.shape
    ff = w1.shape[2]
    tm = _pick_tile(m, tm_target)
    tf = _pick_tile(ff, tf_target)
    nj = ff // tf
    side_in, side_out, side_shapes, side_bytes = _side_cast_specs(
        sides, (m // tm) * nj, lambda i, j: i * nj + j)
    vmem = dict(moving=[2 * _nbytes((tm, d), F32), 2 * _nbytes((d, tf), BF16)] + side_bytes,
                fixed=[_nbytes((tm, d), BF16)], live=_nbytes((tm, tf), F32) + _nbytes((tm, tf), BF16))
    outs = pl.pallas_call(
        functools.partial(_mlp_kernel, final=False, n_side=len(sides)),
        grid=(m // tm, nj),
        in_specs=[
            pl.BlockSpec((tm, d), lambda i, j: (i, 0)),
            pl.BlockSpec((1, d), lambda i, j: (0, 0)),
            pl.BlockSpec((None, d, tf), lambda i, j: (0, 0, j)),
            pl.BlockSpec((None, tf, d), lambda i, j: (0, j, 0)),
            pl.BlockSpec((1, d), lambda i, j: (0, 0)),
        ] + side_in,
        out_specs=[pl.BlockSpec((tm, d), lambda i, j: (i, 0))] + side_out,
        out_shape=[jax.ShapeDtypeStruct((m, d), F32)] + side_shapes,
        scratch_shapes=[pltpu.VMEM((tm, d), BF16)],
        compiler_params=_cparams(("arbitrary", "arbitrary"), **vmem),
        name="mlp",
    )(h, g, w1, w2, gf, *[arr for arr, _ in sides])
    return outs[0], outs[1:]


def _mlp_final(h3, g, w1, w2, gf, tm_target, tf_target):
    b, tp, d = h3.shape
    seq = tp - LEAD
    ff = w1.shape[2]
    tm = _pick_tile(seq, tm_target)
    tf = _pick_tile(ff, tf_target)
    vmem = dict(moving=[2 * _nbytes((tm, d), F32), 2 * _nbytes((d, tf), BF16)],
                fixed=[_nbytes((tm, d), BF16)], live=_nbytes((tm, tf), F32) + _nbytes((tm, tf), BF16))
    return pl.pallas_call(
        functools.partial(_mlp_kernel, final=True, n_side=0),
        grid=(b, seq // tm, ff // tf),
        in_specs=[
            pl.BlockSpec((pl.Element(tm), pl.Element(d)),
                         lambda bi, i, j: (pl.multiple_of(bi * tp + LEAD + i * tm, 16), 0)),
            pl.BlockSpec((1, d), lambda bi, i, j: (0, 0)),
            pl.BlockSpec((None, d, tf), lambda bi, i, j: (0, 0, j)),
            pl.BlockSpec((None, tf, d), lambda bi, i, j: (0, j, 0)),
            pl.BlockSpec((1, d), lambda bi, i, j: (0, 0)),
        ],
        out_specs=pl.BlockSpec((1, tm, d), lambda bi, i, j: (bi, i, 0)),
        out_shape=jax.ShapeDtypeStruct((b, seq, d), F32),
        scratch_shapes=[pltpu.VMEM((tm, d), BF16)],
        compiler_params=_cparams(("parallel", "parallel", "arbitrary"), **vmem),
        name="mlp_final",
    )(h3.reshape(b * tp, d), g, w1, w2, gf)


def _shifted_norm(h_ref, hp_ref, g_ref):
    g = g_ref[...]
    hn = _rms(h_ref[0], g, NORM_EPS)
    pn = _rms(hp_ref[0], g, NORM_EPS)[7:8]
    pn = jnp.where(pl.program_id(1) == 0, 0.0, pn)
    row = lax.broadcasted_iota(jnp.int32, hn.shape, 0)
    prev = jnp.where(row == 0, pn, pltpu.roll(hn, 1, axis=0))
    return hn, prev - hn


def _to_pairs(o_ref, lead, val):
    for p in range(val.shape[1] // LANES):
        o_ref[lead + (p,)] = val[:, p * LANES:(p + 1) * LANES].astype(o_ref.dtype)


def _shift_specs(tm, d):
    nb = tm // 8
    return [
        pl.BlockSpec((1, tm, d), lambda bi, i: (bi, i, 0)),
        pl.BlockSpec((1, 8, d), lambda bi, i: (bi, jnp.maximum(i * nb - 1, 0), 0)),
        pl.BlockSpec((1, d), lambda bi, i: (0, 0)),
        pl.BlockSpec((6, d), lambda bi, i: (0, 0)),
    ]


RKV_MIX_ROWS = (0, 2, 3)


def _rkv_kernel(*refs, n_side):
    h_ref, hp_ref, g_ref, mix_ref, w_ref = refs[:5]
    side_in = refs[5:5 + n_side]
    o_ref = refs[5 + n_side]
    side_out = refs[6 + n_side:]
    hn, xx = _shifted_norm(h_ref, hp_ref, g_ref)
    for s, mrow in enumerate(RKV_MIX_ROWS):
        x = (hn + xx * mix_ref[mrow:mrow + 1, :]).astype(BF16)
        _to_pairs(o_ref, (s,), jnp.dot(x, w_ref[s], preferred_element_type=F32))
    for src, dst in zip(side_in, side_out):
        dst[...] = src[...].astype(BF16)


def _rkv(h3, g, mix, w, sides=()):
    b, tp, d = h3.shape
    tm = _pick_tile(tp, RKV_ROWS)
    nt = tp // tm
    npair = d // LANES
    side_in, side_out, side_shapes, side_bytes = _side_cast_specs(sides, b * nt, lambda bi, i: bi * nt + i)
    vmem = dict(moving=[_nbytes((tm + 8, d), F32), 3 * _nbytes((tm, d), F32)] + side_bytes,
                fixed=[_nbytes((3, d, d), BF16)], live=4 * _nbytes((tm, d), F32))
    outs = pl.pallas_call(
        functools.partial(_rkv_kernel, n_side=len(sides)),
        grid=(b, nt),
        in_specs=_shift_specs(tm, d) + [
            pl.BlockSpec((None, 3, d, d), lambda bi, i: (0, 0, 0, 0), pipeline_mode=pl.Buffered(1)),
        ] + side_in,
        out_specs=[pl.BlockSpec((3, npair, tm, LANES), lambda bi, i: (0, 0, bi * nt + i, 0))] + side_out,
        out_shape=[jax.ShapeDtypeStruct((3, npair, b * tp, LANES), F32)] + side_shapes,
        compiler_params=_cparams(("arbitrary", "arbitrary"), **vmem),
        name="rw_rkv",
    )(h3, h3, g, mix, w, *[arr for arr, _ in sides])
    return outs[0], outs[1:]


def _lora_kernel(*refs, branches):
    h_ref, hp_ref, g_ref, mix_ref = refs[:4]
    nb = len(branches)
    w_refs = refs[4:4 + 3 * nb]
    o_refs = refs[4 + 3 * nb:]
    hn, xx = _shifted_norm(h_ref, hp_ref, g_ref)
    for bi, (mrow, mid, out) in enumerate(branches):
        w1_ref, w2_ref, b_ref = w_refs[3 * bi:3 * bi + 3]
        x = (hn + xx * mix_ref[mrow:mrow + 1, :]).astype(BF16)
        z = jnp.dot(x, w1_ref[...], preferred_element_type=F32)
        if mid == "tanh":
            z = jnp.tanh(z)
        elif mid == "sigmoid":
            z = _sigmoid(z)
        y = jnp.dot(z.astype(BF16), w2_ref[...], preferred_element_type=F32)
        if out == "decay":
            y = -RW_DECAY_SCALE * _sigmoid(y + b_ref[...])
        elif out == "sigmoid":
            y = _sigmoid(y + b_ref[...])
        _to_pairs(o_refs[bi], (), y)


def _lora(h3, g, mix, branches, weights):
    b, tp, d = h3.shape
    tm = _pick_tile(tp, LORA_ROWS)
    nt = tp // tm
    npair = d // LANES
    w_specs, w_args = [], []
    for w1, w2, bias in weights:
        r = w1.shape[1]
        w_specs += [pl.BlockSpec((d, r), lambda bi, i: (0, 0)),
                    pl.BlockSpec((r, d), lambda bi, i: (0, 0)),
                    pl.BlockSpec((1, d), lambda bi, i: (0, 0))]
        w_args += [w1, w2, bias]
    out_spec = pl.BlockSpec((npair, tm, LANES), lambda bi, i: (0, bi * nt + i, 0))
    out_shape = jax.ShapeDtypeStruct((npair, b * tp, LANES), F32)
    vmem = dict(moving=[_nbytes((tm + 8, d), F32)] + [_nbytes((tm, d), F32)] * len(branches),
                fixed=[_nbytes(w.shape, w.dtype) for w in w_args], live=4 * _nbytes((tm, d), F32))
    return pl.pallas_call(
        functools.partial(_lora_kernel, branches=branches),
        grid=(b, nt),
        in_specs=_shift_specs(tm, d) + w_specs,
        out_specs=[out_spec] * len(branches),
        out_shape=[out_shape] * len(branches),
        compiler_params=_cparams(("parallel", "parallel"), **vmem),
        name="rw_lora",
    )(h3, h3, g, mix, *w_args)


def _pad_rank(w1, w2):
    r = w1.shape[1]
    rp = -(-r // LANES) * LANES
    return (jnp.pad(w1, ((0, 0), (0, rp - r))).astype(BF16),
            jnp.pad(w2, ((0, rp - r), (0, 0))).astype(BF16))


def _stack_heads(x, lane_lo):
    return jnp.concatenate([jnp.where(lane_lo, x, 0.0), jnp.where(lane_lo, 0.0, x)], axis=0)


def _rw_scan_kernel(*refs, has_vres):
    if has_vres:
        (rkv_ref, lw_ref, ag_ref, gg_ref, vf_ref, vg_ref,
         kk_ref, ka_ref, rk_ref, lnw_ref, lnb_ref, z_ref, st_ref) = refs
    else:
        (rkv_ref, lw_ref, ag_ref, gg_ref,
         kk_ref, ka_ref, rk_ref, lnw_ref, lnb_ref, z_ref, st_ref) = refs
    c = pl.program_id(1)
    npair = z_ref.shape[0]
    group = min(PAIR_GROUP, npair)
    assert npair % group == 0
    two_l = 2 * CHUNK

    @pl.when(c == 0)
    def _():
        st_ref[...] = jnp.zeros_like(st_ref)

    row = lax.broadcasted_iota(jnp.int32, (two_l, two_l), 0)
    col = lax.broadcasted_iota(jnp.int32, (two_l, two_l), 1)
    t_half = lax.broadcasted_iota(jnp.int32, (CHUNK, two_l), 0)
    s_half = lax.broadcasted_iota(jnp.int32, (CHUNK, two_l), 1) % CHUNK
    strict_t = t_half > s_half
    incl_t = t_half >= s_half
    same_head = (row // CHUNK) == (col // CHUNK)
    tril = jnp.where(lax.broadcasted_iota(jnp.int32, (CHUNK, CHUNK), 0)
                     >= lax.broadcasted_iota(jnp.int32, (CHUNK, CHUNK), 1), 1.0, 0.0).astype(BF16)
    diag = row == col
    same_block = lambda n: (row // n) == (col // n)
    assert INV_BASE_BLOCK == 4, "the base inverse (I + D)(I + D^2) needs D^4 = 0"
    base_mask = same_block(INV_BASE_BLOCK)
    merge_masks = [jnp.logical_and(same_block(2 * n), jnp.logical_not(same_block(n)))
                   for n in (INV_BASE_BLOCK << i for i in range((CHUNK // INV_BASE_BLOCK).bit_length() - 1))]
    lane_lo = lax.broadcasted_iota(jnp.int32, (CHUNK, LANES), 1) < RW_HEAD
    live = jnp.logical_or(c > 0, lax.broadcasted_iota(jnp.int32, (CHUNK, LANES), 0) >= N_DUMMY)

    def each(fn, *lists):
        return [fn(*xs) for xs in zip(*lists)]

    ones_bd = jnp.where(same_head, 1.0, 0.0).astype(BF16)

    def head_sum_lanes(x):
        lo = jnp.sum(jnp.where(lane_lo, x, 0.0), axis=-1, keepdims=True)
        hi = jnp.sum(jnp.where(lane_lo, 0.0, x), axis=-1, keepdims=True)
        return jnp.where(lane_lo, lo, hi)

    def head_sum_matmul(x):
        hi, lo = _split2(x)
        both = jnp.dot(jnp.concatenate([hi, lo], axis=0), ones_bd, preferred_element_type=F32)
        return both[:x.shape[0]] + both[x.shape[0]:]

    def swap_halves(x):
        return pltpu.roll(x, RW_HEAD, axis=1)

    def group_body(gi, carry):
        ps = [gi * group + i for i in range(group)]
        r = [rkv_ref[0, p] for p in ps]
        k0 = [rkv_ref[1, p] for p in ps]
        v = [rkv_ref[2, p] for p in ps]
        lw = [lw_ref[p] for p in ps]
        ag = [ag_ref[p] for p in ps]
        if has_vres:
            v = [vi + (vf_ref[0, p] - vi) * vg_ref[p] for vi, p in zip(v, ps)]
        kk = [ki * kk_ref[p] for ki, p in zip(k0, ps)]
        k = [ki * (1.0 + (ai - 1.0) * ka_ref[p]) for ki, ai, p in zip(k0, ag, ps)]
        kk = each(lambda kki: kki / jnp.maximum(jnp.sqrt(head_sum_lanes(kki * kki)), 1e-12), kk)
        bonus = [head_sum_lanes(ri * ki * rk_ref[p]) * vi for ri, ki, vi, p in zip(r, k, v, ps)]

        cs = each(lambda lwi: _dot_sel_left(tril, lwi), lw)
        e_pos = each(jnp.exp, cs)
        e_neg = each(lambda ci: jnp.exp(-ci), cs)
        w_all = each(lambda ei: ei[CHUNK - 1:CHUNK, :], e_pos)
        a_t = each(lambda kki, ci, lwi: -kki * jnp.exp(ci - lwi), kk, cs, lw)
        r_t = each(lambda ri, ei: ri * ei, r, e_pos)
        b_t = each(lambda kki, ai, ei: kki * ai * ei, kk, ag, e_neg)
        k_t = each(lambda ki, ei: ki * ei, k, e_neg)
        v_s = each(lambda vi: _stack_heads(vi, lane_lo).astype(BF16), v)
        b_s = each(lambda bti: _stack_heads(bti, lane_lo), b_t)
        k_s = each(lambda kti: _stack_heads(kti, lane_lo), k_t)
        bh_t = each(lambda bsi, wi: (bsi * wi).T.astype(BF16), b_s, w_all)
        kh_t = each(lambda ksi, wi: (ksi * wi).T.astype(BF16), k_s, w_all)

        gram = each(lambda ati, rti, bsi, ksi: _dot_nt(
            jnp.concatenate([ati, rti], axis=0), jnp.concatenate([bsi, ksi], axis=0)), a_t, r_t, b_s, k_s)

        block = lambda gm, keep, rows, cols: jnp.where(
            keep, gm[rows * CHUNK:(rows + 1) * CHUNK, cols * two_l:(cols + 1) * two_l], 0.0)
        p_c = each(lambda gm: block(gm, strict_t, 0, 0), gram)
        a_ak = each(lambda gm: block(gm, strict_t, 0, 1).astype(BF16), gram)
        a_rb = each(lambda gm: _stack_heads(block(gm, incl_t, 1, 0), lane_lo).astype(BF16), gram)
        a_rk = each(lambda gm: block(gm, incl_t, 1, 1).astype(BF16), gram)

        xv = each(lambda aki, ari, khi, vsi: jnp.dot(jnp.concatenate([aki, ari, khi], axis=0), vsi,
                                                     preferred_element_type=F32), a_ak, a_rk, kh_t, v_s)

        def z_init(ati, xi):
            su = swap_halves(xi[:CHUNK])
            return jnp.concatenate([jnp.where(lane_lo, ati, su), jnp.where(lane_lo, su, ati)], axis=0)

        z0 = each(z_init, a_t, xv)
        p_bd = each(lambda pci: _stack_heads(pci, lane_lo), p_c)
        d_bf = each(lambda pi: jnp.where(base_mask, pi, 0.0).astype(BF16), p_bd)
        d_sq = each(lambda di: jnp.dot(di, di, preferred_element_type=F32), d_bf)
        t_inv = each(lambda di, qi: jnp.dot(jnp.where(diag, 1.0, di.astype(F32)).astype(BF16),
                                            jnp.where(diag, 1.0, qi).astype(BF16),
                                            preferred_element_type=F32), d_bf, d_sq)
        for merge_mask in merge_masks:
            below = each(lambda pi: jnp.where(merge_mask, pi, 0.0).astype(BF16), p_bd)
            t_bf = each(lambda ti: ti.astype(BF16), t_inv)
            lt = each(lambda li, ti: jnp.dot(li, ti, preferred_element_type=F32).astype(BF16), below, t_bf)
            t_inv = each(lambda ti, tbi, lti: ti + jnp.dot(tbi, lti, preferred_element_type=F32), t_inv, t_bf, lt)

        def apply_inverse(ti, zi):
            z_hi, z_lo = _split2(zi)
            both = jnp.dot(ti.astype(BF16), jnp.concatenate([z_hi, z_lo], axis=1), preferred_element_type=F32)
            return (both[:, :LANES] + both[:, LANES:]).astype(BF16)

        zb = each(apply_inverse, t_inv, z0)

        rb = each(lambda ai, bhi, zi: jnp.dot(jnp.concatenate([ai, bhi], axis=0), zi,
                                              preferred_element_type=F32), a_rb, bh_t, zb)
        r_hat = each(lambda rti, rbi: rti + jnp.where(lane_lo, rbi[:CHUNK], rbi[CHUNK:two_l]), r_t, rb)
        y_hat = each(lambda rbi, xi: swap_halves(jnp.where(lane_lo, rbi[CHUNK:two_l], rbi[:CHUNK]))
                     + xi[CHUNK:two_l], rb, xv)
        m_mat = each(lambda wi, rbi: jnp.where(diag, wi, 0.0) + jnp.where(same_head, rbi[two_l:], 0.0),
                     w_all, rb)
        n_mat = each(lambda rbi, xi: swap_halves(jnp.where(same_head, 0.0, rbi[two_l:])) + xi[two_l:],
                     rb, xv)

        hb = [st_ref[p].astype(BF16) for p in ps]
        ys = each(lambda rh, mm, hi: jnp.dot(jnp.concatenate([rh, mm], axis=0).astype(BF16), hi,
                                             preferred_element_type=F32), r_hat, m_mat, hb)
        for p, ysi, nm in zip(ps, ys, n_mat):
            st_ref[p] = ysi[CHUNK:] + nm
        y = each(lambda ysi, yh: ysi[:CHUNK] + yh, ys, y_hat)

        dy = each(lambda yi: yi - head_sum_matmul(yi) * (1.0 / RW_HEAD), y)
        var = each(lambda di: head_sum_matmul(di * di) * (1.0 / RW_HEAD), dy)
        for p, di, vi, bi in zip(ps, dy, var, bonus):
            yn = di * lax.rsqrt(vi + RW_GN_EPS) * lnw_ref[p] + lnb_ref[p]
            z_ref[p] = jnp.where(live, (yn + bi) * gg_ref[p], 0.0).astype(z_ref.dtype)
        return carry

    lax.fori_loop(0, npair // group, group_body, 0)


def _rw_scan(rkv, lw, ag, gg, vres, kk, ka, rk, lnw, lnb, batch):
    _, npair, m, _ = rkv.shape
    nc = m // batch // CHUNK
    row_map3 = lambda bi, c: (0, bi * nc + c, 0)
    row_map4 = lambda bi, c: (0, 0, bi * nc + c, 0)
    par = pl.BlockSpec((npair, 1, LANES), lambda bi, c: (0, 0, 0))
    tile = pl.BlockSpec((npair, CHUNK, LANES), row_map3)
    in_specs = [pl.BlockSpec((3, npair, CHUNK, LANES), row_map4), tile, tile, tile]
    args = [rkv, lw, ag, gg]
    if vres is not None:
        vfirst, vgate = vres
        in_specs += [pl.BlockSpec((1, npair, CHUNK, LANES), lambda bi, c: (2, 0, bi * nc + c, 0)), tile]
        args += [vfirst, vgate]
    in_specs += [par] * 5
    args += [kk, ka, rk, lnw, lnb]
    tile_bytes = _nbytes((npair, CHUNK, LANES), F32)
    n_tiles = 3 + 3 + (2 if vres is not None else 0) + 1
    state_bytes = _nbytes((npair, 2 * CHUNK, LANES), F32)
    vmem = dict(moving=[n_tiles * tile_bytes], fixed=[state_bytes], live=8 * state_bytes)
    return pl.pallas_call(
        functools.partial(_rw_scan_kernel, has_vres=vres is not None),
        grid=(batch, nc),
        in_specs=in_specs,
        out_specs=tile,
        out_shape=jax.ShapeDtypeStruct((npair, m, LANES), BF16),
        scratch_shapes=[pltpu.VMEM((npair, 2 * CHUNK, LANES), F32)],
        compiler_params=_cparams(("parallel", "arbitrary"), **vmem),
        name="rw_scan",
    )(*args)


def _proj_kernel(x_ref, w_ref, h_ref, o_ref, *, pair_major):
    if pair_major:
        x = jnp.concatenate([x_ref[p] for p in range(x_ref.shape[0])], axis=1)
    else:
        x = x_ref[...]
    o_ref[...] = h_ref[...] + jnp.dot(x, w_ref[...], preferred_element_type=F32)


def _proj_residual(x, w, h, pair_major):
    m, d = h.shape
    kdim = w.shape[1]
    tm = _pick_tile(m, PROJ_ROWS)
    vmem = dict(moving=[_nbytes((tm, kdim), BF16), 2 * _nbytes((tm, d), F32)], fixed=[_nbytes((kdim, d), BF16)],
                live=_nbytes((tm, d), F32))
    if pair_major:
        x_spec = pl.BlockSpec((x.shape[0], tm, LANES), lambda i: (0, i, 0))
    else:
        x_spec = pl.BlockSpec((tm, kdim), lambda i: (i, 0))
    return pl.pallas_call(
        functools.partial(_proj_kernel, pair_major=pair_major),
        grid=(m // tm,),
        in_specs=[x_spec,
                  pl.BlockSpec((None, kdim, d), lambda i: (0, 0, 0)),
                  pl.BlockSpec((tm, d), lambda i: (i, 0))],
        out_specs=pl.BlockSpec((tm, d), lambda i: (i, 0)),
        out_shape=jax.ShapeDtypeStruct((m, d), F32),
        compiler_params=_cparams(("parallel",), **vmem),
        name="proj_residual",
    )(x, w, h)


def _gla_in_kernel(h_ref, g_ref, w_ref, wz_ref, wa_ref, ba_ref, p_ref, gl_ref, xn_ref):
    @pl.when(pl.program_id(1) == 0)
    def _():
        xn = _rms(h_ref[...], g_ref[...], NORM_EPS).astype(BF16)
        xn_ref[...] = xn
        za = jnp.dot(xn, wz_ref[...], preferred_element_type=F32)
        u = _dot(za, wa_ref[...]) + ba_ref[...]
        gl_ref[...] = (jnp.minimum(u, 0.0) - jnp.log1p(jnp.exp(-jnp.abs(u)))) * (1.0 / GLA_TAU)

    p_ref[...] = _dot_nt(xn_ref[...], w_ref[...]).astype(p_ref.dtype)


def _gla_in(h, g, w, n, wz, wa, ba):
    m, d = h.shape
    dk = wa.shape[1]
    tm = _pick_tile(m, GLA_IN_TILE[0])
    tn = _pick_tile(n, GLA_IN_TILE[1])
    vmem = dict(moving=[_nbytes((tm, d), F32), _nbytes((tn, d), BF16), _nbytes((tm, tn), BF16),
                        _nbytes((tm, dk), F32)],
                fixed=[_nbytes((tm, d), BF16), _nbytes((d, LANES), BF16), _nbytes((LANES, dk), F32)],
                live=_nbytes((tm, tn), F32) + 2 * _nbytes((tm, dk), F32))
    return pl.pallas_call(
        _gla_in_kernel,
        grid=(m // tm, n // tn),
        in_specs=[
            pl.BlockSpec((tm, d), lambda i, j: (i, 0)),
            pl.BlockSpec((1, d), lambda i, j: (0, 0)),
            pl.BlockSpec((None, tn, d), lambda i, j: (0, j, 0)),
            pl.BlockSpec((d, LANES), lambda i, j: (0, 0)),
            pl.BlockSpec((LANES, dk), lambda i, j: (0, 0)),
            pl.BlockSpec((1, dk), lambda i, j: (0, 0)),
        ],
        out_specs=[pl.BlockSpec((tm, tn), lambda i, j: (i, j)),
                   pl.BlockSpec((tm, dk), lambda i, j: (i, 0))],
        out_shape=[jax.ShapeDtypeStruct((m, n), BF16), jax.ShapeDtypeStruct((m, dk), F32)],
        scratch_shapes=[pltpu.VMEM((tm, d), BF16)],
        compiler_params=_cparams(("parallel", "arbitrary"), **vmem),
        name="gla_in",
    )(h, g, w, wz, wa, ba)


def _gla_chunk_kernel(q_ref, k_ref, v_ref, gate_ref, gl_ref, gn_ref, z_ref, st_ref, *, scale):
    c = pl.program_id(1)
    nb, nh, hv, hk = st_ref.shape

    @pl.when(c == 0)
    def _():
        st_ref[...] = jnp.zeros_like(st_ref)

    row = lax.broadcasted_iota(jnp.int32, (CHUNK, CHUNK), 0)
    col = lax.broadcasted_iota(jnp.int32, (CHUNK, CHUNK), 1)
    causal = row >= col
    tril = jnp.where(causal, 1.0, 0.0).astype(BF16)
    live = jnp.logical_or(c > 0, lax.broadcasted_iota(jnp.int32, gl_ref.shape[1:], 0) >= N_DUMMY)
    ksl = lambda x, h: x[:, h * hk:(h + 1) * hk]
    vsl = lambda x, h: x[:, h * hv:(h + 1) * hv]
    seqs = range(nb)
    cells = [(b, h) for b in seqs for h in range(nh)]

    gl = [jnp.where(live, gl_ref[b], 0.0) for b in seqs]
    bc = [_dot_sel_left(tril, gl[b]) for b in seqs]
    b_last = [bc[b][CHUNK - 1:CHUNK, :] for b in seqs]
    e_last = [jnp.exp(b_last[b]) for b in seqs]
    q_t = [(q_ref[b] * scale * jnp.exp(bc[b])).astype(BF16) for b in seqs]
    k_t = [(k_ref[b] * jnp.exp(-bc[b])).astype(BF16) for b in seqs]
    k_h = [(k_ref[b] * jnp.exp(b_last[b] - bc[b])).astype(BF16) for b in seqs]
    v = {(b, h): vsl(v_ref[b], h).astype(BF16) for b, h in cells}
    att = {(b, h): jnp.where(causal, _dot_nt(ksl(q_t[b], h), ksl(k_t[b], h)), 0.0).astype(BF16) for b, h in cells}
    st = {(b, h): st_ref[b, h] for b, h in cells}
    o = {(b, h): _dot_nt(ksl(q_t[b], h), st[b, h]) + jnp.dot(att[b, h], v[b, h], preferred_element_type=F32)
         for b, h in cells}
    for b, h in cells:
        st_ref[b, h] = st[b, h] * ksl(e_last[b], h) + _dot_tn(v[b, h], ksl(k_h[b], h))
    for b, h in cells:
        on = o[b, h] * lax.rsqrt(jnp.mean(o[b, h] * o[b, h], axis=-1, keepdims=True) + GLA_HEAD_EPS)
        gate = vsl(gate_ref[b], h).astype(F32)
        z_ref[b, :, h * hv:(h + 1) * hv] = (on * vsl(gn_ref[...], h) * (gate * _sigmoid(gate))).astype(z_ref.dtype)


def _gla_chunk(p, glog, gn_w, batch, d):
    tp = p.shape[0] // batch
    nc = tp // CHUNK
    dk = d // 2
    hk = dk // GLA_HEADS
    hv = d // GLA_HEADS
    p3 = p.reshape(batch, tp, p.shape[1])
    nb = GLA_SEQ_GROUP if batch % GLA_SEQ_GROUP == 0 else 1
    state_bytes = _nbytes((nb, GLA_HEADS, hv, hk), F32)
    vmem = dict(moving=[2 * _nbytes((nb, CHUNK, dk), BF16), 3 * _nbytes((nb, CHUNK, d), BF16),
                        _nbytes((nb, CHUNK, dk), F32)], fixed=[state_bytes], live=state_bytes)
    z = pl.pallas_call(
        functools.partial(_gla_chunk_kernel, scale=hk ** -0.5),
        grid=(batch // nb, nc),
        in_specs=[
            pl.BlockSpec((nb, CHUNK, dk), lambda g, c: (g, c, 0)),
            pl.BlockSpec((nb, CHUNK, dk), lambda g, c: (g, c, 1)),
            pl.BlockSpec((nb, CHUNK, d), lambda g, c: (g, c, 1)),
            pl.BlockSpec((nb, CHUNK, d), lambda g, c: (g, c, 2)),
            pl.BlockSpec((nb, CHUNK, dk), lambda g, c: (g, c, 0)),
            pl.BlockSpec((1, d), lambda g, c: (0, 0)),
        ],
        out_specs=pl.BlockSpec((nb, CHUNK, d), lambda g, c: (g, c, 0)),
        out_shape=jax.ShapeDtypeStruct((batch, tp, d), BF16),
        scratch_shapes=[pltpu.VMEM((nb, GLA_HEADS, hv, hk), F32)],
        compiler_params=_cparams(("parallel", "arbitrary"), **vmem),
        name="gla_chunk",
    )(p3, p3, p3, p3, glog.reshape(batch, tp, dk), gn_w)
    return z.reshape(batch * tp, d)


def _rwkv_layer(h, batch, j, v_first, norm_g, rw_mix, rw_w_rkv, rw_w0, rw_w1, rw_w2, rw_a0, rw_a1, rw_a2,
                rw_v0, rw_v1, rw_v2, rw_g1, rw_g2, rw_k_k, rw_k_a, rw_r_k, rw_ln_w, rw_ln_b, rw_w_o,
                sides=()):
    m, d = h.shape
    npair = d // LANES
    row = lambda t: t.reshape(1, d)
    pairs = lambda t: t.reshape(npair, 1, LANES)
    h3 = h.reshape(batch, m // batch, d)
    g = row(norm_g)
    rkv, cast = _rkv(h3, g, rw_mix[j], rw_w_rkv, sides)
    if rw_w_o is None:
        rw_w_o = cast[0]
    branches = [(1, "tanh", "decay"), (4, "none", "sigmoid"), (5, "sigmoid", "none")]
    weights = [_pad_rank(rw_w1[j], rw_w2[j]) + (row(rw_w0[j]),),
               _pad_rank(rw_a1[j], rw_a2[j]) + (row(rw_a0[j]),),
               _pad_rank(rw_g1[j], rw_g2[j]) + (jnp.zeros((1, d), F32),)]
    if j > 0:
        branches.append((3, "none", "sigmoid"))
        weights.append(_pad_rank(rw_v1[j - 1], rw_v2[j - 1]) + (row(rw_v0[j - 1]),))
    outs = _lora(h3, g, rw_mix[j], tuple(branches), weights)
    lw, ag, gg = outs[:3]
    vres = (v_first, outs[3]) if j > 0 else None
    z = _rw_scan(rkv, lw, ag, gg, vres, pairs(rw_k_k[j]), pairs(rw_k_a[j]), pairs(rw_r_k[j]),
                 pairs(rw_ln_w[j]), pairs(rw_ln_b[j]), batch)
    h = _proj_residual(z, rw_w_o, h, pair_major=True)
    return h, rkv, cast


def _gla_layer(h, batch, j, norm_g, gla_w_in_bf, gla_w_a2, gla_b_a, gla_gn_w, gla_w_o):
    m, d = h.shape
    dk = d // 2
    n_main = 2 * dk + 2 * d
    rank = gla_w_in_bf.shape[1] - n_main
    g = norm_g.reshape(1, d)
    wz = jnp.pad(gla_w_in_bf[0, n_main:, :].T, ((0, 0), (0, LANES - rank)))
    wa = jnp.pad(gla_w_a2[j], ((0, LANES - rank), (0, 0)))
    p, glog = _gla_in(h, g, gla_w_in_bf, n_main, wz, wa, gla_b_a[j].reshape(1, dk))
    z = _gla_chunk(p, glog, gla_gn_w[j].reshape(1, d), batch, d)
    return _proj_residual(z, gla_w_o, h, pair_major=False)


def kernel(x, meta, norm_mix, norm_mlp, norm_f, mlp_w1, mlp_w2, rw_mix, rw_w_rkv, rw_w0, rw_w1, rw_w2, rw_a0, rw_a1, rw_a2, rw_v0, rw_v1, rw_v2, rw_g1, rw_g2, rw_k_k, rw_k_a, rw_r_k, rw_ln_w, rw_ln_b, rw_w_o, gla_w_in, gla_w_a2, gla_b_a, gla_gn_w, gla_w_o):
    batch, seq, d = x.shape
    depth = norm_mix.shape[0]
    assert seq % CHUNK == 0 and d % (2 * LANES) == 0 and meta.shape[0] == N_META
    tp = LEAD + seq
    lead = jnp.concatenate([jnp.zeros((N_DUMMY, d), x.dtype), meta.astype(x.dtype)], axis=0)
    h = jnp.concatenate([jnp.broadcast_to(lead[None], (batch, LEAD, d)), x], axis=1).reshape(batch * tp, d)
    gf = norm_f.reshape(1, d)
    rw_w_rkv2 = rw_w_rkv.reshape(rw_w_rkv.shape[0], 3 * d, d)
    gla_w_in_t = jnp.transpose(gla_w_in, (0, 2, 1))

    def mixer_weights(i):
        j = i // 2
        mixer = [(rw_w_rkv2, j), (rw_w_o, j)] if i % 2 == 0 else [(gla_w_in_t, j), (gla_w_o, j)]
        return mixer + [(mlp_w1, i), (mlp_w2, i)]

    w_first, first_layer = mixer_weights(0)[0]
    cast = [w_first[first_layer:first_layer + 1].astype(BF16), None, None, None]
    v_first = None
    for i in range(depth):
        j = i // 2
        w_a, w_b, w1_bf, w2_bf = cast
        if i % 2 == 0:
            h, rkv, early = _rwkv_layer(h, batch, j, v_first, norm_mix[i], rw_mix, w_a.reshape(1, 3, d, d), rw_w0,
                                        rw_w1, rw_w2, rw_a0, rw_a1, rw_a2, rw_v0, rw_v1, rw_v2, rw_g1, rw_g2,
                                        rw_k_k, rw_k_a, rw_r_k.reshape(rw_r_k.shape[0], d), rw_ln_w, rw_ln_b, w_b,
                                        sides=mixer_weights(0)[1:] if i == 0 else ())
            if i == 0:
                _, w1_bf, w2_bf = early
            if j == 0:
                v_first = rkv
        else:
            h = _gla_layer(h, batch, j, norm_mix[i], w_a, gla_w_a2, gla_b_a, gla_gn_w, w_b)
        g_mlp = norm_mlp[i].reshape(1, d)
        if i == depth - 1:
            return _mlp_final(h.reshape(batch, tp, d), g_mlp, w1_bf, w2_bf, gf, *MLP_FINAL_TILE)
        h, cast = _mlp(h, g_mlp, w1_bf, w2_bf, gf, *MLP_TILE, sides=mixer_weights(i + 1))
```

```python
import functools
import math

import jax
import jax.numpy as jnp
from jax import lax
from jax.experimental import pallas as pl
from jax.experimental.pallas import tpu as pltpu

F32 = jnp.float32
BF16 = jnp.bfloat16

N_META = 16
CHUNK = 64
LEAD = CHUNK
N_DUMMY = LEAD - N_META
NORM_EPS = 1e-6

RW_HEAD = 64
LANES = 128
RW_GN_EPS = 64e-5
RW_DECAY_SCALE = 0.6065306597126334
PAIR_GROUP = 16
INV_BASE_BLOCK = 4

GLA_HEADS = 4
GLA_TAU = 16.0
GLA_HEAD_EPS = 1e-5
GLA_SEQ_GROUP = 4

V7X_VMEM_BYTES = 64 * 1024 * 1024
VMEM_REQUEST_CAP = V7X_VMEM_BYTES - 8 * 1024 * 1024

MLP_TILE = (832, 1024)
MLP_FINAL_TILE = (512, 1024)
RKV_ROWS = 208
LORA_ROWS = 416
PROJ_ROWS = 640
GLA_IN_TILE = (640, 2048)


def _nbytes(shape, dtype):
    return math.prod(shape) * jnp.dtype(dtype).itemsize


def _cparams(sem, moving, fixed=(), live=0):
    need = 2 * sum(moving) + sum(fixed) + live
    return pltpu.CompilerParams(dimension_semantics=sem, vmem_limit_bytes=min(need, VMEM_REQUEST_CAP))


def _dot(a, b):
    return jnp.dot(a.astype(BF16), b.astype(BF16), preferred_element_type=F32)


def _dot_nt(a, b):
    return lax.dot_general(a.astype(BF16), b.astype(BF16), (((1,), (1,)), ((), ())),
                           preferred_element_type=F32)


def _dot_tn(a, b):
    return lax.dot_general(a.astype(BF16), b.astype(BF16), (((0,), (0,)), ((), ())),
                           preferred_element_type=F32)


def _split2(x):
    hi = x.astype(BF16)
    return hi, (x - hi.astype(F32)).astype(BF16)


def _dot_sel_left(sel, x):
    hi, lo = _split2(x)
    return jnp.dot(sel, hi, preferred_element_type=F32) + jnp.dot(sel, lo, preferred_element_type=F32)


def _rms(x, g, eps):
    return x * lax.rsqrt(jnp.mean(x * x, axis=-1, keepdims=True) + eps) * g


def _sigmoid(x):
    return 1.0 / (1.0 + jnp.exp(-x))


def _pick_tile(n, target):
    best = None
    for t in range(16, min(n, target) + 1, 16):
        if n % t == 0:
            best = t
    assert best is not None, (n, target)
    return best


def _side_cast_specs(sides, n_steps, step_of):
    in_specs, out_specs, out_shapes, moving = [], [], [], []
    for arr, layer in sides:
        _, rows, cols = arr.shape
        blk = next(b for b in range(16, rows + 1, 16) if rows % b == 0 and rows // b <= n_steps)
        last = rows // blk - 1
        in_specs.append(pl.BlockSpec(
            (None, blk, cols), lambda *ids, layer=layer, last=last: (layer, jnp.minimum(step_of(*ids), last), 0)))
        out_specs.append(pl.BlockSpec(
            (None, blk, cols), lambda *ids, last=last: (0, jnp.minimum(step_of(*ids), last), 0)))
        out_shapes.append(jax.ShapeDtypeStruct((1, rows, cols), BF16))
        moving += [_nbytes((blk, cols), F32), _nbytes((blk, cols), BF16)]
    return in_specs, out_specs, out_shapes, moving


def _mlp_kernel(*refs, final, n_side):
    h_ref, g_ref, w1_ref, w2_ref, gf_ref = refs[:5]
    side_in = refs[5:5 + n_side]
    o_ref = refs[5 + n_side]
    side_out = refs[6 + n_side:6 + 2 * n_side]
    xn_ref = refs[6 + 2 * n_side]
    ff_axis = 2 if final else 1
    if final:
        o_ref = o_ref.at[0]
    j = pl.program_id(ff_axis)

    @pl.when(j == 0)
    def _():
        x = h_ref[...]
        xn_ref[...] = _rms(x, g_ref[...], NORM_EPS).astype(BF16)
        o_ref[...] = x

    hid = jnp.dot(xn_ref[...], w1_ref[...], preferred_element_type=F32)
    hid = jnp.maximum(hid, 0.0)
    hid = hid * hid
    o_ref[...] += jnp.dot(hid.astype(BF16), w2_ref[...], preferred_element_type=F32)
    for src, dst in zip(side_in, side_out):
        dst[...] = src[...].astype(BF16)

    if final:
        @pl.when(j == pl.num_programs(ff_axis) - 1)
        def _():
            o_ref[...] = _rms(o_ref[...], gf_ref[...], NORM_EPS)


def _mlp(h, g, w1, w2, gf, tm_target, tf_target, sides=()):
    m, d = h.shape
    ff = w1.shape[2]
    tm = _pick_tile(m, tm_target)
    tf = _pick_tile(ff, tf_target)
    nj = ff // tf
    side_in, side_out, side_shapes, side_bytes = _side_cast_specs(
        sides, (m // tm) * nj, lambda i, j: i * nj + j)
    vmem = dict(moving=[2 * _nbytes((tm, d), F32), 2 * _nbytes((d, tf), BF16)] + side_bytes,
                fixed=[_nbytes((tm, d), BF16)], live=_nbytes((tm, tf), F32) + _nbytes((tm, tf), BF16))
    outs = pl.pallas_call(
        functools.partial(_mlp_kernel, final=False, n_side=len(sides)),
        grid=(m // tm, nj),
        in_specs=[
            pl.BlockSpec((tm, d), lambda i, j: (i, 0)),
            pl.BlockSpec((1, d), lambda i, j: (0, 0)),
            pl.BlockSpec((None, d, tf), lambda i, j: (0, 0, j)),
            pl.BlockSpec((None, tf, d), lambda i, j: (0, j, 0)),
            pl.BlockSpec((1, d), lambda i, j: (0, 0)),
        ] + side_in,
        out_specs=[pl.BlockSpec((tm, d), lambda i, j: (i, 0))] + side_out,
        out_shape=[jax.ShapeDtypeStruct((m, d), F32)] + side_shapes,
        scratch_shapes=[pltpu.VMEM((tm, d), BF16)],
        compiler_params=_cparams(("arbitrary", "arbitrary"), **vmem),
        name="mlp",
    )(h, g, w1, w2, gf, *[arr for arr, _ in sides])
    return outs[0], outs[1:]


def _mlp_final(h3, g, w1, w2, gf, tm_target, tf_target):
    b, tp, d = h3.shape
    seq = tp - LEAD
    ff = w1.shape[2]
    tm = _pick_tile(seq, tm_target)
    tf = _pick_tile(ff, tf_target)
    vmem = dict(moving=[2 * _nbytes((tm, d), F32), 2 * _nbytes((d, tf), BF16)],
                fixed=[_nbytes((tm, d), BF16)], live=_nbytes((tm, tf), F32) + _nbytes((tm, tf), BF16))
    return pl.pallas_call(
        functools.partial(_mlp_kernel, final=True, n_side=0),
        grid=(b, seq // tm, ff // tf),
        in_specs=[
            pl.BlockSpec((pl.Element(tm), pl.Element(d)),
                         lambda bi, i, j: (pl.multiple_of(bi * tp + LEAD + i * tm, 16), 0)),
            pl.BlockSpec((1, d), lambda bi, i, j: (0, 0)),
            pl.BlockSpec((None, d, tf), lambda bi, i, j: (0, 0, j)),
            pl.BlockSpec((None, tf, d), lambda bi, i, j: (0, j, 0)),
            pl.BlockSpec((1, d), lambda bi, i, j: (0, 0)),
        ],
        out_specs=pl.BlockSpec((1, tm, d), lambda bi, i, j: (bi, i, 0)),
        out_shape=jax.ShapeDtypeStruct((b, seq, d), F32),
        scratch_shapes=[pltpu.VMEM((tm, d), BF16)],
        compiler_params=_cparams(("parallel", "parallel", "arbitrary"), **vmem),
        name="mlp_final",
    )(h3.reshape(b * tp, d), g, w1, w2, gf)


def _shifted_norm(h_ref, hp_ref, g_ref):
    g = g_ref[...]
    hn = _rms(h_ref[0], g, NORM_EPS)
    pn = _rms(hp_ref[0], g, NORM_EPS)[7:8]
    pn = jnp.where(pl.program_id(1) == 0, 0.0, pn)
    row = lax.broadcasted_iota(jnp.int32, hn.shape, 0)
    prev = jnp.where(row == 0, pn, pltpu.roll(hn, 1, axis=0))
    return hn, prev - hn


def _to_pairs(o_ref, lead, val):
    for p in range(val.shape[1] // LANES):
        o_ref[lead + (p,)] = val[:, p * LANES:(p + 1) * LANES].astype(o_ref.dtype)


def _shift_specs(tm, d):
    nb = tm // 8
    return [
        pl.BlockSpec((1, tm, d), lambda bi, i: (bi, i, 0)),
        pl.BlockSpec((1, 8, d), lambda bi, i: (bi, jnp.maximum(i * nb - 1, 0), 0)),
        pl.BlockSpec((1, d), lambda bi, i: (0, 0)),
        pl.BlockSpec((6, d), lambda bi, i: (0, 0)),
    ]


RKV_MIX_ROWS = (0, 2, 3)


def _rkv_kernel(*refs, n_side):
    h_ref, hp_ref, g_ref, mix_ref, w_ref = refs[:5]
    side_in = refs[5:5 + n_side]
    o_ref = refs[5 + n_side]
    side_out = refs[6 + n_side:]
    hn, xx = _shifted_norm(h_ref, hp_ref, g_ref)
    for s, mrow in enumerate(RKV_MIX_ROWS):
        x = (hn + xx * mix_ref[mrow:mrow + 1, :]).astype(BF16)
        _to_pairs(o_ref, (s,), jnp.dot(x, w_ref[s], preferred_element_type=F32))
    for src, dst in zip(side_in, side_out):
        dst[...] = src[...].astype(BF16)


def _rkv(h3, g, mix, w, sides=()):
    b, tp, d = h3.shape
    tm = _pick_tile(tp, RKV_ROWS)
    nt = tp // tm
    npair = d // LANES
    side_in, side_out, side_shapes, side_bytes = _side_cast_specs(sides, b * nt, lambda bi, i: bi * nt + i)
    vmem = dict(moving=[_nbytes((tm + 8, d), F32), 3 * _nbytes((tm, d), F32)] + side_bytes,
                fixed=[_nbytes((3, d, d), BF16)], live=4 * _nbytes((tm, d), F32))
    outs = pl.pallas_call(
        functools.partial(_rkv_kernel, n_side=len(sides)),
        grid=(b, nt),
        in_specs=_shift_specs(tm, d) + [
            pl.BlockSpec((None, 3, d, d), lambda bi, i: (0, 0, 0, 0), pipeline_mode=pl.Buffered(1)),
        ] + side_in,
        out_specs=[pl.BlockSpec((3, npair, tm, LANES), lambda bi, i: (0, 0, bi * nt + i, 0))] + side_out,
        out_shape=[jax.ShapeDtypeStruct((3, npair, b * tp, LANES), F32)] + side_shapes,
        compiler_params=_cparams(("arbitrary", "arbitrary"), **vmem),
        name="rw_rkv",
    )(h3, h3, g, mix, w, *[arr for arr, _ in sides])
    return outs[0], outs[1:]


def _lora_kernel(*refs, branches):
    h_ref, hp_ref, g_ref, mix_ref = refs[:4]
    nb = len(branches)
    w_refs = refs[4:4 + 3 * nb]
    o_refs = refs[4 + 3 * nb:]
    hn, xx = _shifted_norm(h_ref, hp_ref, g_ref)
    for bi, (mrow, mid, out) in enumerate(branches):
        w1_ref, w2_ref, b_ref = w_refs[3 * bi:3 * bi + 3]
        x = (hn + xx * mix_ref[mrow:mrow + 1, :]).astype(BF16)
        z = jnp.dot(x, w1_ref[...], preferred_element_type=F32)
        if mid == "tanh":
            z = jnp.tanh(z)
        elif mid == "sigmoid":
            z = _sigmoid(z)
        y = jnp.dot(z.astype(BF16), w2_ref[...], preferred_element_type=F32)
        if out == "decay":
            y = -RW_DECAY_SCALE * _sigmoid(y + b_ref[...])
        elif out == "sigmoid":
            y = _sigmoid(y + b_ref[...])
        _to_pairs(o_refs[bi], (), y)


def _lora(h3, g, mix, branches, weights):
    b, tp, d = h3.shape
    tm = _pick_tile(tp, LORA_ROWS)
    nt = tp // tm
    npair = d // LANES
    w_specs, w_args = [], []
    for w1, w2, bias in weights:
        r = w1.shape[1]
        w_specs += [pl.BlockSpec((d, r), lambda bi, i: (0, 0)),
                    pl.BlockSpec((r, d), lambda bi, i: (0, 0)),
                    pl.BlockSpec((1, d), lambda bi, i: (0, 0))]
        w_args += [w1, w2, bias]
    out_spec = pl.BlockSpec((npair, tm, LANES), lambda bi, i: (0, bi * nt + i, 0))
    out_shape = jax.ShapeDtypeStruct((npair, b * tp, LANES), F32)
    vmem = dict(moving=[_nbytes((tm + 8, d), F32)] + [_nbytes((tm, d), F32)] * len(branches),
                fixed=[_nbytes(w.shape, w.dtype) for w in w_args], live=4 * _nbytes((tm, d), F32))
    return pl.pallas_call(
        functools.partial(_lora_kernel, branches=branches),
        grid=(b, nt),
        in_specs=_shift_specs(tm, d) + w_specs,
        out_specs=[out_spec] * len(branches),
        out_shape=[out_shape] * len(branches),
        compiler_params=_cparams(("parallel", "parallel"), **vmem),
        name="rw_lora",
    )(h3, h3, g, mix, *w_args)


def _pad_rank(w1, w2):
    r = w1.shape[1]
    rp = -(-r // LANES) * LANES
    return (jnp.pad(w1, ((0, 0), (0, rp - r))).astype(BF16),
            jnp.pad(w2, ((0, rp - r), (0, 0))).astype(BF16))


def _stack_heads(x, lane_lo):
    return jnp.concatenate([jnp.where(lane_lo, x, 0.0), jnp.where(lane_lo, 0.0, x)], axis=0)


def _rw_scan_kernel(*refs, has_vres):
    if has_vres:
        (rkv_ref, lw_ref, ag_ref, gg_ref, vf_ref, vg_ref,
         kk_ref, ka_ref, rk_ref, lnw_ref, lnb_ref, z_ref, st_ref) = refs
    else:
        (rkv_ref, lw_ref, ag_ref, gg_ref,
         kk_ref, ka_ref, rk_ref, lnw_ref, lnb_ref, z_ref, st_ref) = refs
    c = pl.program_id(1)
    npair = z_ref.shape[0]
    group = min(PAIR_GROUP, npair)
    assert npair % group == 0
    two_l = 2 * CHUNK

    @pl.when(c == 0)
    def _():
        st_ref[...] = jnp.zeros_like(st_ref)

    row = lax.broadcasted_iota(jnp.int32, (two_l, two_l), 0)
    col = lax.broadcasted_iota(jnp.int32, (two_l, two_l), 1)
    t_half = lax.broadcasted_iota(jnp.int32, (CHUNK, two_l), 0)
    s_half = lax.broadcasted_iota(jnp.int32, (CHUNK, two_l), 1) % CHUNK
    strict_t = t_half > s_half
    incl_t = t_half >= s_half
    same_head = (row // CHUNK) == (col // CHUNK)
    tril = jnp.where(lax.broadcasted_iota(jnp.int32, (CHUNK, CHUNK), 0)
                     >= lax.broadcasted_iota(jnp.int32, (CHUNK, CHUNK), 1), 1.0, 0.0).astype(BF16)
    diag = row == col
    same_block = lambda n: (t_half // n) == (s_half // n)
    diag_t = t_half == s_half
    assert INV_BASE_BLOCK == 4, "the base inverse (I + D)(I + D^2) needs D^4 = 0"
    base_mask = same_block(INV_BASE_BLOCK)
    merge_masks = [jnp.logical_and(same_block(2 * n), jnp.logical_not(same_block(n)))
                   for n in (INV_BASE_BLOCK << i for i in range((CHUNK // INV_BASE_BLOCK).bit_length() - 1))]
    lane_lo = lax.broadcasted_iota(jnp.int32, (CHUNK, LANES), 1) < RW_HEAD
    live = jnp.logical_or(c > 0, lax.broadcasted_iota(jnp.int32, (CHUNK, LANES), 0) >= N_DUMMY)

    def each(fn, *lists):
        return [fn(*xs) for xs in zip(*lists)]

    ones_bd = jnp.where(same_head, 1.0, 0.0).astype(BF16)

    def head_sum_lanes(x):
        lo = jnp.sum(jnp.where(lane_lo, x, 0.0), axis=-1, keepdims=True)
        hi = jnp.sum(jnp.where(lane_lo, 0.0, x), axis=-1, keepdims=True)
        return jnp.where(lane_lo, lo, hi)

    def head_sum_matmul(x):
        hi, lo = _split2(x)
        both = jnp.dot(jnp.concatenate([hi, lo], axis=0), ones_bd, preferred_element_type=F32)
        return both[:x.shape[0]] + both[x.shape[0]:]

    def swap_halves(x):
        return pltpu.roll(x, RW_HEAD, axis=1)

    def group_body(gi, carry):
        ps = [gi * group + i for i in range(group)]
        r = [rkv_ref[0, p] for p in ps]
        k0 = [rkv_ref[1, p] for p in ps]
        v = [rkv_ref[2, p] for p in ps]
        lw = [lw_ref[p] for p in ps]
        ag = [ag_ref[p] for p in ps]
        if has_vres:
            v = [vi + (vf_ref[0, p] - vi) * vg_ref[p] for vi, p in zip(v, ps)]
        kk = [ki * kk_ref[p] for ki, p in zip(k0, ps)]
        k = [ki * (1.0 + (ai - 1.0) * ka_ref[p]) for ki, ai, p in zip(k0, ag, ps)]
        kk = each(lambda kki: kki / jnp.maximum(jnp.sqrt(head_sum_lanes(kki * kki)), 1e-12), kk)
        bonus = [head_sum_lanes(ri * ki * rk_ref[p]) * vi for ri, ki, vi, p in zip(r, k, v, ps)]

        cs = each(lambda lwi: _dot_sel_left(tril, lwi), lw)
        e_pos = each(jnp.exp, cs)
        e_neg = each(lambda ci: jnp.exp(-ci), cs)
        w_all = each(lambda ei: ei[CHUNK - 1:CHUNK, :], e_pos)
        a_t = each(lambda kki, ci, lwi: -kki * jnp.exp(ci - lwi), kk, cs, lw)
        r_t = each(lambda ri, ei: ri * ei, r, e_pos)
        b_t = each(lambda kki, ai, ei: kki * ai * ei, kk, ag, e_neg)
        k_t = each(lambda ki, ei: ki * ei, k, e_neg)
        v_s = each(lambda vi: _stack_heads(vi, lane_lo).astype(BF16), v)
        b_s = each(lambda bti: _stack_heads(bti, lane_lo), b_t)
        k_s = each(lambda kti: _stack_heads(kti, lane_lo), k_t)
        bh_t = each(lambda bsi, wi: (bsi * wi).T.astype(BF16), b_s, w_all)
        kh_t = each(lambda ksi, wi: (ksi * wi).T.astype(BF16), k_s, w_all)

        gram = each(lambda ati, rti, bsi, ksi: _dot_nt(
            jnp.concatenate([ati, rti], axis=0), jnp.concatenate([bsi, ksi], axis=0)), a_t, r_t, b_s, k_s)

        block = lambda gm, keep, rows, cols: jnp.where(
            keep, gm[rows * CHUNK:(rows + 1) * CHUNK, cols * two_l:(cols + 1) * two_l], 0.0)
        p_c = each(lambda gm: block(gm, strict_t, 0, 0), gram)
        a_ak = each(lambda gm: block(gm, strict_t, 0, 1).astype(BF16), gram)
        a_rb = each(lambda gm: _stack_heads(block(gm, incl_t, 1, 0), lane_lo).astype(BF16), gram)
        a_rk = each(lambda gm: block(gm, incl_t, 1, 1).astype(BF16), gram)

        xv = each(lambda aki, ari, khi, vsi: jnp.dot(jnp.concatenate([aki, ari, khi], axis=0), vsi,
                                                     preferred_element_type=F32), a_ak, a_rk, kh_t, v_s)

        def z_init(ati, xi):
            su = swap_halves(xi[:CHUNK])
            return jnp.concatenate([jnp.where(lane_lo, ati, su), jnp.where(lane_lo, su, ati)], axis=0)

        z0 = each(z_init, a_t, xv)
        stacked = lambda xc: _stack_heads(xc, lane_lo).astype(BF16)
        d_c = each(lambda pci: jnp.where(base_mask, pci, 0.0), p_c)
        d_sq = each(lambda di: jnp.dot(di.astype(BF16), stacked(di), preferred_element_type=F32), d_c)
        t_inv = each(lambda di, qi: jnp.dot(jnp.where(diag_t, 1.0, di).astype(BF16),
                                            stacked(jnp.where(diag_t, 1.0, qi)),
                                            preferred_element_type=F32), d_c, d_sq)
        for merge_mask in merge_masks:
            lt = each(lambda pci, ti: jnp.dot(jnp.where(merge_mask, pci, 0.0).astype(BF16), stacked(ti),
                                              preferred_element_type=F32), p_c, t_inv)
            t_inv = each(lambda ti, lti: ti + jnp.dot(ti.astype(BF16), stacked(lti), preferred_element_type=F32),
                         t_inv, lt)

        def apply_inverse(ti, zi):
            z_hi, z_lo = _split2(zi)
            both = jnp.dot(stacked(ti), jnp.concatenate([z_hi, z_lo], axis=1), preferred_element_type=F32)
            return (both[:, :LANES] + both[:, LANES:]).astype(BF16)

        zb = each(apply_inverse, t_inv, z0)

        rb = each(lambda ai, bhi, zi: jnp.dot(jnp.concatenate([ai, bhi], axis=0), zi,
                                              preferred_element_type=F32), a_rb, bh_t, zb)
        r_hat = each(lambda rti, rbi: rti + jnp.where(lane_lo, rbi[:CHUNK], rbi[CHUNK:two_l]), r_t, rb)
        y_hat = each(lambda rbi, xi: swap_halves(jnp.where(lane_lo, rbi[CHUNK:two_l], rbi[:CHUNK]))
                     + xi[CHUNK:two_l], rb, xv)
        m_mat = each(lambda wi, rbi: jnp.where(diag, wi, 0.0) + jnp.where(same_head, rbi[two_l:], 0.0),
                     w_all, rb)
        n_mat = each(lambda rbi, xi: swap_halves(jnp.where(same_head, 0.0, rbi[two_l:])) + xi[two_l:],
                     rb, xv)

        hb = [st_ref[p].astype(BF16) for p in ps]
        ys = each(lambda rh, mm, hi: jnp.dot(jnp.concatenate([rh, mm], axis=0).astype(BF16), hi,
                                             preferred_element_type=F32), r_hat, m_mat, hb)
        for p, ysi, nm in zip(ps, ys, n_mat):
            st_ref[p] = ysi[CHUNK:] + nm
        y = each(lambda ysi, yh: ysi[:CHUNK] + yh, ys, y_hat)

        dy = each(lambda yi: yi - head_sum_matmul(yi) * (1.0 / RW_HEAD), y)
        var = each(lambda di: head_sum_matmul(di * di) * (1.0 / RW_HEAD), dy)
        for p, di, vi, bi in zip(ps, dy, var, bonus):
            yn = di * lax.rsqrt(vi + RW_GN_EPS) * lnw_ref[p] + lnb_ref[p]
            z_ref[p] = jnp.where(live, (yn + bi) * gg_ref[p], 0.0).astype(z_ref.dtype)
        return carry

    lax.fori_loop(0, npair // group, group_body, 0)


def _rw_scan(rkv, lw, ag, gg, vres, kk, ka, rk, lnw, lnb, batch):
    _, npair, m, _ = rkv.shape
    nc = m // batch // CHUNK
    row_map3 = lambda bi, c: (0, bi * nc + c, 0)
    row_map4 = lambda bi, c: (0, 0, bi * nc + c, 0)
    par = pl.BlockSpec((npair, 1, LANES), lambda bi, c: (0, 0, 0))
    tile = pl.BlockSpec((npair, CHUNK, LANES), row_map3)
    in_specs = [pl.BlockSpec((3, npair, CHUNK, LANES), row_map4), tile, tile, tile]
    args = [rkv, lw, ag, gg]
    if vres is not None:
        vfirst, vgate = vres
        in_specs += [pl.BlockSpec((1, npair, CHUNK, LANES), lambda bi, c: (2, 0, bi * nc + c, 0)), tile]
        args += [vfirst, vgate]
    in_specs += [par] * 5
    args += [kk, ka, rk, lnw, lnb]
    tile_bytes = _nbytes((npair, CHUNK, LANES), F32)
    n_tiles = 3 + 3 + (2 if vres is not None else 0) + 1
    state_bytes = _nbytes((npair, 2 * CHUNK, LANES), F32)
    vmem = dict(moving=[n_tiles * tile_bytes], fixed=[state_bytes], live=8 * state_bytes)
    return pl.pallas_call(
        functools.partial(_rw_scan_kernel, has_vres=vres is not None),
        grid=(batch, nc),
        in_specs=in_specs,
        out_specs=tile,
        out_shape=jax.ShapeDtypeStruct((npair, m, LANES), BF16),
        scratch_shapes=[pltpu.VMEM((npair, 2 * CHUNK, LANES), F32)],
        compiler_params=_cparams(("parallel", "arbitrary"), **vmem),
        name="rw_scan",
    )(*args)


def _proj_kernel(x_ref, w_ref, h_ref, o_ref, *, pair_major):
    if pair_major:
        x = jnp.concatenate([x_ref[p] for p in range(x_ref.shape[0])], axis=1)
    else:
        x = x_ref[...]
    o_ref[...] = h_ref[...] + jnp.dot(x, w_ref[...], preferred_element_type=F32)


def _proj_residual(x, w, h, pair_major):
    m, d = h.shape
    kdim = w.shape[1]
    tm = _pick_tile(m, PROJ_ROWS)
    vmem = dict(moving=[_nbytes((tm, kdim), BF16), 2 * _nbytes((tm, d), F32)], fixed=[_nbytes((kdim, d), BF16)],
                live=_nbytes((tm, d), F32))
    if pair_major:
        x_spec = pl.BlockSpec((x.shape[0], tm, LANES), lambda i: (0, i, 0))
    else:
        x_spec = pl.BlockSpec((tm, kdim), lambda i: (i, 0))
    return pl.pallas_call(
        functools.partial(_proj_kernel, pair_major=pair_major),
        grid=(m // tm,),
        in_specs=[x_spec,
                  pl.BlockSpec((None, kdim, d), lambda i: (0, 0, 0)),
                  pl.BlockSpec((tm, d), lambda i: (i, 0))],
        out_specs=pl.BlockSpec((tm, d), lambda i: (i, 0)),
        out_shape=jax.ShapeDtypeStruct((m, d), F32),
        compiler_params=_cparams(("parallel",), **vmem),
        name="proj_residual",
    )(x, w, h)


def _gla_in_kernel(h_ref, g_ref, w_ref, wz_ref, wa_ref, ba_ref, p_ref, gl_ref, xn_ref):
    @pl.when(pl.program_id(1) == 0)
    def _():
        xn = _rms(h_ref[...], g_ref[...], NORM_EPS).astype(BF16)
        xn_ref[...] = xn
        za = jnp.dot(xn, wz_ref[...], preferred_element_type=F32)
        u = _dot(za, wa_ref[...]) + ba_ref[...]
        gl_ref[...] = (jnp.minimum(u, 0.0) - jnp.log1p(jnp.exp(-jnp.abs(u)))) * (1.0 / GLA_TAU)

    p_ref[...] = _dot_nt(xn_ref[...], w_ref[...]).astype(p_ref.dtype)


def _gla_in(h, g, w, n, wz, wa, ba):
    m, d = h.shape
    dk = wa.shape[1]
    tm = _pick_tile(m, GLA_IN_TILE[0])
    tn = _pick_tile(n, GLA_IN_TILE[1])
    vmem = dict(moving=[_nbytes((tm, d), F32), _nbytes((tn, d), BF16), _nbytes((tm, tn), BF16),
                        _nbytes((tm, dk), F32)],
                fixed=[_nbytes((tm, d), BF16), _nbytes((d, LANES), BF16), _nbytes((LANES, dk), F32)],
                live=_nbytes((tm, tn), F32) + 2 * _nbytes((tm, dk), F32))
    return pl.pallas_call(
        _gla_in_kernel,
        grid=(m // tm, n // tn),
        in_specs=[
            pl.BlockSpec((tm, d), lambda i, j: (i, 0)),
            pl.BlockSpec((1, d), lambda i, j: (0, 0)),
            pl.BlockSpec((None, tn, d), lambda i, j: (0, j, 0)),
            pl.BlockSpec((d, LANES), lambda i, j: (0, 0)),
            pl.BlockSpec((LANES, dk), lambda i, j: (0, 0)),
            pl.BlockSpec((1, dk), lambda i, j: (0, 0)),
        ],
        out_specs=[pl.BlockSpec((tm, tn), lambda i, j: (i, j)),
                   pl.BlockSpec((tm, dk), lambda i, j: (i, 0))],
        out_shape=[jax.ShapeDtypeStruct((m, n), BF16), jax.ShapeDtypeStruct((m, dk), F32)],
        scratch_shapes=[pltpu.VMEM((tm, d), BF16)],
        compiler_params=_cparams(("parallel", "arbitrary"), **vmem),
        name="gla_in",
    )(h, g, w, wz, wa, ba)


def _gla_chunk_kernel(q_ref, k_ref, v_ref, gate_ref, gl_ref, gn_ref, z_ref, st_ref, *, scale):
    c = pl.program_id(1)
    nb, nh, hv, hk = st_ref.shape

    @pl.when(c == 0)
    def _():
        st_ref[...] = jnp.zeros_like(st_ref)

    row = lax.broadcasted_iota(jnp.int32, (CHUNK, CHUNK), 0)
    col = lax.broadcasted_iota(jnp.int32, (CHUNK, CHUNK), 1)
    causal = row >= col
    tril = jnp.where(causal, 1.0, 0.0).astype(BF16)
    live = jnp.logical_or(c > 0, lax.broadcasted_iota(jnp.int32, gl_ref.shape[1:], 0) >= N_DUMMY)
    ksl = lambda x, h: x[:, h * hk:(h + 1) * hk]
    vsl = lambda x, h: x[:, h * hv:(h + 1) * hv]
    seqs = range(nb)
    cells = [(b, h) for b in seqs for h in range(nh)]

    gl = [jnp.where(live, gl_ref[b], 0.0) for b in seqs]
    bc = [_dot_sel_left(tril, gl[b]) for b in seqs]
    b_last = [bc[b][CHUNK - 1:CHUNK, :] for b in seqs]
    e_last = [jnp.exp(b_last[b]) for b in seqs]
    q_t = [(q_ref[b] * scale * jnp.exp(bc[b])).astype(BF16) for b in seqs]
    k_t = [(k_ref[b] * jnp.exp(-bc[b])).astype(BF16) for b in seqs]
    k_h = [(k_ref[b] * jnp.exp(b_last[b] - bc[b])).astype(BF16) for b in seqs]
    v = {(b, h): vsl(v_ref[b], h).astype(BF16) for b, h in cells}
    att = {(b, h): jnp.where(causal, _dot_nt(ksl(q_t[b], h), ksl(k_t[b], h)), 0.0).astype(BF16) for b, h in cells}
    st = {(b, h): st_ref[b, h] for b, h in cells}
    o = {(b, h): _dot_nt(ksl(q_t[b], h), st[b, h]) + jnp.dot(att[b, h], v[b, h], preferred_element_type=F32)
         for b, h in cells}
    for b, h in cells:
        st_ref[b, h] = st[b, h] * ksl(e_last[b], h) + _dot_tn(v[b, h], ksl(k_h[b], h))
    for b, h in cells:
        on = o[b, h] * lax.rsqrt(jnp.mean(o[b, h] * o[b, h], axis=-1, keepdims=True) + GLA_HEAD_EPS)
        gate = vsl(gate_ref[b], h).astype(F32)
        z_ref[b, :, h * hv:(h + 1) * hv] = (on * vsl(gn_ref[...], h) * (gate * _sigmoid(gate))).astype(z_ref.dtype)


def _gla_chunk(p, glog, gn_w, batch, d):
    tp = p.shape[0] // batch
    nc = tp // CHUNK
    dk = d // 2
    hk = dk // GLA_HEADS
    hv = d // GLA_HEADS
    p3 = p.reshape(batch, tp, p.shape[1])
    nb = GLA_SEQ_GROUP if batch % GLA_SEQ_GROUP == 0 else 1
    state_bytes = _nbytes((nb, GLA_HEADS, hv, hk), F32)
    vmem = dict(moving=[2 * _nbytes((nb, CHUNK, dk), BF16), 3 * _nbytes((nb, CHUNK, d), BF16),
                        _nbytes((nb, CHUNK, dk), F32)], fixed=[state_bytes], live=state_bytes)
    z = pl.pallas_call(
        functools.partial(_gla_chunk_kernel, scale=hk ** -0.5),
        grid=(batch // nb, nc),
        in_specs=[
            pl.BlockSpec((nb, CHUNK, dk), lambda g, c: (g, c, 0)),
            pl.BlockSpec((nb, CHUNK, dk), lambda g, c: (g, c, 1)),
            pl.BlockSpec((nb, CHUNK, d), lambda g, c: (g, c, 1)),
            pl.BlockSpec((nb, CHUNK, d), lambda g, c: (g, c, 2)),
            pl.BlockSpec((nb, CHUNK, dk), lambda g, c: (g, c, 0)),
            pl.BlockSpec((1, d), lambda g, c: (0, 0)),
        ],
        out_specs=pl.BlockSpec((nb, CHUNK, d), lambda g, c: (g, c, 0)),
        out_shape=jax.ShapeDtypeStruct((batch, tp, d), BF16),
        scratch_shapes=[pltpu.VMEM((nb, GLA_HEADS, hv, hk), F32)],
        compiler_params=_cparams(("parallel", "arbitrary"), **vmem),
        name="gla_chunk",
    )(p3, p3, p3, p3, glog.reshape(batch, tp, dk), gn_w)
    return z.reshape(batch * tp, d)


def _rwkv_layer(h, batch, j, v_first, norm_g, rw_mix, rw_w_rkv, rw_w0, rw_w1, rw_w2, rw_a0, rw_a1, rw_a2,
                rw_v0, rw_v1, rw_v2, rw_g1, rw_g2, rw_k_k, rw_k_a, rw_r_k, rw_ln_w, rw_ln_b, rw_w_o,
                sides=()):
    m, d = h.shape
    npair = d // LANES
    row = lambda t: t.reshape(1, d)
    pairs = lambda t: t.reshape(npair, 1, LANES)
    h3 = h.reshape(batch, m // batch, d)
    g = row(norm_g)
    rkv, cast = _rkv(h3, g, rw_mix[j], rw_w_rkv, sides)
    if rw_w_o is None:
        rw_w_o = cast[0]
    branches = [(1, "tanh", "decay"), (4, "none", "sigmoid"), (5, "sigmoid", "none")]
    weights = [_pad_rank(rw_w1[j], rw_w2[j]) + (row(rw_w0[j]),),
               _pad_rank(rw_a1[j], rw_a2[j]) + (row(rw_a0[j]),),
               _pad_rank(rw_g1[j], rw_g2[j]) + (jnp.zeros((1, d), F32),)]
    if j > 0:
        branches.append((3, "none", "sigmoid"))
        weights.append(_pad_rank(rw_v1[j - 1], rw_v2[j - 1]) + (row(rw_v0[j - 1]),))
    outs = _lora(h3, g, rw_mix[j], tuple(branches), weights)
    lw, ag, gg = outs[:3]
    vres = (v_first, outs[3]) if j > 0 else None
    z = _rw_scan(rkv, lw, ag, gg, vres, pairs(rw_k_k[j]), pairs(rw_k_a[j]), pairs(rw_r_k[j]),
                 pairs(rw_ln_w[j]), pairs(rw_ln_b[j]), batch)
    h = _proj_residual(z, rw_w_o, h, pair_major=True)
    return h, rkv, cast


def _gla_layer(h, batch, j, norm_g, gla_w_in_bf, gla_w_a2, gla_b_a, gla_gn_w, gla_w_o):
    m, d = h.shape
    dk = d // 2
    n_main = 2 * dk + 2 * d
    rank = gla_w_in_bf.shape[1] - n_main
    g = norm_g.reshape(1, d)
    wz = jnp.pad(gla_w_in_bf[0, n_main:, :].T, ((0, 0), (0, LANES - rank)))
    wa = jnp.pad(gla_w_a2[j], ((0, LANES - rank), (0, 0)))
    p, glog = _gla_in(h, g, gla_w_in_bf, n_main, wz, wa, gla_b_a[j].reshape(1, dk))
    z = _gla_chunk(p, glog, gla_gn_w[j].reshape(1, d), batch, d)
    return _proj_residual(z, gla_w_o, h, pair_major=False)


def kernel(x, meta, norm_mix, norm_mlp, norm_f, mlp_w1, mlp_w2, rw_mix, rw_w_rkv, rw_w0, rw_w1, rw_w2, rw_a0, rw_a1, rw_a2, rw_v0, rw_v1, rw_v2, rw_g1, rw_g2, rw_k_k, rw_k_a, rw_r_k, rw_ln_w, rw_ln_b, rw_w_o, gla_w_in, gla_w_a2, gla_b_a, gla_gn_w, gla_w_o):
    batch, seq, d = x.shape
    depth = norm_mix.shape[0]
    assert seq % CHUNK == 0 and d % (2 * LANES) == 0 and meta.shape[0] == N_META
    tp = LEAD + seq
    lead = jnp.concatenate([jnp.zeros((N_DUMMY, d), x.dtype), meta.astype(x.dtype)], axis=0)
    h = jnp.concatenate([jnp.broadcast_to(lead[None], (batch, LEAD, d)), x], axis=1).reshape(batch * tp, d)
    gf = norm_f.reshape(1, d)
    rw_w_rkv2 = rw_w_rkv.reshape(rw_w_rkv.shape[0], 3 * d, d)
    gla_w_in_t = jnp.transpose(gla_w_in, (0, 2, 1))

    def mixer_weights(i):
        j = i // 2
        mixer = [(rw_w_rkv2, j), (rw_w_o, j)] if i % 2 == 0 else [(gla_w_in_t, j), (gla_w_o, j)]
        return mixer + [(mlp_w1, i), (mlp_w2, i)]

    w_first, first_layer = mixer_weights(0)[0]
    cast = [w_first[first_layer:first_layer + 1].astype(BF16), None, None, None]
    v_first = None
    for i in range(depth):
        j = i // 2
        w_a, w_b, w1_bf, w2_bf = cast
        if i % 2 == 0:
            h, rkv, early = _rwkv_layer(h, batch, j, v_first, norm_mix[i], rw_mix, w_a.reshape(1, 3, d, d), rw_w0,
                                        rw_w1, rw_w2, rw_a0, rw_a1, rw_a2, rw_v0, rw_v1, rw_v2, rw_g1, rw_g2,
                                        rw_k_k, rw_k_a, rw_r_k.reshape(rw_r_k.shape[0], d), rw_ln_w, rw_ln_b, w_b,
                                        sides=mixer_weights(0)[1:] if i == 0 else ())
            if i == 0:
                _, w1_bf, w2_bf = early
            if j == 0:
                v_first = rkv
        else:
            h = _gla_layer(h, batch, j, norm_mix[i], w_a, gla_w_a2, gla_b_a, gla_gn_w, w_b)
        g_mlp = norm_mlp[i].reshape(1, d)
        if i == depth - 1:
            return _mlp_final(h.reshape(batch, tp, d), g_mlp, w1_bf, w2_bf, gf, *MLP_FINAL_TILE)
        h, cast = _mlp(h, g_mlp, w1_bf, w2_bf, gf, *MLP_TILE, sides=mixer_weights(i + 1))
```
